```python
import math
import jax, jax.numpy as jnp
from jax import lax
import numpy as np

D_MODEL = 1024
BATCH = 4
SEQ = 4096
DEPTH = 1
DEC_BATCH = 32
DEC_SEQ = 8
PAST_LEN = 8192
PAGE_SIZE = 128

ATTN_WIDTH = D_MODEL // 2
CONV_CH = D_MODEL - ATTN_WIDTH
N_HEADS = 4
DV = ATTN_WIDTH // N_HEADS
DQK = DV // 2
ROT_DIM = DQK // 4
ROPE_THETA = 500000.0
CONV_K = 31
N_EXPERTS = 32
TOP_K = 4
EXPERT_FF = D_MODEL
SWIGLU_LIMIT = 7.0
SWIGLU_ALPHA = 1.702
Q_BLOCK = 128
EPS = 1e-5
QK_W = 2 * N_HEADS * DQK
V_W = N_HEADS * DV
IN_WIDTH = 2 * QK_W + V_W + 2 * CONV_CH

kernel_name = 'hymba_diffattn_conformer_moe_step'


def lambda_init_fn(layer):
    return 0.8 - 0.6 * math.exp(-0.3 * layer)


def rmsnorm(x, g):
    xf = x.astype(jnp.float32)
    y = xf * lax.rsqrt(jnp.mean(xf * xf, axis=-1, keepdims=True) + EPS)
    return (y * g.astype(jnp.float32)).astype(x.dtype)


def adaln(c, w, b):
    m = (jax.nn.silu(c) @ w + b)[:, None, :]
    return jnp.split(m, 6, axis=-1)


def rope(x, pos):
    inv = ROPE_THETA ** (-jnp.arange(0, ROT_DIM, 2, dtype=jnp.float32) / ROT_DIM)
    ang = pos.astype(jnp.float32)[:, None] * inv
    cos = jnp.cos(ang)[:, None, :].astype(x.dtype)
    sin = jnp.sin(ang)[:, None, :].astype(x.dtype)
    x1 = x[..., : ROT_DIM // 2]
    x2 = x[..., ROT_DIM // 2: ROT_DIM]
    return jnp.concatenate([x1 * cos - x2 * sin, x2 * cos + x1 * sin, x[..., ROT_DIM:]], axis=-1)


def in_project(hn, w_in, pos):
    B, T = hn.shape[:2]
    proj = hn @ w_in
    q, k, v, a, gl = jnp.split(proj, [QK_W, 2 * QK_W, 2 * QK_W + V_W, 2 * QK_W + V_W + CONV_CH], axis=-1)
    q = rope(q.reshape(B, T, 2 * N_HEADS, DQK), pos) * (DQK ** -0.5)
    k = rope(k.reshape(B, T, 2 * N_HEADS, DQK), pos)
    v = v.reshape(B, T, N_HEADS, DV)
    u = a * jax.nn.sigmoid(gl)
    return q, k, v, u


def diff_attention(q, segments, lam):
    scores = []
    for k, _, m in segments:
        s = jnp.einsum('bqhd,bkhd->bhqk', q, k).astype(jnp.float32)
        scores.append(s if m is None else jnp.where(m, s, -jnp.inf))
    p = jax.nn.softmax(jnp.concatenate(scores, axis=-1), axis=-1)
    B, _, Q, KT = p.shape
    p = p.reshape(B, N_HEADS, 2, Q, KT)
    p = p[:, :, 0] - lam * p[:, :, 1]
    out, off = None, 0
    for k, v, _ in segments:
        n = k.shape[1]
        o = jnp.einsum('bhqk,bkhd->bqhd', p[..., off:off + n].astype(v.dtype), v)
        out = o if out is None else out + o
        off += n
    return out


def conv_tail(u_ext, w_dw, b_dw, ln_g, ln_b):
    y = lax.conv_general_dilated(u_ext, w_dw[:, None, :].astype(u_ext.dtype), (1,), 'VALID',
                                 dimension_numbers=('NWC', 'WIO', 'NWC'),
                                 feature_group_count=CONV_CH) + b_dw
    yf = y.astype(jnp.float32)
    mu = jnp.mean(yf, axis=-1, keepdims=True)
    var = jnp.mean(jnp.square(yf - mu), axis=-1, keepdims=True)
    yn = ((yf - mu) * lax.rsqrt(var + EPS) * ln_g.astype(jnp.float32) + ln_b.astype(jnp.float32)).astype(y.dtype)
    return jax.nn.silu(yn)


def merge_heads(o_attn, y_conv, w_out, subln_g, lam_init):
    B, T = o_attn.shape[:2]
    o = rmsnorm(o_attn, subln_g) * (1.0 - lam_init)
    return jnp.concatenate([o.reshape(B, T, ATTN_WIDTH), y_conv], axis=-1) @ w_out


def moe_ffn(x, w_router, b_router, w_gate_up, b_gate_up, w_down, b_down):
    logits = (x @ w_router + b_router).astype(jnp.float32)
    top_v, top_i = lax.top_k(logits, TOP_K)
    top_w = jax.nn.softmax(top_v, axis=-1)
    gates = jnp.sum(jax.nn.one_hot(top_i, N_EXPERTS, dtype=jnp.float32) * top_w[..., None], axis=1).astype(x.dtype)

    def expert(acc, p):
        w1, b1, w2, b2, g = p
        gu = x @ w1 + b1
        gate = jnp.minimum(gu[..., ::2], SWIGLU_LIMIT)
        up = jnp.clip(gu[..., 1::2], -SWIGLU_LIMIT, SWIGLU_LIMIT)
        y = ((up + 1) * gate * jax.nn.sigmoid(SWIGLU_ALPHA * gate)) @ w2 + b2
        return acc + g[:, None] * y, None

    out, _ = lax.scan(expert, jnp.zeros_like(x), (w_gate_up, b_gate_up, w_down, b_down, gates.T))
    return out


def setup_inputs(seed: int = 0) -> dict:
    key = jax.random.key(seed)
    ks = jax.random.split(key, 32)
    f32 = jnp.float32
    n_pages = PAST_LEN // PAGE_SIZE
    n_pool = (DEC_BATCH * n_pages * 5) // 4

    def nrm(k, shape, s):
        return jax.random.normal(k, shape, f32) * s

    page_table = jax.random.permutation(ks[0], n_pool)[: DEC_BATCH * n_pages].reshape(DEC_BATCH, n_pages).astype(jnp.int32)
    return {
        'x_prompt': nrm(ks[1], (BATCH, SEQ, D_MODEL), 1.0),
        'x_sample': nrm(ks[2], (DEC_BATCH, DEC_SEQ, D_MODEL), 1.0),
        'cache_k': nrm(ks[3], (DEPTH, n_pool, PAGE_SIZE, 2 * N_HEADS, DQK), 1.0),
        'cache_v': nrm(ks[4], (DEPTH, n_pool, PAGE_SIZE, N_HEADS, DV), 1.0),
        'state_conv': nrm(ks[5], (DEPTH, DEC_BATCH, CONV_K - 1, CONV_CH), 0.5),
        'page_table': page_table,
        'c_prompt': nrm(ks[6], (BATCH, D_MODEL), 1.0),
        'c_sample': nrm(ks[7], (DEC_BATCH, D_MODEL), 1.0),
        'norm1_g': 1.0 + nrm(ks[8], (DEPTH, D_MODEL), 0.02),
        'norm2_g': 1.0 + nrm(ks[9], (DEPTH, D_MODEL), 0.02),
        'w_ada': nrm(ks[10], (DEPTH, D_MODEL, 6 * D_MODEL), 0.5 * D_MODEL ** -0.5),
        'b_ada': nrm(ks[11], (DEPTH, 6 * D_MODEL), 0.02),
        'w_in': nrm(ks[12], (DEPTH, D_MODEL, IN_WIDTH), D_MODEL ** -0.5),
        'lambda_q1': nrm(ks[13], (DEPTH, DQK), 0.1),
        'lambda_k1': nrm(ks[14], (DEPTH, DQK), 0.1),
        'lambda_q2': nrm(ks[15], (DEPTH, DQK), 0.1),
        'lambda_k2': nrm(ks[16], (DEPTH, DQK), 0.1),
        'subln_g': 1.0 + nrm(ks[17], (DEPTH, DV), 0.02),
        'w_dw': nrm(ks[18], (DEPTH, CONV_K, CONV_CH), CONV_K ** -0.5),
        'b_dw': nrm(ks[19], (DEPTH, CONV_CH), 0.02),
        'conv_ln_g': 1.0 + nrm(ks[20], (DEPTH, CONV_CH), 0.02),
        'conv_ln_b': nrm(ks[21], (DEPTH, CONV_CH), 0.02),
        'w_out': nrm(ks[22], (DEPTH, ATTN_WIDTH + CONV_CH, D_MODEL), (ATTN_WIDTH + CONV_CH) ** -0.5),
        'w_router': nrm(ks[23], (DEPTH, D_MODEL, N_EXPERTS), D_MODEL ** -0.5),
        'b_router': nrm(ks[24], (DEPTH, N_EXPERTS), 0.01),
        'w_gate_up': nrm(ks[25], (DEPTH, N_EXPERTS, D_MODEL, 2 * EXPERT_FF), D_MODEL ** -0.5),
        'b_gate_up': nrm(ks[26], (DEPTH, N_EXPERTS, 2 * EXPERT_FF), 0.02),
        'w_down': nrm(ks[27], (DEPTH, N_EXPERTS, EXPERT_FF, D_MODEL), EXPERT_FF ** -0.5),
        'b_down': nrm(ks[28], (DEPTH, N_EXPERTS, D_MODEL), 0.02),
        'normf_g': 1.0 + nrm(ks[29], (D_MODEL,), 0.02),
    }


def reference(x_prompt, x_sample, cache_k, cache_v, state_conv, page_table, c_prompt, c_sample,
              norm1_g, norm2_g, w_ada, b_ada, w_in, lambda_q1, lambda_k1, lambda_q2, lambda_k2,
              subln_g, w_dw, b_dw, conv_ln_g, conv_ln_b, w_out, w_router, b_router,
              w_gate_up, b_gate_up, w_down, b_down, normf_g):
    B, T = x_prompt.shape[:2]
    DB, TS = x_sample.shape[:2]
    n_pages = page_table.shape[1]
    past = n_pages * cache_k.shape[2]
    pos_p = jnp.arange(T)
    pos_s = past + jnp.arange(TS)
    kpos_p = jnp.arange(T)
    n_blk = T // Q_BLOCK
    blk_starts = jnp.arange(n_blk) * Q_BLOCK
    mask_new = jnp.arange(TS)[:, None] >= jnp.arange(TS)[None, :]
    hp, hs = x_prompt, x_sample
    kp_l, vp_l, cp_l, ks_l, vs_l, cs_l = [], [], [], [], [], []

    for l in range(DEPTH):
        lam_init = lambda_init_fn(l)
        f32 = jnp.float32
        lam = (jnp.exp(jnp.sum(lambda_q1[l].astype(f32) * lambda_k1[l].astype(f32)))
               - jnp.exp(jnp.sum(lambda_q2[l].astype(f32) * lambda_k2[l].astype(f32))) + lam_init)
        sh1p, sc1p, g1p, sh2p, sc2p, g2p = adaln(c_prompt, w_ada[l], b_ada[l])
        sh1s, sc1s, g1s, sh2s, sc2s, g2s = adaln(c_sample, w_ada[l], b_ada[l])

        hn = rmsnorm(hp, norm1_g[l]) * (1 + sc1p) + sh1p
        q, k, v, u = in_project(hn, w_in[l], pos_p)
        q_blocks = q.reshape(B, n_blk, Q_BLOCK, 2 * N_HEADS, DQK).swapaxes(0, 1)

        def attend_block(args, k=k, v=v):
            qi, s0 = args
            m = (s0 + jnp.arange(Q_BLOCK))[:, None] >= kpos_p[None, :]
            return diff_attention(qi, [(k, v, m)], lam)

        o = lax.map(attend_block, (q_blocks, blk_starts))
        o = o.swapaxes(0, 1).reshape(B, T, N_HEADS, DV)
        u_ext = jnp.concatenate([jnp.zeros((B, CONV_K - 1, CONV_CH), u.dtype), u], axis=1)
        yc = conv_tail(u_ext, w_dw[l], b_dw[l], conv_ln_g[l], conv_ln_b[l])
        hp = hp + g1p * merge_heads(o, yc, w_out[l], subln_g[l], lam_init)
        kp_l.append(k)
        vp_l.append(v)
        cp_l.append(u_ext[:, -(CONV_K - 1):])

        hn = rmsnorm(hs, norm1_g[l]) * (1 + sc1s) + sh1s
        q, k, v, u = in_project(hn, w_in[l], pos_s)
        k_past = cache_k[l][page_table].reshape(DB, past, 2 * N_HEADS, DQK)
        v_past = cache_v[l][page_table].reshape(DB, past, N_HEADS, DV)
        o = diff_attention(q, [(k_past, v_past, None), (k, v, mask_new)], lam)
        u_ext = jnp.concatenate([state_conv[l].astype(u.dtype), u], axis=1)
        yc = conv_tail(u_ext, w_dw[l], b_dw[l], conv_ln_g[l], conv_ln_b[l])
        hs = hs + g1s * merge_heads(o, yc, w_out[l], subln_g[l], lam_init)
        ks_l.append(k)
        vs_l.append(v)
        cs_l.append(u_ext[:, -(CONV_K - 1):])

        n2p = rmsnorm(hp, norm2_g[l]) * (1 + sc2p) + sh2p
        n2s = rmsnorm(hs, norm2_g[l]) * (1 + sc2s) + sh2s
        flat = jnp.concatenate([n2p.reshape(B * T, D_MODEL), n2s.reshape(DB * TS, D_MODEL)], axis=0)
        f = moe_ffn(flat, w_router[l], b_router[l], w_gate_up[l], b_gate_up[l], w_down[l], b_down[l])
        hp = hp + g2p * f[: B * T].reshape(B, T, D_MODEL)
        hs = hs + g2s * f[B * T:].reshape(DB, TS, D_MODEL)

    y_prompt = rmsnorm(hp, normf_g)
    y_sample = rmsnorm(hs, normf_g)
    k_prompt = jnp.stack(kp_l)
    v_prompt = jnp.stack(vp_l)
    conv_prompt = jnp.stack(cp_l)
    k_sample = jnp.stack(ks_l)
    v_sample = jnp.stack(vs_l)
    conv_sample = jnp.stack(cs_l)
    return (y_prompt, y_sample, k_prompt, v_prompt, conv_prompt, k_sample, v_sample, conv_sample)
```

```python
import functools
import math

import jax
import jax.numpy as jnp
from jax import lax
from jax.experimental import pallas as pl
from jax.experimental.pallas import tpu as pltpu

F32 = jnp.float32
BF16 = jnp.bfloat16
I32 = jnp.int32
HIGHEST = lax.Precision.HIGHEST

D_MODEL = 1024
N_HEADS = 4
DV = 128
DQK = 64
ROT_DIM = 16
ROPE_THETA = 500000.0
CONV_K = 31
CONV_CH = 512
QK_W = 512
V_W = 512
IN_WIDTH = 2 * QK_W + V_W + 2 * CONV_CH
N_EXPERTS = 32
TOP_K = 4
SWIGLU_LIMIT = 7.0
SWIGLU_ALPHA = 1.702
EPS = 1e-5
LAM_INIT = 0.8 - 0.6 * math.exp(-0.3 * 0)
PAGE_SIZE = 128

LANES = 128
SUBLANES = 8
VMEM_LIMIT = 48 * 1024 * 1024

TM = 256
TQ = 512
TK = 512
PAGES_PER_STEP = 8
HALO = 32
CONV_CHUNK = 64
NEG_INF = float("-inf")


def _cparams(sem):
    return pltpu.CompilerParams(dimension_semantics=sem, vmem_limit_bytes=VMEM_LIMIT)


def _adaln_kernel(c_ref, w_ref, b_ref, o_ref):
    c = c_ref[...]
    s = c * jax.nn.sigmoid(c)
    o_ref[...] = jnp.dot(s, w_ref[...], precision=HIGHEST, preferred_element_type=F32) + b_ref[...]


def _adaln(c_all, w, b):
    n, d = c_all.shape
    width = w.shape[1]
    bn = 1536
    return pl.pallas_call(
        _adaln_kernel,
        grid=(width // bn,),
        in_specs=[
            pl.BlockSpec((n, d), lambda j: (0, 0)),
            pl.BlockSpec((d, bn), lambda j: (0, j)),
            pl.BlockSpec((1, bn), lambda j: (0, j)),
        ],
        out_specs=pl.BlockSpec((n, bn), lambda j: (0, j)),
        out_shape=jax.ShapeDtypeStruct((n, width), F32),
        compiler_params=_cparams(("parallel",)),
        name="adaln",
    )(c_all, w, b.reshape(1, width))


def _rope_tables(pos):
    inv = ROPE_THETA ** (-jnp.arange(0, ROT_DIM, 2, dtype=F32) / ROT_DIM)
    ang = pos.astype(F32)[:, None] * inv
    cos, sin = jnp.cos(ang), jnp.sin(ang)
    half = ROT_DIM // 2
    ones = jnp.ones((pos.shape[0], DQK - ROT_DIM), F32)
    zeros_h = jnp.zeros((pos.shape[0], half), F32)
    zeros_r = jnp.zeros((pos.shape[0], DQK - ROT_DIM), F32)
    c64 = jnp.concatenate([cos, cos, ones], axis=1)
    a64 = jnp.concatenate([-sin, zeros_h, zeros_r], axis=1)
    b64 = jnp.concatenate([zeros_h, sin, zeros_r], axis=1)
    rep = LANES // DQK
    return jnp.tile(c64, (1, rep)), jnp.tile(a64, (1, rep)), jnp.tile(b64, (1, rep))


def _inproj_kernel(x_ref, g_ref, sc_ref, sh_ref, cos_ref, sa_ref, sb_ref, w_ref, *out_refs, head_major):
    x = x_ref[...]
    hn = x * lax.rsqrt(jnp.mean(x * x, axis=-1, keepdims=True) + EPS) * g_ref[...]
    hn = hn * (1.0 + sc_ref[0]) + sh_ref[0]
    proj = jnp.dot(hn.astype(BF16), w_ref[...], preferred_element_type=F32)
    cos, sa, sb = cos_ref[...], sa_ref[...], sb_ref[...]

    def rope(blk):
        return blk * cos + pltpu.roll(blk, LANES - ROT_DIM // 2, 1) * sa + pltpu.roll(blk, ROT_DIM // 2, 1) * sb

    a = proj[:, 2 * QK_W + V_W: 2 * QK_W + V_W + CONV_CH]
    gl = proj[:, 2 * QK_W + V_W + CONV_CH:]
    if head_major:
        qb_ref, k32_ref, v32_ref, kb_ref, vb_ref, u_ref = out_refs
    else:
        q32_ref, k32_ref, v32_ref, u_ref = out_refs
    u_ref[...] = a * jax.nn.sigmoid(gl)
    for h in range(N_HEADS):
        lo, hi = h * LANES, (h + 1) * LANES
        qh = rope(proj[:, lo:hi]) * (DQK ** -0.5)
        kh = rope(proj[:, QK_W + lo: QK_W + hi])
        vh = proj[:, 2 * QK_W + lo: 2 * QK_W + hi]
        k32_ref[:, lo:hi] = kh
        v32_ref[:, lo:hi] = vh
        if head_major:
            qb_ref[h] = qh.astype(BF16)
            kb_ref[h] = kh.astype(BF16)
            vb_ref[h] = vh.astype(BF16)
        else:
            q32_ref[:, lo:hi] = qh


def _inproj(x, norm_g, sc, sh, tabs, w_in_bf16, tiles_per_group, tiles_per_seq, head_major):
    n = x.shape[0]
    nt = n // TM
    r = sc.shape[1]
    row = lambda i: (i, 0)
    grp = lambda i: (i // tiles_per_group, 0, 0)
    tab = lambda i: (i % tiles_per_seq, 0)
    in_specs = [
        pl.BlockSpec((TM, D_MODEL), row),
        pl.BlockSpec((1, D_MODEL), lambda i: (0, 0)),
        pl.BlockSpec((1, r, D_MODEL), grp),
        pl.BlockSpec((1, r, D_MODEL), grp),
        pl.BlockSpec((TM, LANES), tab),
        pl.BlockSpec((TM, LANES), tab),
        pl.BlockSpec((TM, LANES), tab),
        pl.BlockSpec((D_MODEL, IN_WIDTH), lambda i: (0, 0)),
    ]
    wide = pl.BlockSpec((TM, QK_W), row)
    hm = pl.BlockSpec((N_HEADS, TM, LANES), lambda i: (0, i, 0))
    if head_major:
        out_specs = [hm, wide, wide, hm, hm, wide]
        out_shape = [
            jax.ShapeDtypeStruct((N_HEADS, n, LANES), BF16),
            jax.ShapeDtypeStruct((n, QK_W), F32),
            jax.ShapeDtypeStruct((n, V_W), F32),
            jax.ShapeDtypeStruct((N_HEADS, n, LANES), BF16),
            jax.ShapeDtypeStruct((N_HEADS, n, LANES), BF16),
            jax.ShapeDtypeStruct((n, CONV_CH), F32),
        ]
    else:
        out_specs = [wide, wide, wide, wide]
        out_shape = [jax.ShapeDtypeStruct((n, QK_W), F32)] * 4
    return pl.pallas_call(
        functools.partial(_inproj_kernel, head_major=head_major),
        grid=(nt,),
        in_specs=in_specs,
        out_specs=out_specs,
        out_shape=out_shape,
        compiler_params=_cparams(("parallel",)),
        name="inproj_hm" if head_major else "inproj",
    )(x, norm_g.reshape(1, D_MODEL), sc, sh, *tabs, w_in_bf16)


def _lambda_value(lq1, lk1, lq2, lk2):
    a = jnp.exp(jnp.sum(lq1[...] * lk1[...], axis=-1, keepdims=True))
    b = jnp.exp(jnp.sum(lq2[...] * lk2[...], axis=-1, keepdims=True))
    return a - b + LAM_INIT


def _diff_merge(o1, l1, o2, l2, lam, subln_g):
    o = o1 / l1 - lam * (o2 / l2)
    o = o * lax.rsqrt(jnp.mean(o * o, axis=-1, keepdims=True) + EPS) * subln_g
    return o * (1.0 - LAM_INIT)


def _attn_kernel(qt_ref, kt_ref, q_ref, k_ref, v_ref, lq1, lk1, lq2, lk2, sg_ref, o_ref, qs, m_s, l_s, acc):
    s_idx = pl.program_id(2)
    qi = qt_ref[s_idx]
    ki = kt_ref[s_idx]

    @pl.when(ki == 0)
    def _():
        q = q_ref[0]
        lane = lax.broadcasted_iota(I32, q.shape, 1)
        zero = jnp.zeros_like(q)
        qs[0:TQ, :] = jnp.where(lane < DQK, q, zero)
        qs[TQ:2 * TQ, :] = jnp.where(lane >= DQK, q, zero)
        m_s[...] = jnp.full(m_s.shape, NEG_INF, F32)
        l_s[...] = jnp.zeros(l_s.shape, F32)
        acc[...] = jnp.zeros(acc.shape, F32)

    def update(masked):
        s = lax.dot_general(qs[...], k_ref[0], (((1,), (1,)), ((), ())), preferred_element_type=F32)
        if masked:
            row = lax.broadcasted_iota(I32, s.shape, 0) & (TQ - 1)
            col = lax.broadcasted_iota(I32, s.shape, 1)
            s = jnp.where(row >= col, s, NEG_INF)
        m_prev = m_s[...]
        m_next = jnp.maximum(m_prev, jnp.max(s, axis=1, keepdims=True))
        p = jnp.exp(s - jnp.tile(m_next, (1, TK // LANES)))
        alpha = jnp.exp(m_prev - m_next)
        l_s[...] = alpha * l_s[...] + jnp.sum(p, axis=1, keepdims=True)
        acc[...] = alpha * acc[...] + jnp.dot(p.astype(BF16), v_ref[0], preferred_element_type=F32)
        m_s[...] = m_next

    @pl.when(ki < qi)
    def _():
        update(False)

    @pl.when(ki == qi)
    def _():
        update(True)
        lam = _lambda_value(lq1, lk1, lq2, lk2)
        o = _diff_merge(acc[0:TQ, :], l_s[0:TQ, :], acc[TQ:2 * TQ, :], l_s[TQ:2 * TQ, :], lam, sg_ref[...])
        o_ref[...] = o.astype(o_ref.dtype)


def _prompt_attention(q_hm, k_hm, v_hm, lams, subln_g, batch, seq):
    nq = seq // TQ
    pairs = [(qi, ki) for qi in range(nq) for ki in range(qi + 1)]
    qt = jnp.asarray([p[0] for p in pairs], I32)
    kt = jnp.asarray([p[1] for p in pairs], I32)
    n = batch * seq
    vec = lambda b, h, s, qt, kt: (0, 0)
    grid_spec = pltpu.PrefetchScalarGridSpec(
        num_scalar_prefetch=2,
        grid=(batch, N_HEADS, len(pairs)),
        in_specs=[
            pl.BlockSpec((1, TQ, LANES), lambda b, h, s, qt, kt: (h, b * nq + qt[s], 0)),
            pl.BlockSpec((1, TK, LANES), lambda b, h, s, qt, kt: (h, b * nq + kt[s], 0)),
            pl.BlockSpec((1, TK, LANES), lambda b, h, s, qt, kt: (h, b * nq + kt[s], 0)),
            pl.BlockSpec((1, DQK), vec), pl.BlockSpec((1, DQK), vec),
            pl.BlockSpec((1, DQK), vec), pl.BlockSpec((1, DQK), vec),
            pl.BlockSpec((1, DV), vec),
        ],
        out_specs=pl.BlockSpec((TQ, LANES), lambda b, h, s, qt, kt: (b * nq + qt[s], h)),
        scratch_shapes=[
            pltpu.VMEM((2 * TQ, LANES), BF16),
            pltpu.VMEM((2 * TQ, LANES), F32),
            pltpu.VMEM((2 * TQ, LANES), F32),
            pltpu.VMEM((2 * TQ, LANES), F32),
        ],
    )
    return pl.pallas_call(
        _attn_kernel,
        grid_spec=grid_spec,
        out_shape=jax.ShapeDtypeStruct((n, N_HEADS * DV), BF16),
        compiler_params=_cparams(("parallel", "parallel", "arbitrary")),
        name="prompt_attn",
    )(qt, kt, q_hm, k_hm, v_hm, *lams, subln_g.reshape(1, DV))


def _paged_attn_kernel(pt_ref, q_ref, kn_ref, vn_ref, lq1, lk1, lq2, lk2, sg_ref, *rest, n_steps, ts):
    kp = rest[:PAGES_PER_STEP]
    vp = rest[PAGES_PER_STEP:2 * PAGES_PER_STEP]
    o_ref, qe, m_s, l_s, acc = rest[2 * PAGES_PER_STEP:]
    n_maps = 2 * N_HEADS
    rows = n_maps * ts
    p_idx = pl.program_id(1)

    @pl.when(p_idx == 0)
    def _():
        qt = jnp.concatenate([q_ref[...]] * n_maps, axis=0)
        row = lax.broadcasted_iota(I32, qt.shape, 0)
        col = lax.broadcasted_iota(I32, qt.shape, 1)
        same_map = (row >> (ts.bit_length() - 1)) == (col >> (DQK.bit_length() - 1))
        qe[...] = jnp.where(same_map, qt, 0.0).astype(BF16)
        m_s[...] = jnp.full(m_s.shape, NEG_INF, F32)
        l_s[...] = jnp.zeros(l_s.shape, F32)
        acc[...] = jnp.zeros(acc.shape, F32)

    def update(kmat, vmat, causal):
        s = lax.dot_general(qe[...], kmat, (((1,), (1,)), ((), ())), preferred_element_type=F32)
        if causal:
            row = lax.broadcasted_iota(I32, s.shape, 0) & (ts - 1)
            col = lax.broadcasted_iota(I32, s.shape, 1)
            s = jnp.where(row >= col, s, NEG_INF)
        m_prev = m_s[...]
        m_next = jnp.maximum(m_prev, jnp.max(s, axis=1, keepdims=True))
        p = jnp.exp(s - m_next[:, 0:1])
        alpha = jnp.exp(m_prev - m_next)
        l_s[...] = alpha * l_s[...] + jnp.sum(p, axis=1, keepdims=True)
        acc[...] = jnp.tile(alpha, (1, V_W // LANES)) * acc[...] + jnp.dot(
            p.astype(BF16), vmat, preferred_element_type=F32)
        m_s[...] = m_next

    @pl.when(p_idx < n_steps)
    def _():
        kmat = jnp.concatenate([r[...] for r in kp], axis=0).astype(BF16)
        vmat = jnp.concatenate([r[...] for r in vp], axis=0).astype(BF16)
        update(kmat, vmat, False)

    @pl.when(p_idx == n_steps)
    def _():
        update(kn_ref[...].astype(BF16), vn_ref[...].astype(BF16), True)
        lam = _lambda_value(lq1, lk1, lq2, lk2)
        for h in range(N_HEADS):
            r1, r2 = 2 * h * ts, (2 * h + 1) * ts
            c0, c1 = h * DV, (h + 1) * DV
            o = _diff_merge(acc[r1:r1 + ts, c0:c1], l_s[r1:r1 + ts, :],
                            acc[r2:r2 + ts, c0:c1], l_s[r2:r2 + ts, :], lam, sg_ref[...])
            o_ref[:, c0:c1] = o


def _sample_attention(q_s, k_s, v_s, cache_k, cache_v, page_table, lams, subln_g):
    db, n_pages = page_table.shape
    ts = q_s.shape[0] // db
    n_pool = cache_k.shape[0]
    ck = cache_k.reshape(n_pool, PAGE_SIZE, QK_W)
    cv = cache_v.reshape(n_pool, PAGE_SIZE, V_W)
    n_steps = n_pages // PAGES_PER_STEP
    pt = page_table.reshape(-1).astype(I32)
    vec = lambda b, p, pt: (0, 0)
    new = lambda b, p, pt: (b, 0)

    def page_spec(j):
        def idx(b, p, pt):
            step = jnp.minimum(p, n_steps - 1)
            return (pt[b * n_pages + step * PAGES_PER_STEP + j], 0, 0)
        return pl.BlockSpec((None, PAGE_SIZE, QK_W), idx)

    rows = 2 * N_HEADS * ts
    grid_spec = pltpu.PrefetchScalarGridSpec(
        num_scalar_prefetch=1,
        grid=(db, n_steps + 1),
        in_specs=[
            pl.BlockSpec((ts, QK_W), new), pl.BlockSpec((ts, QK_W), new), pl.BlockSpec((ts, V_W), new),
            pl.BlockSpec((1, DQK), vec), pl.BlockSpec((1, DQK), vec),
            pl.BlockSpec((1, DQK), vec), pl.BlockSpec((1, DQK), vec),
            pl.BlockSpec((1, DV), vec),
        ] + [page_spec(j) for j in range(PAGES_PER_STEP)] + [page_spec(j) for j in range(PAGES_PER_STEP)],
        out_specs=pl.BlockSpec((ts, V_W), new),
        scratch_shapes=[
            pltpu.VMEM((rows, QK_W), BF16),
            pltpu.VMEM((rows, LANES), F32),
            pltpu.VMEM((rows, LANES), F32),
            pltpu.VMEM((rows, V_W), F32),
        ],
    )
    return pl.pallas_call(
        functools.partial(_paged_attn_kernel, n_steps=n_steps, ts=ts),
        grid_spec=grid_spec,
        out_shape=jax.ShapeDtypeStruct((db * ts, V_W), F32),
        compiler_params=_cparams(("parallel", "arbitrary")),
        name="paged_attn",
    )(pt, q_s, k_s, v_s, *lams, subln_g.reshape(1, DV), *([ck] * PAGES_PER_STEP), *([cv] * PAGES_PER_STEP))


def _conv_ln_swish(y, lng, lnb):
    mu = jnp.mean(y, axis=-1, keepdims=True)
    var = jnp.mean(jnp.square(y - mu), axis=-1, keepdims=True)
    yn = (y - mu) * lax.rsqrt(var + EPS) * lng + lnb
    return yn * jax.nn.sigmoid(yn)


N_MIX_PROMPT_INPUTS = 16


def _mix_kernel(*refs, prompt, tiles_per_seq, n_real):
    i = pl.program_id(0)

    @pl.when(i < n_real)
    def _():
        _mix_body(*refs, prompt=prompt, tiles_per_seq=tiles_per_seq)

    if prompt:
        @pl.when(i == n_real)
        def _():
            for ref in refs[N_MIX_PROMPT_INPUTS:N_MIX_PROMPT_INPUTS + 3]:
                ref[...] = jnp.zeros(ref.shape, ref.dtype)


def _mix_body(*refs, prompt, tiles_per_seq):
    if prompt:
        (o_ref, ucur_ref, uhalo_ref, x_ref, g1_ref, sc2_ref, sh2_ref, wdw_ref, bdw_ref, lng_ref, lnb_ref,
         wout_ref, n2g_ref, wr_ref, br_ref, basein_ref,
         hp1_ref, n2_ref, meta_ref, cnt_ref, ext, base) = refs
    else:
        (o_ref, uext_ref, x_ref, g1_ref, sc2_ref, sh2_ref, wdw_ref, bdw_ref, lng_ref, lnb_ref,
         wout_ref, n2g_ref, wr_ref, br_ref, basein_ref, hp1_in, n2_in, meta_in,
         hp1_ref, n2_ref, meta_ref, cnt_ref, base) = refs
    i = pl.program_id(0)

    @pl.when(i == 0)
    def _():
        base[...] = basein_ref[0:1, :]

    if prompt:
        first = (i % tiles_per_seq) == 0
        halo = uhalo_ref[...]
        ext[0:HALO, :] = jnp.where(first, jnp.zeros_like(halo), halo)
        ext[HALO:HALO + TM, :] = ucur_ref[...]
        off = HALO - (CONV_K - 1)
        chunks = []
        for c in range(TM // CONV_CHUNK):
            a = jnp.zeros((CONV_CHUNK, CONV_CH), F32) + bdw_ref[...]
            for j in range(CONV_K):
                a = a + wdw_ref[j:j + 1, :] * ext[pl.ds(off + j + c * CONV_CHUNK, CONV_CHUNK), :]
            chunks.append(a)
        y = jnp.concatenate(chunks, axis=0)
    else:
        nb, text, _ = uext_ref.shape
        ts = text - (CONV_K - 1)
        a = jnp.zeros((nb, ts, CONV_CH), F32) + bdw_ref[...]
        for j in range(CONV_K):
            a = a + wdw_ref[j:j + 1, :] * uext_ref[:, j:j + ts, :]
        y = a.reshape(nb * ts, CONV_CH)
    yc = _conv_ln_swish(y, lng_ref[...], lnb_ref[...])

    proj = (jnp.dot(o_ref[...].astype(BF16), wout_ref[0:N_HEADS * DV, :], preferred_element_type=F32)
            + jnp.dot(yc.astype(BF16), wout_ref[N_HEADS * DV:, :], preferred_element_type=F32))
    hp1 = x_ref[...] + g1_ref[0] * proj
    hp1_ref[...] = hp1
    n2 = hp1 * lax.rsqrt(jnp.mean(hp1 * hp1, axis=-1, keepdims=True) + EPS) * n2g_ref[...]
    n2 = n2 * (1.0 + sc2_ref[0]) + sh2_ref[0]
    n2_ref[...] = n2

    logits = jnp.dot(n2, wr_ref[...], precision=HIGHEST, preferred_element_type=F32) + br_ref[...]
    lane = lax.broadcasted_iota(I32, logits.shape, 1)
    lane_f = lane.astype(F32)
    lg = jnp.where(lane < N_EXPERTS, logits, NEG_INF)
    onehots, vals, idxs = [], [], []
    for _ in range(TOP_K):
        mx = jnp.max(lg, axis=-1, keepdims=True)
        idx = jnp.min(jnp.where(lg == mx, lane_f, float(LANES)), axis=-1, keepdims=True)
        oh = lane_f == idx
        lg = jnp.where(oh, NEG_INF, lg)
        onehots.append(oh)
        vals.append(mx)
        idxs.append(idx)
    exps = [jnp.exp(v - vals[0]) for v in vals]
    denom = exps[0] + exps[1] + exps[2] + exps[3]

    sel = jnp.zeros(logits.shape, F32)
    for oh in onehots:
        sel = sel + oh.astype(F32)
    r_i = lax.broadcasted_iota(I32, (TM, TM), 0)
    c_i = lax.broadcasted_iota(I32, (TM, TM), 1)
    ltri = (r_i > c_i).astype(BF16)
    before = jnp.dot(ltri, sel.astype(BF16), preferred_element_type=F32) + base[...]
    meta = jnp.zeros(logits.shape, F32)
    for k in range(TOP_K):
        rank = jnp.sum(jnp.where(onehots[k], before, 0.0), axis=-1, keepdims=True)
        meta = meta + jnp.where(lane == k, idxs[k], 0.0)
        meta = meta + jnp.where(lane == TOP_K + k, exps[k] / denom, 0.0)
        meta = meta + jnp.where(lane == 2 * TOP_K + k, rank, 0.0)
    meta_ref[...] = meta
    new_base = base[...] + jnp.sum(sel, axis=0, keepdims=True)
    base[...] = new_base
    cnt_ref[...] = jnp.broadcast_to(new_base, cnt_ref.shape)


def _mix_prompt(o_attn, u, x, g1, sc2, sh2, conv_w, w_out_bf16, norm2_g, w_router_pad, b_router_pad,
                base_in, n_total, tiles_per_seq):
    n = x.shape[0]
    nt = n // TM
    assert n_total == n + TM
    row = lambda i: (jnp.minimum(i, nt - 1), 0)
    out_row = lambda i: (i, 0)
    const2 = lambda i: (0, 0)
    grp = lambda i: (jnp.minimum(i, nt - 1) // tiles_per_seq, 0, 0)
    halo_row = lambda i: (jnp.maximum(jnp.minimum(i, nt - 1) * (TM // HALO) - 1, 0), 0)
    wdw, bdw, lng, lnb = conv_w
    in_specs = [
        pl.BlockSpec((TM, N_HEADS * DV), row),
        pl.BlockSpec((TM, CONV_CH), row),
        pl.BlockSpec((HALO, CONV_CH), halo_row),
        pl.BlockSpec((TM, D_MODEL), row),
        pl.BlockSpec((1, 1, D_MODEL), grp), pl.BlockSpec((1, 1, D_MODEL), grp), pl.BlockSpec((1, 1, D_MODEL), grp),
        pl.BlockSpec((CONV_K, CONV_CH), const2), pl.BlockSpec((1, CONV_CH), const2),
        pl.BlockSpec((1, CONV_CH), const2), pl.BlockSpec((1, CONV_CH), const2),
        pl.BlockSpec((D_MODEL, D_MODEL), const2),
        pl.BlockSpec((1, D_MODEL), const2),
        pl.BlockSpec((D_MODEL, LANES), const2), pl.BlockSpec((1, LANES), const2),
        pl.BlockSpec((SUBLANES, LANES), const2),
    ]
    assert len(in_specs) == N_MIX_PROMPT_INPUTS
    out_specs = [
        pl.BlockSpec((TM, D_MODEL), out_row),
        pl.BlockSpec((TM, D_MODEL), out_row),
        pl.BlockSpec((TM, LANES), out_row),
        pl.BlockSpec((SUBLANES, LANES), const2),
    ]
    out_shape = [
        jax.ShapeDtypeStruct((n_total, D_MODEL), F32),
        jax.ShapeDtypeStruct((n_total, D_MODEL), F32),
        jax.ShapeDtypeStruct((n_total, LANES), F32),
        jax.ShapeDtypeStruct((SUBLANES, LANES), F32),
    ]
    return pl.pallas_call(
        functools.partial(_mix_kernel, prompt=True, tiles_per_seq=tiles_per_seq, n_real=nt),
        grid=(nt + 1,),
        in_specs=in_specs,
        out_specs=out_specs,
        out_shape=out_shape,
        scratch_shapes=[pltpu.VMEM((HALO + TM, CONV_CH), F32), pltpu.VMEM((1, LANES), F32)],
        compiler_params=_cparams(("arbitrary",)),
        name="mix_prompt",
    )(o_attn, u, u, x, g1, sc2, sh2, wdw, bdw.reshape(1, -1), lng.reshape(1, -1), lnb.reshape(1, -1),
      w_out_bf16, norm2_g.reshape(1, -1), w_router_pad, b_router_pad, base_in)


def _mix_sample(o_attn, uext, x, g1, sc2, sh2, conv_w, w_out_bf16, norm2_g, w_router_pad, b_router_pad,
                base_in, hp1_all, n2_all, meta_all, tile0):
    const2 = lambda i: (0, 0)
    const3 = lambda i: (0, 0, 0)
    out_row = lambda i: (tile0, 0)
    wdw, bdw, lng, lnb = conv_w
    nb, text, _ = uext.shape
    in_specs = [
        pl.BlockSpec((TM, N_HEADS * DV), const2),
        pl.BlockSpec((nb, text, CONV_CH), const3),
        pl.BlockSpec((TM, D_MODEL), const2),
        pl.BlockSpec((1, TM, D_MODEL), const3), pl.BlockSpec((1, TM, D_MODEL), const3),
        pl.BlockSpec((1, TM, D_MODEL), const3),
        pl.BlockSpec((CONV_K, CONV_CH), const2), pl.BlockSpec((1, CONV_CH), const2),
        pl.BlockSpec((1, CONV_CH), const2), pl.BlockSpec((1, CONV_CH), const2),
        pl.BlockSpec((D_MODEL, D_MODEL), const2),
        pl.BlockSpec((1, D_MODEL), const2),
        pl.BlockSpec((D_MODEL, LANES), const2), pl.BlockSpec((1, LANES), const2),
        pl.BlockSpec((SUBLANES, LANES), const2),
        pl.BlockSpec(memory_space=pl.ANY), pl.BlockSpec(memory_space=pl.ANY), pl.BlockSpec(memory_space=pl.ANY),
    ]
    out_specs = [
        pl.BlockSpec((TM, D_MODEL), out_row),
        pl.BlockSpec((TM, D_MODEL), out_row),
        pl.BlockSpec((TM, LANES), out_row),
        pl.BlockSpec((SUBLANES, LANES), const2),
    ]
    out_shape = [
        jax.ShapeDtypeStruct(hp1_all.shape, F32),
        jax.ShapeDtypeStruct(n2_all.shape, F32),
        jax.ShapeDtypeStruct(meta_all.shape, F32),
        jax.ShapeDtypeStruct((SUBLANES, LANES), F32),
    ]
    return pl.pallas_call(
        functools.partial(_mix_kernel, prompt=False, tiles_per_seq=1, n_real=1),
        grid=(1,),
        in_specs=in_specs,
        out_specs=out_specs,
        out_shape=out_shape,
        scratch_shapes=[pltpu.VMEM((1, LANES), F32)],
        input_output_aliases={15: 0, 16: 1, 17: 2},
        compiler_params=_cparams(("arbitrary",)),
        name="mix_sample",
    )(o_attn, uext, x, g1, sc2, sh2, wdw, bdw.reshape(1, -1), lng.reshape(1, -1), lnb.reshape(1, -1),
      w_out_bf16, norm2_g.reshape(1, -1), w_router_pad, b_router_pad, base_in, hp1_all, n2_all, meta_all)


def _lane_cumsum(x, lane):
    s = 1
    while s < LANES:
        x = x + jnp.where(lane >= s, pltpu.roll(x, s, 1), 0)
        s *= 2
    return x


def _slots_kernel(meta_ref, cnt_ref, slots_ref, tmap_ref, einfo_ref, *, n_tiles_pad):
    shift = TM.bit_length() - 1
    lane8 = lax.broadcasted_iota(I32, (SUBLANES, LANES), 1)
    cnt = cnt_ref[...].astype(I32)
    padded = ((cnt + (TM - 1)) >> shift) << shift
    csum = _lane_cumsum(padded, lane8)
    gstart = csum - padded

    meta = meta_ref[...]
    lane = lax.broadcasted_iota(I32, meta.shape, 1)
    lane_f = lane.astype(F32)
    gstart_f = gstart[0:1, :].astype(F32)
    out = jnp.zeros(meta.shape, F32)
    for k in range(TOP_K):
        sel = lane_f == meta[:, k:k + 1]
        gs = jnp.sum(jnp.where(sel, gstart_f, 0.0), axis=-1, keepdims=True)
        out = out + jnp.where(lane == k, gs + meta[:, 2 * TOP_K + k: 2 * TOP_K + k + 1], 0.0)
    slots_ref[...] = out.astype(I32)

    @pl.when(pl.program_id(0) == 0)
    def _():
        ctiles = csum[0:1, :] >> shift
        n_valid = jnp.max(ctiles, axis=-1, keepdims=True)
        t = lax.broadcasted_iota(I32, (n_tiles_pad, LANES), 0)
        t = jnp.minimum(t, n_valid - 1)
        lane_t = lax.broadcasted_iota(I32, (n_tiles_pad, LANES), 1)
        hit = jnp.where((lane_t < N_EXPERTS) & (ctiles <= t), 1, 0)
        te = jnp.sum(hit, axis=-1, keepdims=True)
        tmap_ref[...] = jnp.broadcast_to(jnp.minimum(te, N_EXPERTS - 1), tmap_ref.shape)
        row8 = lax.broadcasted_iota(I32, (SUBLANES, LANES), 0)
        info = jnp.where(row8 == 0, gstart, 0)
        info = info + jnp.where(row8 == 1, padded, 0)
        info = info + jnp.where(row8 == 2, cnt, 0)
        info = info + jnp.where(row8 == 3, jnp.broadcast_to(n_valid, (SUBLANES, LANES)), 0)
        einfo_ref[...] = info


def _routing_slots(meta_all, counts, n_tiles_pad):
    n = meta_all.shape[0]
    const2 = lambda i: (0, 0)
    return pl.pallas_call(
        functools.partial(_slots_kernel, n_tiles_pad=n_tiles_pad),
        grid=(n // TM,),
        in_specs=[pl.BlockSpec((TM, LANES), lambda i: (i, 0)), pl.BlockSpec((SUBLANES, LANES), const2)],
        out_specs=[
            pl.BlockSpec((TM, LANES), lambda i: (i, 0)),
            pl.BlockSpec((n_tiles_pad, LANES), const2),
            pl.BlockSpec((SUBLANES, LANES), const2),
        ],
        out_shape=[
            jax.ShapeDtypeStruct((n, LANES), I32),
            jax.ShapeDtypeStruct((n_tiles_pad, LANES), I32),
            jax.ShapeDtypeStruct((SUBLANES, LANES), I32),
        ],
        compiler_params=_cparams(("arbitrary",)),
        name="routing_slots",
    )(meta_all, counts)


def _row_copy(src, dst, src_row, dst_row, sem):
    return pltpu.make_async_copy(src.at[pl.ds(src_row, 1)], dst.at[pl.ds(dst_row, 1)], sem)


def _dispatch_kernel(gstart_ref, padded_ref, cnt_ref, nv_ref, slots_hbm, src_hbm, xs_hbm, idx, zbuf, sems, zsem):
    i = pl.program_id(0)
    n_steps = pl.num_programs(0)
    cur = i % 2
    n_slot_tiles = xs_hbm.shape[0] // TM

    def zero_tile_copy(start):
        return pltpu.make_async_copy(zbuf, xs_hbm.at[pl.ds(pl.multiple_of(start, TM), TM)], zsem)

    def pad_tile_copy(e):
        return zero_tile_copy(gstart_ref[e] + padded_ref[e] - TM)

    @pl.when(i == 0)
    def _():
        zbuf[...] = jnp.zeros(zbuf.shape, zbuf.dtype)
        for e in range(N_EXPERTS):
            @pl.when(cnt_ref[e] > 0)
            def _():
                pad_tile_copy(e).start()

        def start_unused(t, carry):
            zero_tile_copy(t * TM).start()
            return carry

        def wait_unused(t, carry):
            zero_tile_copy(t * TM).wait()
            return carry

        lax.fori_loop(nv_ref[0], n_slot_tiles, start_unused, 0)
        for e in range(N_EXPERTS):
            @pl.when(cnt_ref[e] > 0)
            def _():
                pad_tile_copy(e).wait()
        lax.fori_loop(nv_ref[0], n_slot_tiles, wait_unused, 0)

    def for_each_row(step, buf, fn):
        def body(r, carry):
            for k in range(TOP_K):
                fn(_row_copy(src_hbm, xs_hbm, step * TM + r, idx[buf, r * TOP_K + k], sems.at[buf]))
            return carry
        lax.fori_loop(0, TM, body, 0, unroll=8)

    pltpu.sync_copy(slots_hbm.at[i], idx.at[cur])
    for_each_row(i, cur, lambda c: c.start())

    @pl.when(i > 0)
    def _():
        for_each_row(i - 1, 1 - cur, lambda c: c.wait())

    @pl.when(i == n_steps - 1)
    def _():
        for_each_row(i, cur, lambda c: c.wait())


def _dispatch(gstart, padded, cnt, n_valid, slots2d, n2_all, s_max):
    nt = slots2d.shape[0]
    grid_spec = pltpu.PrefetchScalarGridSpec(
        num_scalar_prefetch=4,
        grid=(nt,),
        in_specs=[pl.BlockSpec(memory_space=pl.ANY), pl.BlockSpec(memory_space=pl.ANY)],
        out_specs=pl.BlockSpec(memory_space=pl.ANY),
        scratch_shapes=[
            pltpu.SMEM((2, TM * TOP_K), I32),
            pltpu.VMEM((TM, D_MODEL), F32),
            pltpu.SemaphoreType.DMA((2,)),
            pltpu.SemaphoreType.DMA,
        ],
    )
    return pl.pallas_call(
        _dispatch_kernel,
        grid_spec=grid_spec,
        out_shape=jax.ShapeDtypeStruct((s_max, D_MODEL), F32),
        compiler_params=pltpu.CompilerParams(dimension_semantics=("arbitrary",), has_side_effects=True),
        name="moe_dispatch",
    )(gstart, padded, cnt, n_valid, slots2d, n2_all)


def _expert_kernel(te_ref, nv_ref, x_ref, w1g_ref, w1u_ref, w2_ref, b1g_ref, b1u_ref, b2_ref, y_ref):
    @pl.when(pl.program_id(0) < nv_ref[0])
    def _():
        x = x_ref[...].astype(BF16)
        hg = jnp.dot(x, w1g_ref[0], preferred_element_type=F32) + b1g_ref[0]
        hu = jnp.dot(x, w1u_ref[0], preferred_element_type=F32) + b1u_ref[0]
        gate = jnp.minimum(hg, SWIGLU_LIMIT)
        up = jnp.clip(hu, -SWIGLU_LIMIT, SWIGLU_LIMIT)
        act = (up + 1.0) * gate * jax.nn.sigmoid(SWIGLU_ALPHA * gate)
        y_ref[...] = jnp.dot(act.astype(BF16), w2_ref[0], preferred_element_type=F32) + b2_ref[0]

    @pl.when(pl.program_id(0) >= nv_ref[0])
    def _():
        y_ref[...] = jnp.zeros(y_ref.shape, y_ref.dtype)


def _expert_mlp(tile_expert, n_valid, xs, w1g, w1u, w2, b1g, b1u, b2):
    s_max = xs.shape[0]
    nt = s_max // TM
    ff = w1g.shape[2]
    tile = lambda i, te, nv: (jnp.minimum(i, nv[0] - 1), 0)
    wsel = lambda i, te, nv: (te[i], 0, 0)
    grid_spec = pltpu.PrefetchScalarGridSpec(
        num_scalar_prefetch=2,
        grid=(nt,),
        in_specs=[
            pl.BlockSpec((TM, D_MODEL), tile),
            pl.BlockSpec((1, D_MODEL, ff), wsel),
            pl.BlockSpec((1, D_MODEL, ff), wsel),
            pl.BlockSpec((1, ff, D_MODEL), wsel),
            pl.BlockSpec((1, 1, ff), wsel),
            pl.BlockSpec((1, 1, ff), wsel),
            pl.BlockSpec((1, 1, D_MODEL), wsel),
        ],
        out_specs=pl.BlockSpec((TM, D_MODEL), lambda i, te, nv: (i, 0)),
    )
    return pl.pallas_call(
        _expert_kernel,
        grid_spec=grid_spec,
        out_shape=jax.ShapeDtypeStruct((s_max, D_MODEL), F32),
        compiler_params=_cparams(("arbitrary",)),
        name="expert_mlp",
    )(tile_expert, n_valid, xs, w1g, w1u, w2, b1g, b1u, b2)


def _combine_kernel(slots_hbm, ys_hbm, hp1_ref, meta_ref, g2_ref, nf_ref, o_ref, idx, buf, sems, *, tile0):
    i = pl.program_id(0)
    n_steps = pl.num_programs(0)
    cur = i % 2

    def for_each_row(b, fn):
        def body(r, carry):
            for k in range(TOP_K):
                fn(pltpu.make_async_copy(ys_hbm.at[pl.ds(idx[b, r * TOP_K + k], 1)],
                                         buf.at[b, k, pl.ds(r, 1)], sems.at[b]))
            return carry
        lax.fori_loop(0, TM, body, 0, unroll=8)

    def fetch(step, b):
        pltpu.sync_copy(slots_hbm.at[tile0 + step], idx.at[b])
        for_each_row(b, lambda c: c.start())

    @pl.when(i == 0)
    def _():
        fetch(i, cur)

    @pl.when(i + 1 < n_steps)
    def _():
        fetch(i + 1, 1 - cur)

    for_each_row(cur, lambda c: c.wait())

    meta = meta_ref[...]
    f = jnp.zeros((TM, D_MODEL), F32)
    for k in range(TOP_K):
        f = f + meta[:, TOP_K + k: TOP_K + k + 1] * buf[cur, k]
    hp2 = hp1_ref[...] + g2_ref[0] * f
    o_ref[...] = hp2 * lax.rsqrt(jnp.mean(hp2 * hp2, axis=-1, keepdims=True) + EPS) * nf_ref[...]


def _combine(slots2d, ys, hp1_all, meta_all, g2, normf_g, tile0, n_rows, tiles_per_group):
    nt = n_rows // TM
    r = g2.shape[1]
    row_in = lambda i: (tile0 + i, 0)
    return pl.pallas_call(
        functools.partial(_combine_kernel, tile0=tile0),
        grid=(nt,),
        in_specs=[
            pl.BlockSpec(memory_space=pl.ANY),
            pl.BlockSpec(memory_space=pl.ANY),
            pl.BlockSpec((TM, D_MODEL), row_in),
            pl.BlockSpec((TM, LANES), row_in),
            pl.BlockSpec((1, r, D_MODEL), lambda i: (i // tiles_per_group, 0, 0)),
            pl.BlockSpec((1, D_MODEL), lambda i: (0, 0)),
        ],
        out_specs=pl.BlockSpec((TM, D_MODEL), lambda i: (i, 0)),
        out_shape=jax.ShapeDtypeStruct((n_rows, D_MODEL), F32),
        scratch_shapes=[
            pltpu.SMEM((2, TM * TOP_K), I32),
            pltpu.VMEM((2, TOP_K, TM, D_MODEL), F32),
            pltpu.SemaphoreType.DMA((2,)),
        ],
        compiler_params=_cparams(("arbitrary",)),
        name="moe_combine",
    )(slots2d, ys, hp1_all, meta_all, g2, normf_g.reshape(1, D_MODEL))


def kernel(x_prompt, x_sample, cache_k, cache_v, state_conv, page_table, c_prompt, c_sample, norm1_g, norm2_g, w_ada, b_ada, w_in, lambda_q1, lambda_k1, lambda_q2, lambda_k2, subln_g, w_dw, b_dw, conv_ln_g, conv_ln_b, w_out, w_router, b_router, w_gate_up, b_gate_up, w_down, b_down, normf_g):
    depth = norm1_g.shape[0]
    assert depth == 1, "single-layer step"
    bsz, seq, d = x_prompt.shape
    db, ts, _ = x_sample.shape
    n_p, n_s = bsz * seq, db * ts
    assert d == D_MODEL and n_s == TM and seq % TQ == 0 and n_p % TM == 0
    n_all = n_p + n_s
    n_pages = page_table.shape[1]
    past = n_pages * cache_k.shape[2]
    tiles_per_seq = seq // TM
    l = 0

    n_cond = bsz + db
    c_all = jnp.concatenate([c_prompt, c_sample], axis=0)
    c_all = jnp.pad(c_all, ((0, -n_cond % SUBLANES), (0, 0)))
    mod = _adaln(c_all, w_ada[l], b_ada[l])[:n_cond]
    mod_p = mod[:bsz].reshape(bsz, 1, 6, D_MODEL)
    mod_s = jnp.repeat(mod[bsz:], ts, axis=0).reshape(1, n_s, 6, D_MODEL)
    sh1p, sc1p, g1p, sh2p, sc2p, g2p = [mod_p[:, :, j] for j in range(6)]
    sh1s, sc1s, g1s, sh2s, sc2s, g2s = [mod_s[:, :, j] for j in range(6)]

    lams = [v[l].reshape(1, DQK) for v in (lambda_q1, lambda_k1, lambda_q2, lambda_k2)]
    w_in_b = w_in[l].astype(BF16)
    w_out_b = w_out[l].astype(BF16)
    conv_w = (w_dw[l], b_dw[l], conv_ln_g[l], conv_ln_b[l])
    w_router_pad = jnp.pad(w_router[l], ((0, 0), (0, LANES - N_EXPERTS)))
    b_router_pad = jnp.pad(b_router[l], (0, LANES - N_EXPERTS)).reshape(1, LANES)

    xp = x_prompt.reshape(n_p, D_MODEL)
    xs_tok = x_sample.reshape(n_s, D_MODEL)
    tabs_p = _rope_tables(jnp.arange(seq))
    tabs_s = _rope_tables(jnp.tile(past + jnp.arange(ts), db))
    q_hm, k_p, v_p, k_hm, v_hm, u_p = _inproj(xp, norm1_g[l], sc1p, sh1p, tabs_p, w_in_b,
                                              tiles_per_seq, tiles_per_seq, True)
    q_s, k_s, v_s, u_s = _inproj(xs_tok, norm1_g[l], sc1s, sh1s, tabs_s, w_in_b, 1, 1, False)

    o_p = _prompt_attention(q_hm, k_hm, v_hm, lams, subln_g[l], bsz, seq)
    o_s = _sample_attention(q_s, k_s, v_s, cache_k[l], cache_v[l], page_table, lams, subln_g[l])

    uext_s = jnp.concatenate([state_conv[l], u_s.reshape(db, ts, CONV_CH)], axis=1)

    zeros_base = jnp.zeros((SUBLANES, LANES), F32)
    hp1_all, n2_all, meta_all, cnt_p = _mix_prompt(
        o_p, u_p, xp, g1p, sc2p, sh2p, conv_w, w_out_b, norm2_g[l], w_router_pad, b_router_pad,
        zeros_base, n_all, tiles_per_seq)
    hp1_all, n2_all, meta_all, counts = _mix_sample(
        o_s, uext_s, xs_tok, g1s, sc2s, sh2s, conv_w, w_out_b, norm2_g[l], w_router_pad, b_router_pad,
        cnt_p, hp1_all, n2_all, meta_all, n_p // TM)

    n_tok_tiles = n_all // TM
    s_max = n_all * TOP_K + N_EXPERTS * TM
    n_slot_tiles = s_max // TM
    n_tiles_pad = -(-n_slot_tiles // SUBLANES) * SUBLANES
    slots, tmap, einfo = _routing_slots(meta_all, counts, n_tiles_pad)
    slots2d = slots[:, :TOP_K].reshape(n_tok_tiles, TM * TOP_K)
    tile_expert = tmap[:n_slot_tiles, 0]
    gstart, padded, cnt = einfo[0, :N_EXPERTS], einfo[1, :N_EXPERTS], einfo[2, :N_EXPERTS]
    n_valid = einfo[3, :1]

    x_sorted = _dispatch(gstart, padded, cnt, n_valid, slots2d, n2_all, s_max)

    wgu = w_gate_up[l]
    w1g = wgu[:, :, 0::2].astype(BF16)
    w1u = wgu[:, :, 1::2].astype(BF16)
    bgu = b_gate_up[l]
    b1g = bgu[:, 0::2].reshape(N_EXPERTS, 1, -1)
    b1u = bgu[:, 1::2].reshape(N_EXPERTS, 1, -1)
    w2 = w_down[l].astype(BF16)
    b2 = b_down[l].reshape(N_EXPERTS, 1, D_MODEL)
    y_sorted = _expert_mlp(tile_expert, n_valid, x_sorted, w1g, w1u, w2, b1g, b1u, b2)

    y_p = _combine(slots2d, y_sorted, hp1_all, meta_all, g2p, normf_g, 0, n_p, tiles_per_seq)
    y_s = _combine(slots2d, y_sorted, hp1_all, meta_all, g2s, normf_g, n_p // TM, n_s, 1)

    y_prompt = y_p.reshape(bsz, seq, D_MODEL)
    y_sample = y_s.reshape(db, ts, D_MODEL)
    k_prompt = k_p.reshape(1, bsz, seq, 2 * N_HEADS, DQK)
    v_prompt = v_p.reshape(1, bsz, seq, N_HEADS, DV)
    conv_prompt = u_p.reshape(bsz, seq, CONV_CH)[:, seq - (CONV_K - 1):][None]
    k_sample = k_s.reshape(1, db, ts, 2 * N_HEADS, DQK)
    v_sample = v_s.reshape(1, db, ts, N_HEADS, DV)
    conv_sample = uext_s[:, ts:][None]
    return (y_prompt, y_sample, k_prompt, v_prompt, conv_prompt, k_sample, v_sample, conv_sample)
```

```python
import functools
import math

import jax
import jax.numpy as jnp
from jax import lax
from jax.experimental import pallas as pl
from jax.experimental.pallas import tpu as pltpu

F32 = jnp.float32
BF16 = jnp.bfloat16
I32 = jnp.int32
HIGHEST = lax.Precision.HIGHEST

D_MODEL = 1024
N_HEADS = 4
DV = 128
DQK = 64
ROT_DIM = 16
ROPE_THETA = 500000.0
CONV_K = 31
CONV_CH = 512
QK_W = 512
V_W = 512
IN_WIDTH = 2 * QK_W + V_W + 2 * CONV_CH
N_EXPERTS = 32
TOP_K = 4
SWIGLU_LIMIT = 7.0
SWIGLU_ALPHA = 1.702
EPS = 1e-5
LAM_INIT = 0.8 - 0.6 * math.exp(-0.3 * 0)
PAGE_SIZE = 128

LANES = 128
SUBLANES = 8
VMEM_LIMIT = 48 * 1024 * 1024

TM = 256
TQ = 512
TK = 512
PAGES_PER_STEP = 8
HALO = 32
CONV_CHUNK = 64
NEG_INF = float("-inf")


def _cparams(sem):
    return pltpu.CompilerParams(dimension_semantics=sem, vmem_limit_bytes=VMEM_LIMIT)


def _adaln_kernel(c_ref, w_ref, b_ref, o_ref):
    c = c_ref[...]
    s = c * jax.nn.sigmoid(c)
    o_ref[...] = jnp.dot(s, w_ref[...], precision=HIGHEST, preferred_element_type=F32) + b_ref[...]


def _adaln(c_all, w, b):
    n, d = c_all.shape
    width = w.shape[1]
    bn = 1536
    return pl.pallas_call(
        _adaln_kernel,
        grid=(width // bn,),
        in_specs=[
            pl.BlockSpec((n, d), lambda j: (0, 0)),
            pl.BlockSpec((d, bn), lambda j: (0, j)),
            pl.BlockSpec((1, bn), lambda j: (0, j)),
        ],
        out_specs=pl.BlockSpec((n, bn), lambda j: (0, j)),
        out_shape=jax.ShapeDtypeStruct((n, width), F32),
        compiler_params=_cparams(("parallel",)),
        name="adaln",
    )(c_all, w, b.reshape(1, width))


def _rope_tables(pos):
    inv = ROPE_THETA ** (-jnp.arange(0, ROT_DIM, 2, dtype=F32) / ROT_DIM)
    ang = pos.astype(F32)[:, None] * inv
    cos, sin = jnp.cos(ang), jnp.sin(ang)
    half = ROT_DIM // 2
    ones = jnp.ones((pos.shape[0], DQK - ROT_DIM), F32)
    zeros_h = jnp.zeros((pos.shape[0], half), F32)
    zeros_r = jnp.zeros((pos.shape[0], DQK - ROT_DIM), F32)
    c64 = jnp.concatenate([cos, cos, ones], axis=1)
    a64 = jnp.concatenate([-sin, zeros_h, zeros_r], axis=1)
    b64 = jnp.concatenate([zeros_h, sin, zeros_r], axis=1)
    rep = LANES // DQK
    return jnp.tile(c64, (1, rep)), jnp.tile(a64, (1, rep)), jnp.tile(b64, (1, rep))


def _inproj_kernel(x_ref, g_ref, sc_ref, sh_ref, cos_ref, sa_ref, sb_ref, w_ref, *out_refs, head_major):
    x = x_ref[...]
    hn = x * lax.rsqrt(jnp.mean(x * x, axis=-1, keepdims=True) + EPS) * g_ref[...]
    hn = hn * (1.0 + sc_ref[0]) + sh_ref[0]
    proj = jnp.dot(hn.astype(BF16), w_ref[...], preferred_element_type=F32)
    cos, sa, sb = cos_ref[...], sa_ref[...], sb_ref[...]

    def rope(blk):
        return blk * cos + pltpu.roll(blk, LANES - ROT_DIM // 2, 1) * sa + pltpu.roll(blk, ROT_DIM // 2, 1) * sb

    a = proj[:, 2 * QK_W + V_W: 2 * QK_W + V_W + CONV_CH]
    gl = proj[:, 2 * QK_W + V_W + CONV_CH:]
    if head_major:
        qb_ref, k32_ref, v32_ref, kb_ref, vb_ref, u_ref = out_refs
    else:
        q32_ref, k32_ref, v32_ref, u_ref = out_refs
    u_ref[...] = a * jax.nn.sigmoid(gl)
    for h in range(N_HEADS):
        lo, hi = h * LANES, (h + 1) * LANES
        qh = rope(proj[:, lo:hi]) * (DQK ** -0.5)
        kh = rope(proj[:, QK_W + lo: QK_W + hi])
        vh = proj[:, 2 * QK_W + lo: 2 * QK_W + hi]
        k32_ref[:, lo:hi] = kh
        v32_ref[:, lo:hi] = vh
        if head_major:
            qb_ref[h] = qh.astype(BF16)
            kb_ref[h] = kh.astype(BF16)
            vb_ref[h] = vh.astype(BF16)
        else:
            q32_ref[:, lo:hi] = qh


def _inproj(x, norm_g, sc, sh, tabs, w_in_bf16, tiles_per_group, tiles_per_seq, head_major):
    n = x.shape[0]
    nt = n // TM
    r = sc.shape[1]
    row = lambda i: (i, 0)
    grp = lambda i: (i // tiles_per_group, 0, 0)
    tab = lambda i: (i % tiles_per_seq, 0)
    in_specs = [
        pl.BlockSpec((TM, D_MODEL), row),
        pl.BlockSpec((1, D_MODEL), lambda i: (0, 0)),
        pl.BlockSpec((1, r, D_MODEL), grp),
        pl.BlockSpec((1, r, D_MODEL), grp),
        pl.BlockSpec((TM, LANES), tab),
        pl.BlockSpec((TM, LANES), tab),
        pl.BlockSpec((TM, LANES), tab),
        pl.BlockSpec((D_MODEL, IN_WIDTH), lambda i: (0, 0)),
    ]
    wide = pl.BlockSpec((TM, QK_W), row)
    hm = pl.BlockSpec((N_HEADS, TM, LANES), lambda i: (0, i, 0))
    if head_major:
        out_specs = [hm, wide, wide, hm, hm, wide]
        out_shape = [
            jax.ShapeDtypeStruct((N_HEADS, n, LANES), BF16),
            jax.ShapeDtypeStruct((n, QK_W), F32),
            jax.ShapeDtypeStruct((n, V_W), F32),
            jax.ShapeDtypeStruct((N_HEADS, n, LANES), BF16),
            jax.ShapeDtypeStruct((N_HEADS, n, LANES), BF16),
            jax.ShapeDtypeStruct((n, CONV_CH), F32),
        ]
    else:
        out_specs = [wide, wide, wide, wide]
        out_shape = [jax.ShapeDtypeStruct((n, QK_W), F32)] * 4
    return pl.pallas_call(
        functools.partial(_inproj_kernel, head_major=head_major),
        grid=(nt,),
        in_specs=in_specs,
        out_specs=out_specs,
        out_shape=out_shape,
        compiler_params=_cparams(("parallel",)),
        name="inproj_hm" if head_major else "inproj",
    )(x, norm_g.reshape(1, D_MODEL), sc, sh, *tabs, w_in_bf16)


def _lambda_value(lq1, lk1, lq2, lk2):
    a = jnp.exp(jnp.sum(lq1[...] * lk1[...], axis=-1, keepdims=True))
    b = jnp.exp(jnp.sum(lq2[...] * lk2[...], axis=-1, keepdims=True))
    return a - b + LAM_INIT


def _diff_merge(o1, l1, o2, l2, lam, subln_g):
    o = o1 / l1 - lam * (o2 / l2)
    o = o * lax.rsqrt(jnp.mean(o * o, axis=-1, keepdims=True) + EPS) * subln_g
    return o * (1.0 - LAM_INIT)


def _attn_kernel(qt_ref, kt_ref, q_ref, k_ref, v_ref, lq1, lk1, lq2, lk2, sg_ref, o_ref, qs, m_s, l_s, acc):
    s_idx = pl.program_id(2)
    qi = qt_ref[s_idx]
    ki = kt_ref[s_idx]

    @pl.when(ki == 0)
    def _():
        q = q_ref[0]
        lane = lax.broadcasted_iota(I32, q.shape, 1)
        zero = jnp.zeros_like(q)
        qs[0:TQ, :] = jnp.where(lane < DQK, q, zero)
        qs[TQ:2 * TQ, :] = jnp.where(lane >= DQK, q, zero)
        m_s[...] = jnp.full(m_s.shape, NEG_INF, F32)
        l_s[...] = jnp.zeros(l_s.shape, F32)
        acc[...] = jnp.zeros(acc.shape, F32)

    def update(masked):
        s = lax.dot_general(qs[...], k_ref[0], (((1,), (1,)), ((), ())), preferred_element_type=F32)
        if masked:
            row = lax.broadcasted_iota(I32, s.shape, 0) & (TQ - 1)
            col = lax.broadcasted_iota(I32, s.shape, 1)
            s = jnp.where(row >= col, s, NEG_INF)
        m_prev = m_s[...]
        m_next = jnp.maximum(m_prev, jnp.max(s, axis=1, keepdims=True))
        p = jnp.exp(s - jnp.tile(m_next, (1, TK // LANES)))
        alpha = jnp.exp(m_prev - m_next)
        l_s[...] = alpha * l_s[...] + jnp.sum(p, axis=1, keepdims=True)
        acc[...] = alpha * acc[...] + jnp.dot(p.astype(BF16), v_ref[0], preferred_element_type=F32)
        m_s[...] = m_next

    @pl.when(ki < qi)
    def _():
        update(False)

    @pl.when(ki == qi)
    def _():
        update(True)
        lam = _lambda_value(lq1, lk1, lq2, lk2)
        o = _diff_merge(acc[0:TQ, :], l_s[0:TQ, :], acc[TQ:2 * TQ, :], l_s[TQ:2 * TQ, :], lam, sg_ref[...])
        o_ref[...] = o.astype(o_ref.dtype)


def _prompt_attention(q_hm, k_hm, v_hm, lams, subln_g, batch, seq):
    nq = seq // TQ
    pairs = [(qi, ki) for qi in range(nq) for ki in range(qi + 1)]
    qt = jnp.asarray([p[0] for p in pairs], I32)
    kt = jnp.asarray([p[1] for p in pairs], I32)
    n = batch * seq
    vec = lambda b, h, s, qt, kt: (0, 0)
    grid_spec = pltpu.PrefetchScalarGridSpec(
        num_scalar_prefetch=2,
        grid=(batch, N_HEADS, len(pairs)),
        in_specs=[
            pl.BlockSpec((1, TQ, LANES), lambda b, h, s, qt, kt: (h, b * nq + qt[s], 0)),
            pl.BlockSpec((1, TK, LANES), lambda b, h, s, qt, kt: (h, b * nq + kt[s], 0)),
            pl.BlockSpec((1, TK, LANES), lambda b, h, s, qt, kt: (h, b * nq + kt[s], 0)),
            pl.BlockSpec((1, DQK), vec), pl.BlockSpec((1, DQK), vec),
            pl.BlockSpec((1, DQK), vec), pl.BlockSpec((1, DQK), vec),
            pl.BlockSpec((1, DV), vec),
        ],
        out_specs=pl.BlockSpec((TQ, LANES), lambda b, h, s, qt, kt: (b * nq + qt[s], h)),
        scratch_shapes=[
            pltpu.VMEM((2 * TQ, LANES), BF16),
            pltpu.VMEM((2 * TQ, LANES), F32),
            pltpu.VMEM((2 * TQ, LANES), F32),
            pltpu.VMEM((2 * TQ, LANES), F32),
        ],
    )
    return pl.pallas_call(
        _attn_kernel,
        grid_spec=grid_spec,
        out_shape=jax.ShapeDtypeStruct((n, N_HEADS * DV), BF16),
        compiler_params=_cparams(("parallel", "parallel", "arbitrary")),
        name="prompt_attn",
    )(qt, kt, q_hm, k_hm, v_hm, *lams, subln_g.reshape(1, DV))


def _paged_attn_kernel(pt_ref, q_ref, kn_ref, vn_ref, lq1, lk1, lq2, lk2, sg_ref, *rest, n_steps, ts):
    kp = rest[:PAGES_PER_STEP]
    vp = rest[PAGES_PER_STEP:2 * PAGES_PER_STEP]
    o_ref, qe, m_s, l_s, acc = rest[2 * PAGES_PER_STEP:]
    n_maps = 2 * N_HEADS
    rows = n_maps * ts
    p_idx = pl.program_id(1)

    @pl.when(p_idx == 0)
    def _():
        qt = jnp.concatenate([q_ref[...]] * n_maps, axis=0)
        row = lax.broadcasted_iota(I32, qt.shape, 0)
        col = lax.broadcasted_iota(I32, qt.shape, 1)
        same_map = (row >> (ts.bit_length() - 1)) == (col >> (DQK.bit_length() - 1))
        qe[...] = jnp.where(same_map, qt, 0.0).astype(BF16)
        m_s[...] = jnp.full(m_s.shape, NEG_INF, F32)
        l_s[...] = jnp.zeros(l_s.shape, F32)
        acc[...] = jnp.zeros(acc.shape, F32)

    def update(kmat, vmat, causal, k_transposed):
        if k_transposed:
            s = jnp.dot(qe[...], kmat, preferred_element_type=F32)
        else:
            s = lax.dot_general(qe[...], kmat, (((1,), (1,)), ((), ())), preferred_element_type=F32)
        if causal:
            row = lax.broadcasted_iota(I32, s.shape, 0) & (ts - 1)
            col = lax.broadcasted_iota(I32, s.shape, 1)
            s = jnp.where(row >= col, s, NEG_INF)
        m_prev = m_s[...]
        m_next = jnp.maximum(m_prev, jnp.max(s, axis=1, keepdims=True))
        p = jnp.exp(s - m_next[:, 0:1])
        alpha = jnp.exp(m_prev - m_next)
        l_s[...] = alpha * l_s[...] + jnp.sum(p, axis=1, keepdims=True)
        acc[...] = jnp.tile(alpha, (1, V_W // LANES)) * acc[...] + jnp.dot(
            p.astype(BF16), vmat, preferred_element_type=F32)
        m_s[...] = m_next

    @pl.when(p_idx < n_steps)
    def _():
        kmat = jnp.concatenate([r[...].reshape(QK_W, PAGE_SIZE) for r in kp], axis=1).astype(BF16)
        vmat = jnp.concatenate(
            [jnp.concatenate([r[:, h, :] for h in range(N_HEADS)], axis=1) for r in vp], axis=0).astype(BF16)
        update(kmat, vmat, False, True)

    @pl.when(p_idx == n_steps)
    def _():
        update(kn_ref[...].astype(BF16), vn_ref[...].astype(BF16), True, False)
        lam = _lambda_value(lq1, lk1, lq2, lk2)
        for h in range(N_HEADS):
            r1, r2 = 2 * h * ts, (2 * h + 1) * ts
            c0, c1 = h * DV, (h + 1) * DV
            o = _diff_merge(acc[r1:r1 + ts, c0:c1], l_s[r1:r1 + ts, :],
                            acc[r2:r2 + ts, c0:c1], l_s[r2:r2 + ts, :], lam, sg_ref[...])
            o_ref[:, c0:c1] = o


def _sample_attention(q_s, k_s, v_s, cache_k, cache_v, page_table, lams, subln_g):
    db, n_pages = page_table.shape
    ts = q_s.shape[0] // db
    n_pool = cache_k.shape[1]
    ck = jnp.transpose(cache_k, (0, 1, 3, 4, 2)).reshape(n_pool, 2 * N_HEADS, DQK, PAGE_SIZE)
    cv = cache_v.reshape(n_pool, PAGE_SIZE, N_HEADS, DV)
    n_steps = n_pages // PAGES_PER_STEP
    pt = page_table.reshape(-1).astype(I32)
    vec = lambda b, p, pt: (0, 0)
    new = lambda b, p, pt: (b, 0)

    def page_spec(j, block):
        def idx(b, p, pt):
            step = jnp.minimum(p, n_steps - 1)
            return (pt[b * n_pages + step * PAGES_PER_STEP + j], 0, 0, 0)
        return pl.BlockSpec(block, idx)

    k_block = (None, 2 * N_HEADS, DQK, PAGE_SIZE)
    v_block = (None, PAGE_SIZE, N_HEADS, DV)

    rows = 2 * N_HEADS * ts
    grid_spec = pltpu.PrefetchScalarGridSpec(
        num_scalar_prefetch=1,
        grid=(db, n_steps + 1),
        in_specs=[
            pl.BlockSpec((ts, QK_W), new), pl.BlockSpec((ts, QK_W), new), pl.BlockSpec((ts, V_W), new),
            pl.BlockSpec((1, DQK), vec), pl.BlockSpec((1, DQK), vec),
            pl.BlockSpec((1, DQK), vec), pl.BlockSpec((1, DQK), vec),
            pl.BlockSpec((1, DV), vec),
        ] + [page_spec(j, k_block) for j in range(PAGES_PER_STEP)]
          + [page_spec(j, v_block) for j in range(PAGES_PER_STEP)],
        out_specs=pl.BlockSpec((ts, V_W), new),
        scratch_shapes=[
            pltpu.VMEM((rows, QK_W), BF16),
            pltpu.VMEM((rows, LANES), F32),
            pltpu.VMEM((rows, LANES), F32),
            pltpu.VMEM((rows, V_W), F32),
        ],
    )
    return pl.pallas_call(
        functools.partial(_paged_attn_kernel, n_steps=n_steps, ts=ts),
        grid_spec=grid_spec,
        out_shape=jax.ShapeDtypeStruct((db * ts, V_W), F32),
        compiler_params=_cparams(("parallel", "arbitrary")),
        name="paged_attn",
    )(pt, q_s, k_s, v_s, *lams, subln_g.reshape(1, DV), *([ck] * PAGES_PER_STEP), *([cv] * PAGES_PER_STEP))


ROW_CHUNKS = D_MODEL // LANES


def _store_token_rows(ref, x):
    t = x.shape[0]
    for j in range(ROW_CHUNKS):
        ref[pl.ds(j, t, stride=ROW_CHUNKS), :] = x[:, j * LANES:(j + 1) * LANES]


def _load_token_rows(ref, t, lead=()):
    return jnp.concatenate(
        [ref[lead + (pl.ds(j, t, stride=ROW_CHUNKS), slice(None))] for j in range(ROW_CHUNKS)], axis=1)


def _conv_ln_swish(y, lng, lnb):
    mu = jnp.mean(y, axis=-1, keepdims=True)
    var = jnp.mean(jnp.square(y - mu), axis=-1, keepdims=True)
    yn = (y - mu) * lax.rsqrt(var + EPS) * lng + lnb
    return yn * jax.nn.sigmoid(yn)


N_MIX_PROMPT_INPUTS = 16


def _mix_kernel(*refs, prompt, tiles_per_seq, n_real):
    i = pl.program_id(0)

    @pl.when(i < n_real)
    def _():
        _mix_body(*refs, prompt=prompt, tiles_per_seq=tiles_per_seq)

    if prompt:
        @pl.when(i == n_real)
        def _():
            for ref in refs[N_MIX_PROMPT_INPUTS:N_MIX_PROMPT_INPUTS + 3]:
                ref[...] = jnp.zeros(ref.shape, ref.dtype)


def _mix_body(*refs, prompt, tiles_per_seq):
    if prompt:
        (o_ref, ucur_ref, uhalo_ref, x_ref, g1_ref, sc2_ref, sh2_ref, wdw_ref, bdw_ref, lng_ref, lnb_ref,
         wout_ref, n2g_ref, wr_ref, br_ref, basein_ref,
         hp1_ref, n2_ref, meta_ref, cnt_ref, ext, base) = refs
    else:
        (o_ref, uext_ref, x_ref, g1_ref, sc2_ref, sh2_ref, wdw_ref, bdw_ref, lng_ref, lnb_ref,
         wout_ref, n2g_ref, wr_ref, br_ref, basein_ref, hp1_in, n2_in, meta_in,
         hp1_ref, n2_ref, meta_ref, cnt_ref, base) = refs
    i = pl.program_id(0)

    @pl.when(i == 0)
    def _():
        base[...] = basein_ref[0:1, :]

    if prompt:
        first = (i % tiles_per_seq) == 0
        halo = uhalo_ref[...]
        ext[0:HALO, :] = jnp.where(first, jnp.zeros_like(halo), halo)
        ext[HALO:HALO + TM, :] = ucur_ref[...]
        off = HALO - (CONV_K - 1)
        chunks = []
        for c in range(TM // CONV_CHUNK):
            a = jnp.zeros((CONV_CHUNK, CONV_CH), F32) + bdw_ref[...]
            for j in range(CONV_K):
                a = a + wdw_ref[j:j + 1, :] * ext[pl.ds(off + j + c * CONV_CHUNK, CONV_CHUNK), :]
            chunks.append(a)
        y = jnp.concatenate(chunks, axis=0)
    else:
        nb, text, _ = uext_ref.shape
        ts = text - (CONV_K - 1)
        a = jnp.zeros((nb, ts, CONV_CH), F32) + bdw_ref[...]
        for j in range(CONV_K):
            a = a + wdw_ref[j:j + 1, :] * uext_ref[:, j:j + ts, :]
        y = a.reshape(nb * ts, CONV_CH)
    yc = _conv_ln_swish(y, lng_ref[...], lnb_ref[...])

    proj = (jnp.dot(o_ref[...].astype(BF16), wout_ref[0:N_HEADS * DV, :], preferred_element_type=F32)
            + jnp.dot(yc.astype(BF16), wout_ref[N_HEADS * DV:, :], preferred_element_type=F32))
    hp1 = x_ref[...] + g1_ref[0] * proj
    hp1_ref[...] = hp1
    n2 = hp1 * lax.rsqrt(jnp.mean(hp1 * hp1, axis=-1, keepdims=True) + EPS) * n2g_ref[...]
    n2 = n2 * (1.0 + sc2_ref[0]) + sh2_ref[0]
    _store_token_rows(n2_ref, n2)

    logits = jnp.dot(n2, wr_ref[...], precision=HIGHEST, preferred_element_type=F32) + br_ref[...]
    lane = lax.broadcasted_iota(I32, logits.shape, 1)
    lane_f = lane.astype(F32)
    lg = jnp.where(lane < N_EXPERTS, logits, NEG_INF)
    onehots, vals, idxs = [], [], []
    for _ in range(TOP_K):
        mx = jnp.max(lg, axis=-1, keepdims=True)
        idx = jnp.min(jnp.where(lg == mx, lane_f, float(LANES)), axis=-1, keepdims=True)
        oh = lane_f == idx
        lg = jnp.where(oh, NEG_INF, lg)
        onehots.append(oh)
        vals.append(mx)
        idxs.append(idx)
    exps = [jnp.exp(v - vals[0]) for v in vals]
    denom = exps[0] + exps[1] + exps[2] + exps[3]

    sel = jnp.zeros(logits.shape, F32)
    for oh in onehots:
        sel = sel + oh.astype(F32)
    r_i = lax.broadcasted_iota(I32, (TM, TM), 0)
    c_i = lax.broadcasted_iota(I32, (TM, TM), 1)
    ltri = (r_i > c_i).astype(BF16)
    before = jnp.dot(ltri, sel.astype(BF16), preferred_element_type=F32) + base[...]
    meta = jnp.zeros(logits.shape, F32)
    for k in range(TOP_K):
        rank = jnp.sum(jnp.where(onehots[k], before, 0.0), axis=-1, keepdims=True)
        meta = meta + jnp.where(lane == k, idxs[k], 0.0)
        meta = meta + jnp.where(lane == TOP_K + k, exps[k] / denom, 0.0)
        meta = meta + jnp.where(lane == 2 * TOP_K + k, rank, 0.0)
    meta_ref[...] = meta
    new_base = base[...] + jnp.sum(sel, axis=0, keepdims=True)
    base[...] = new_base
    cnt_ref[...] = jnp.broadcast_to(new_base, cnt_ref.shape)


def _mix_prompt(o_attn, u, x, g1, sc2, sh2, conv_w, w_out_bf16, norm2_g, w_router_pad, b_router_pad,
                base_in, n_total, tiles_per_seq):
    n = x.shape[0]
    nt = n // TM
    assert n_total == n + TM
    row = lambda i: (jnp.minimum(i, nt - 1), 0)
    out_row = lambda i: (i, 0)
    const2 = lambda i: (0, 0)
    grp = lambda i: (jnp.minimum(i, nt - 1) // tiles_per_seq, 0, 0)
    halo_row = lambda i: (jnp.maximum(jnp.minimum(i, nt - 1) * (TM // HALO) - 1, 0), 0)
    wdw, bdw, lng, lnb = conv_w
    in_specs = [
        pl.BlockSpec((TM, N_HEADS * DV), row),
        pl.BlockSpec((TM, CONV_CH), row),
        pl.BlockSpec((HALO, CONV_CH), halo_row),
        pl.BlockSpec((TM, D_MODEL), row),
        pl.BlockSpec((1, 1, D_MODEL), grp), pl.BlockSpec((1, 1, D_MODEL), grp), pl.BlockSpec((1, 1, D_MODEL), grp),
        pl.BlockSpec((CONV_K, CONV_CH), const2), pl.BlockSpec((1, CONV_CH), const2),
        pl.BlockSpec((1, CONV_CH), const2), pl.BlockSpec((1, CONV_CH), const2),
        pl.BlockSpec((D_MODEL, D_MODEL), const2),
        pl.BlockSpec((1, D_MODEL), const2),
        pl.BlockSpec((D_MODEL, LANES), const2), pl.BlockSpec((1, LANES), const2),
        pl.BlockSpec((SUBLANES, LANES), const2),
    ]
    assert len(in_specs) == N_MIX_PROMPT_INPUTS
    out_specs = [
        pl.BlockSpec((TM, D_MODEL), out_row),
        pl.BlockSpec((TM * ROW_CHUNKS, LANES), out_row),
        pl.BlockSpec((TM, LANES), out_row),
        pl.BlockSpec((SUBLANES, LANES), const2),
    ]
    out_shape = [
        jax.ShapeDtypeStruct((n_total, D_MODEL), F32),
        jax.ShapeDtypeStruct((n_total * ROW_CHUNKS, LANES), F32),
        jax.ShapeDtypeStruct((n_total, LANES), F32),
        jax.ShapeDtypeStruct((SUBLANES, LANES), F32),
    ]
    return pl.pallas_call(
        functools.partial(_mix_kernel, prompt=True, tiles_per_seq=tiles_per_seq, n_real=nt),
        grid=(nt + 1,),
        in_specs=in_specs,
        out_specs=out_specs,
        out_shape=out_shape,
        scratch_shapes=[pltpu.VMEM((HALO + TM, CONV_CH), F32), pltpu.VMEM((1, LANES), F32)],
        compiler_params=_cparams(("arbitrary",)),
        name="mix_prompt",
    )(o_attn, u, u, x, g1, sc2, sh2, wdw, bdw.reshape(1, -1), lng.reshape(1, -1), lnb.reshape(1, -1),
      w_out_bf16, norm2_g.reshape(1, -1), w_router_pad, b_router_pad, base_in)


def _mix_sample(o_attn, uext, x, g1, sc2, sh2, conv_w, w_out_bf16, norm2_g, w_router_pad, b_router_pad,
                base_in, hp1_all, n2_all, meta_all, tile0):
    const2 = lambda i: (0, 0)
    const3 = lambda i: (0, 0, 0)
    out_row = lambda i: (tile0, 0)
    wdw, bdw, lng, lnb = conv_w
    nb, text, _ = uext.shape
    in_specs = [
        pl.BlockSpec((TM, N_HEADS * DV), const2),
        pl.BlockSpec((nb, text, CONV_CH), const3),
        pl.BlockSpec((TM, D_MODEL), const2),
        pl.BlockSpec((1, TM, D_MODEL), const3), pl.BlockSpec((1, TM, D_MODEL), const3),
        pl.BlockSpec((1, TM, D_MODEL), const3),
        pl.BlockSpec((CONV_K, CONV_CH), const2), pl.BlockSpec((1, CONV_CH), const2),
        pl.BlockSpec((1, CONV_CH), const2), pl.BlockSpec((1, CONV_CH), const2),
        pl.BlockSpec((D_MODEL, D_MODEL), const2),
        pl.BlockSpec((1, D_MODEL), const2),
        pl.BlockSpec((D_MODEL, LANES), const2), pl.BlockSpec((1, LANES), const2),
        pl.BlockSpec((SUBLANES, LANES), const2),
        pl.BlockSpec(memory_space=pl.ANY), pl.BlockSpec(memory_space=pl.ANY), pl.BlockSpec(memory_space=pl.ANY),
    ]
    out_specs = [
        pl.BlockSpec((TM, D_MODEL), out_row),
        pl.BlockSpec((TM * ROW_CHUNKS, LANES), out_row),
        pl.BlockSpec((TM, LANES), out_row),
        pl.BlockSpec((SUBLANES, LANES), const2),
    ]
    out_shape = [
        jax.ShapeDtypeStruct(hp1_all.shape, F32),
        jax.ShapeDtypeStruct(n2_all.shape, F32),
        jax.ShapeDtypeStruct(meta_all.shape, F32),
        jax.ShapeDtypeStruct((SUBLANES, LANES), F32),
    ]
    return pl.pallas_call(
        functools.partial(_mix_kernel, prompt=False, tiles_per_seq=1, n_real=1),
        grid=(1,),
        in_specs=in_specs,
        out_specs=out_specs,
        out_shape=out_shape,
        scratch_shapes=[pltpu.VMEM((1, LANES), F32)],
        input_output_aliases={15: 0, 16: 1, 17: 2},
        compiler_params=_cparams(("arbitrary",)),
        name="mix_sample",
    )(o_attn, uext, x, g1, sc2, sh2, wdw, bdw.reshape(1, -1), lng.reshape(1, -1), lnb.reshape(1, -1),
      w_out_bf16, norm2_g.reshape(1, -1), w_router_pad, b_router_pad, base_in, hp1_all, n2_all, meta_all)


def _lane_cumsum(x, lane):
    s = 1
    while s < LANES:
        x = x + jnp.where(lane >= s, pltpu.roll(x, s, 1), 0)
        s *= 2
    return x


def _slots_kernel(meta_ref, cnt_ref, slots_ref, tmap_ref, einfo_ref, *, n_tiles_pad):
    shift = TM.bit_length() - 1
    lane8 = lax.broadcasted_iota(I32, (SUBLANES, LANES), 1)
    cnt = cnt_ref[...].astype(I32)
    padded = ((cnt + (TM - 1)) >> shift) << shift
    csum = _lane_cumsum(padded, lane8)
    gstart = csum - padded

    meta = meta_ref[...]
    lane = lax.broadcasted_iota(I32, meta.shape, 1)
    lane_f = lane.astype(F32)
    gstart_f = gstart[0:1, :].astype(F32)
    out = jnp.zeros(meta.shape, F32)
    for k in range(TOP_K):
        sel = lane_f == meta[:, k:k + 1]
        gs = jnp.sum(jnp.where(sel, gstart_f, 0.0), axis=-1, keepdims=True)
        out = out + jnp.where(lane == k, gs + meta[:, 2 * TOP_K + k: 2 * TOP_K + k + 1], 0.0)
    slots_ref[...] = out.astype(I32)

    @pl.when(pl.program_id(0) == 0)
    def _():
        ctiles = csum[0:1, :] >> shift
        n_valid = jnp.max(ctiles, axis=-1, keepdims=True)
        t = lax.broadcasted_iota(I32, (n_tiles_pad, LANES), 0)
        t = jnp.minimum(t, n_valid - 1)
        lane_t = lax.broadcasted_iota(I32, (n_tiles_pad, LANES), 1)
        hit = jnp.where((lane_t < N_EXPERTS) & (ctiles <= t), 1, 0)
        te = jnp.sum(hit, axis=-1, keepdims=True)
        tmap_ref[...] = jnp.broadcast_to(jnp.minimum(te, N_EXPERTS - 1), tmap_ref.shape)
        row8 = lax.broadcasted_iota(I32, (SUBLANES, LANES), 0)
        info = jnp.where(row8 == 0, gstart, 0)
        info = info + jnp.where(row8 == 1, padded, 0)
        info = info + jnp.where(row8 == 2, cnt, 0)
        info = info + jnp.where(row8 == 3, jnp.broadcast_to(n_valid, (SUBLANES, LANES)), 0)
        einfo_ref[...] = info


def _routing_slots(meta_all, counts, n_tiles_pad):
    n = meta_all.shape[0]
    const2 = lambda i: (0, 0)
    return pl.pallas_call(
        functools.partial(_slots_kernel, n_tiles_pad=n_tiles_pad),
        grid=(n // TM,),
        in_specs=[pl.BlockSpec((TM, LANES), lambda i: (i, 0)), pl.BlockSpec((SUBLANES, LANES), const2)],
        out_specs=[
            pl.BlockSpec((TM, LANES), lambda i: (i, 0)),
            pl.BlockSpec((n_tiles_pad, LANES), const2),
            pl.BlockSpec((SUBLANES, LANES), const2),
        ],
        out_shape=[
            jax.ShapeDtypeStruct((n, LANES), I32),
            jax.ShapeDtypeStruct((n_tiles_pad, LANES), I32),
            jax.ShapeDtypeStruct((SUBLANES, LANES), I32),
        ],
        compiler_params=_cparams(("arbitrary",)),
        name="routing_slots",
    )(meta_all, counts)


def _token_rows(ref, row):
    return ref.at[pl.ds(pl.multiple_of(row * ROW_CHUNKS, ROW_CHUNKS), ROW_CHUNKS)]


def _dispatch_kernel(gstart_ref, padded_ref, cnt_ref, nv_ref, slots_hbm, src_ref, xs_hbm, idx, zbuf, sem, zsem):
    i = pl.program_id(0)
    tile_rows = TM * ROW_CHUNKS
    n_slot_tiles = xs_hbm.shape[0] // tile_rows

    def zero_tile_copy(tile):
        start = pl.multiple_of(tile * tile_rows, tile_rows)
        return pltpu.make_async_copy(zbuf, xs_hbm.at[pl.ds(start, tile_rows)], zsem)

    def pad_tile_copy(e):
        return zero_tile_copy((gstart_ref[e] + padded_ref[e]) // TM - 1)

    @pl.when(i == 0)
    def _():
        zbuf[...] = jnp.zeros(zbuf.shape, zbuf.dtype)
        for e in range(N_EXPERTS):
            @pl.when(cnt_ref[e] > 0)
            def _():
                pad_tile_copy(e).start()

        def start_unused(t, carry):
            zero_tile_copy(t).start()
            return carry

        def wait_unused(t, carry):
            zero_tile_copy(t).wait()
            return carry

        lax.fori_loop(nv_ref[0], n_slot_tiles, start_unused, 0)
        for e in range(N_EXPERTS):
            @pl.when(cnt_ref[e] > 0)
            def _():
                pad_tile_copy(e).wait()
        lax.fori_loop(nv_ref[0], n_slot_tiles, wait_unused, 0)

    def for_each_row(fn):
        def body(r, carry):
            for k in range(TOP_K):
                fn(pltpu.make_async_copy(_token_rows(src_ref, r), _token_rows(xs_hbm, idx[r * TOP_K + k]), sem))
            return carry
        lax.fori_loop(0, TM, body, 0, unroll=8)

    pltpu.sync_copy(slots_hbm.at[i], idx)
    for_each_row(lambda c: c.start())
    for_each_row(lambda c: c.wait())


def _dispatch(gstart, padded, cnt, n_valid, slots2d, n2_all, s_max):
    nt = slots2d.shape[0]
    grid_spec = pltpu.PrefetchScalarGridSpec(
        num_scalar_prefetch=4,
        grid=(nt,),
        in_specs=[
            pl.BlockSpec(memory_space=pl.ANY),
            pl.BlockSpec((TM * ROW_CHUNKS, LANES), lambda i, *_: (i, 0)),
        ],
        out_specs=pl.BlockSpec(memory_space=pl.ANY),
        scratch_shapes=[
            pltpu.SMEM((TM * TOP_K,), I32),
            pltpu.VMEM((TM * ROW_CHUNKS, LANES), F32),
            pltpu.SemaphoreType.DMA,
            pltpu.SemaphoreType.DMA,
        ],
    )
    return pl.pallas_call(
        _dispatch_kernel,
        grid_spec=grid_spec,
        out_shape=jax.ShapeDtypeStruct((s_max * ROW_CHUNKS, LANES), F32),
        compiler_params=pltpu.CompilerParams(dimension_semantics=("arbitrary",), has_side_effects=True),
        name="moe_dispatch",
    )(gstart, padded, cnt, n_valid, slots2d, n2_all)


DEINT_BLOCK = 2 * LANES
EXPERT_VMEM_LIMIT = 56 * 1024 * 1024


def _expert_kernel(te_ref, nv_ref, x_ref, wgu_ref, wd_ref, b1g_ref, b1u_ref, b2_ref, y_ref, w1g, w1u, w2):
    i = pl.program_id(0)
    valid = i < nv_ref[0]
    new_expert = jnp.logical_or(i == 0, te_ref[i] != te_ref[jnp.maximum(i - 1, 0)])

    @pl.when(jnp.logical_and(valid, new_expert))
    def _():
        r = lax.broadcasted_iota(I32, (DEINT_BLOCK, DEINT_BLOCK), 0)
        c = lax.broadcasted_iota(I32, (DEINT_BLOCK, DEINT_BLOCK), 1)
        src_col = jnp.where(c < LANES, 2 * c, 2 * (c - LANES) + 1)
        perm = jnp.where(r == src_col, 1.0, 0.0).astype(BF16)
        for blk in range(wgu_ref.shape[2] // DEINT_BLOCK):
            cols = wgu_ref[0, :, blk * DEINT_BLOCK:(blk + 1) * DEINT_BLOCK].astype(BF16)
            split = jnp.dot(cols, perm, preferred_element_type=F32).astype(BF16)
            w1g[:, blk * LANES:(blk + 1) * LANES] = split[:, :LANES]
            w1u[:, blk * LANES:(blk + 1) * LANES] = split[:, LANES:]
        w2[...] = wd_ref[0].astype(BF16)

    @pl.when(valid)
    def _():
        x = _load_token_rows(x_ref, TM).astype(BF16)
        hg = jnp.dot(x, w1g[...], preferred_element_type=F32) + b1g_ref[0]
        hu = jnp.dot(x, w1u[...], preferred_element_type=F32) + b1u_ref[0]
        gate = jnp.minimum(hg, SWIGLU_LIMIT)
        up = jnp.clip(hu, -SWIGLU_LIMIT, SWIGLU_LIMIT)
        act = (up + 1.0) * gate * jax.nn.sigmoid(SWIGLU_ALPHA * gate)
        y = jnp.dot(act.astype(BF16), w2[...], preferred_element_type=F32) + b2_ref[0]
        _store_token_rows(y_ref, y)

    @pl.when(jnp.logical_not(valid))
    def _():
        y_ref[...] = jnp.zeros(y_ref.shape, y_ref.dtype)


def _expert_mlp(tile_expert, n_valid, xs, w_gate_up, w_down, b1g, b1u, b2):
    tile_rows = TM * ROW_CHUNKS
    nt = xs.shape[0] // tile_rows
    ff = w_down.shape[1]
    tile = lambda i, te, nv: (jnp.minimum(i, nv[0] - 1), 0)
    wsel = lambda i, te, nv: (te[i], 0, 0)
    grid_spec = pltpu.PrefetchScalarGridSpec(
        num_scalar_prefetch=2,
        grid=(nt,),
        in_specs=[
            pl.BlockSpec((tile_rows, LANES), tile),
            pl.BlockSpec((1, D_MODEL, 2 * ff), wsel),
            pl.BlockSpec((1, ff, D_MODEL), wsel),
            pl.BlockSpec((1, 1, ff), wsel),
            pl.BlockSpec((1, 1, ff), wsel),
            pl.BlockSpec((1, 1, D_MODEL), wsel),
        ],
        out_specs=pl.BlockSpec((tile_rows, LANES), lambda i, te, nv: (i, 0)),
        scratch_shapes=[
            pltpu.VMEM((D_MODEL, ff), BF16),
            pltpu.VMEM((D_MODEL, ff), BF16),
            pltpu.VMEM((ff, D_MODEL), BF16),
        ],
    )
    return pl.pallas_call(
        _expert_kernel,
        grid_spec=grid_spec,
        out_shape=jax.ShapeDtypeStruct(xs.shape, F32),
        compiler_params=pltpu.CompilerParams(dimension_semantics=("arbitrary",),
                                             vmem_limit_bytes=EXPERT_VMEM_LIMIT),
        name="expert_mlp",
    )(tile_expert, n_valid, xs, w_gate_up, w_down, b1g, b1u, b2)


def _combine_kernel(slots_hbm, ys_hbm, hp1_ref, meta_ref, g2_ref, nf_ref, o_ref, idx, buf, sems, *, tile0):
    i = pl.program_id(0)
    n_steps = pl.num_programs(0)
    cur = i % 2

    def for_each_row(b, fn):
        def body(r, carry):
            for k in range(TOP_K):
                fn(pltpu.make_async_copy(_token_rows(ys_hbm, idx[b, r * TOP_K + k]),
                                         _token_rows(buf.at[b, k], r), sems.at[b]))
            return carry
        lax.fori_loop(0, TM, body, 0, unroll=8)

    def fetch(step, b):
        pltpu.sync_copy(slots_hbm.at[tile0 + step], idx.at[b])
        for_each_row(b, lambda c: c.start())

    @pl.when(i == 0)
    def _():
        fetch(i, cur)

    @pl.when(i + 1 < n_steps)
    def _():
        fetch(i + 1, 1 - cur)

    for_each_row(cur, lambda c: c.wait())

    meta = meta_ref[...]
    f = jnp.zeros((TM, D_MODEL), F32)
    for k in range(TOP_K):
        f = f + meta[:, TOP_K + k: TOP_K + k + 1] * _load_token_rows(buf, TM, (cur, k))
    hp2 = hp1_ref[...] + g2_ref[0] * f
    o_ref[...] = hp2 * lax.rsqrt(jnp.mean(hp2 * hp2, axis=-1, keepdims=True) + EPS) * nf_ref[...]


def _combine(slots2d, ys, hp1_all, meta_all, g2, normf_g, tile0, n_rows, tiles_per_group):
    nt = n_rows // TM
    r = g2.shape[1]
    row_in = lambda i: (tile0 + i, 0)
    return pl.pallas_call(
        functools.partial(_combine_kernel, tile0=tile0),
        grid=(nt,),
        in_specs=[
            pl.BlockSpec(memory_space=pl.ANY),
            pl.BlockSpec(memory_space=pl.ANY),
            pl.BlockSpec((TM, D_MODEL), row_in),
            pl.BlockSpec((TM, LANES), row_in),
            pl.BlockSpec((1, r, D_MODEL), lambda i: (i // tiles_per_group, 0, 0)),
            pl.BlockSpec((1, D_MODEL), lambda i: (0, 0)),
        ],
        out_specs=pl.BlockSpec((TM, D_MODEL), lambda i: (i, 0)),
        out_shape=jax.ShapeDtypeStruct((n_rows, D_MODEL), F32),
        scratch_shapes=[
            pltpu.SMEM((2, TM * TOP_K), I32),
            pltpu.VMEM((2, TOP_K, TM * ROW_CHUNKS, LANES), F32),
            pltpu.SemaphoreType.DMA((2,)),
        ],
        compiler_params=_cparams(("arbitrary",)),
        name="moe_combine",
    )(slots2d, ys, hp1_all, meta_all, g2, normf_g.reshape(1, D_MODEL))


def kernel(x_prompt, x_sample, cache_k, cache_v, state_conv, page_table, c_prompt, c_sample, norm1_g, norm2_g, w_ada, b_ada, w_in, lambda_q1, lambda_k1, lambda_q2, lambda_k2, subln_g, w_dw, b_dw, conv_ln_g, conv_ln_b, w_out, w_router, b_router, w_gate_up, b_gate_up, w_down, b_down, normf_g):
    depth = norm1_g.shape[0]
    assert depth == 1, "single-layer step"
    bsz, seq, d = x_prompt.shape
    db, ts, _ = x_sample.shape
    n_p, n_s = bsz * seq, db * ts
    assert d == D_MODEL and n_s == TM and seq % TQ == 0 and n_p % TM == 0
    n_all = n_p + n_s
    n_pages = page_table.shape[1]
    past = n_pages * cache_k.shape[2]
    tiles_per_seq = seq // TM
    l = 0

    n_cond = bsz + db
    c_all = jnp.concatenate([c_prompt, c_sample], axis=0)
    c_all = jnp.pad(c_all, ((0, -n_cond % SUBLANES), (0, 0)))
    mod = _adaln(c_all, w_ada[l], b_ada[l])[:n_cond]
    mod_p = mod[:bsz].reshape(bsz, 1, 6, D_MODEL)
    mod_s = jnp.repeat(mod[bsz:], ts, axis=0).reshape(1, n_s, 6, D_MODEL)
    sh1p, sc1p, g1p, sh2p, sc2p, g2p = [mod_p[:, :, j] for j in range(6)]
    sh1s, sc1s, g1s, sh2s, sc2s, g2s = [mod_s[:, :, j] for j in range(6)]

    lams = [v[l].reshape(1, DQK) for v in (lambda_q1, lambda_k1, lambda_q2, lambda_k2)]
    w_in_b = w_in[l].astype(BF16)
    w_out_b = w_out[l].astype(BF16)
    conv_w = (w_dw[l], b_dw[l], conv_ln_g[l], conv_ln_b[l])
    w_router_pad = jnp.pad(w_router[l], ((0, 0), (0, LANES - N_EXPERTS)))
    b_router_pad = jnp.pad(b_router[l], (0, LANES - N_EXPERTS)).reshape(1, LANES)

    xp = x_prompt.reshape(n_p, D_MODEL)
    xs_tok = x_sample.reshape(n_s, D_MODEL)
    tabs_p = _rope_tables(jnp.arange(seq))
    tabs_s = _rope_tables(jnp.tile(past + jnp.arange(ts), db))
    q_hm, k_p, v_p, k_hm, v_hm, u_p = _inproj(xp, norm1_g[l], sc1p, sh1p, tabs_p, w_in_b,
                                              tiles_per_seq, tiles_per_seq, True)
    q_s, k_s, v_s, u_s = _inproj(xs_tok, norm1_g[l], sc1s, sh1s, tabs_s, w_in_b, 1, 1, False)

    o_p = _prompt_attention(q_hm, k_hm, v_hm, lams, subln_g[l], bsz, seq)
    o_s = _sample_attention(q_s, k_s, v_s, cache_k, cache_v, page_table, lams, subln_g[l])

    uext_s = jnp.concatenate([state_conv[l], u_s.reshape(db, ts, CONV_CH)], axis=1)

    zeros_base = jnp.zeros((SUBLANES, LANES), F32)
    hp1_all, n2_all, meta_all, cnt_p = _mix_prompt(
        o_p, u_p, xp, g1p, sc2p, sh2p, conv_w, w_out_b, norm2_g[l], w_router_pad, b_router_pad,
        zeros_base, n_all, tiles_per_seq)
    hp1_all, n2_all, meta_all, counts = _mix_sample(
        o_s, uext_s, xs_tok, g1s, sc2s, sh2s, conv_w, w_out_b, norm2_g[l], w_router_pad, b_router_pad,
        cnt_p, hp1_all, n2_all, meta_all, n_p // TM)

    n_tok_tiles = n_all // TM
    s_max = n_all * TOP_K + N_EXPERTS * TM
    n_slot_tiles = s_max // TM
    n_tiles_pad = -(-n_slot_tiles // SUBLANES) * SUBLANES
    slots, tmap, einfo = _routing_slots(meta_all, counts, n_tiles_pad)
    slots2d = slots[:, :TOP_K].reshape(n_tok_tiles, TM * TOP_K)
    tile_expert = tmap[:n_slot_tiles, 0]
    gstart, padded, cnt = einfo[0, :N_EXPERTS], einfo[1, :N_EXPERTS], einfo[2, :N_EXPERTS]
    n_valid = einfo[3, :1]

    x_sorted = _dispatch(gstart, padded, cnt, n_valid, slots2d, n2_all, s_max)

    bgu = b_gate_up[l]
    b1g = bgu[:, 0::2].reshape(N_EXPERTS, 1, -1)
    b1u = bgu[:, 1::2].reshape(N_EXPERTS, 1, -1)
    b2 = b_down[l].reshape(N_EXPERTS, 1, D_MODEL)
    y_sorted = _expert_mlp(tile_expert, n_valid, x_sorted, w_gate_up.reshape(w_gate_up.shape[1:]),
                           w_down.reshape(w_down.shape[1:]), b1g, b1u, b2)

    y_p = _combine(slots2d, y_sorted, hp1_all, meta_all, g2p, normf_g, 0, n_p, tiles_per_seq)
    y_s = _combine(slots2d, y_sorted, hp1_all, meta_all, g2s, normf_g, n_p // TM, n_s, 1)

    y_prompt = y_p.reshape(bsz, seq, D_MODEL)
    y_sample = y_s.reshape(db, ts, D_MODEL)
    k_prompt = k_p.reshape(1, bsz, seq, 2 * N_HEADS, DQK)
    v_prompt = v_p.reshape(1, bsz, seq, N_HEADS, DV)
    conv_prompt = u_p.reshape(bsz, seq, CONV_CH)[:, seq - (CONV_K - 1):][None]
    k_sample = k_s.reshape(1, db, ts, 2 * N_HEADS, DQK)
    v_sample = v_s.reshape(1, db, ts, N_HEADS, DV)
    conv_sample = uext_s[:, ts:][None]
    return (y_prompt, y_sample, k_prompt, v_prompt, conv_prompt, k_sample, v_sample, conv_sample)
```

```python
import functools
import math

import jax
import jax.numpy as jnp
from jax import lax
from jax.experimental import pallas as pl
from jax.experimental.pallas import tpu as pltpu

F32 = jnp.float32
BF16 = jnp.bfloat16
I32 = jnp.int32
HIGHEST = lax.Precision.HIGHEST

D_MODEL = 1024
N_HEADS = 4
DV = 128
DQK = 64
ROT_DIM = 16
ROPE_THETA = 500000.0
CONV_K = 31
CONV_CH = 512
QK_W = 512
V_W = 512
IN_WIDTH = 2 * QK_W + V_W + 2 * CONV_CH
N_EXPERTS = 32
TOP_K = 4
SWIGLU_LIMIT = 7.0
SWIGLU_ALPHA = 1.702
EPS = 1e-5
LAM_INIT = 0.8 - 0.6 * math.exp(-0.3 * 0)
PAGE_SIZE = 128
Q_SCALE = DQK ** -0.5 * math.log2(math.e)

LANES = 128
SUBLANES = 8
VMEM_LIMIT = 48 * 1024 * 1024
N_DMA_PRIORITIES = 2

TM = 256
TQ = 512
TK = 512
PAGES_PER_STEP = 16
HALO = 32
CONV_CHUNK = 64
NEG_INF = float("-inf")


def _cparams(sem):
    return pltpu.CompilerParams(dimension_semantics=sem, vmem_limit_bytes=VMEM_LIMIT)


def _adaln_kernel(c_ref, w_ref, b_ref, o_ref):
    c = c_ref[...]
    s = c * jax.nn.sigmoid(c)
    o_ref[...] = jnp.dot(s, w_ref[...], precision=HIGHEST, preferred_element_type=F32) + b_ref[...]


def _adaln(c_all, w, b):
    n, d = c_all.shape
    width = w.shape[1]
    bn = 1536
    return pl.pallas_call(
        _adaln_kernel,
        grid=(width // bn,),
        in_specs=[
            pl.BlockSpec((n, d), lambda j: (0, 0)),
            pl.BlockSpec((d, bn), lambda j: (0, j)),
            pl.BlockSpec((1, bn), lambda j: (0, j)),
        ],
        out_specs=pl.BlockSpec((n, bn), lambda j: (0, j)),
        out_shape=jax.ShapeDtypeStruct((n, width), F32),
        compiler_params=_cparams(("parallel",)),
        name="adaln",
    )(c_all, w, b.reshape(1, width))


def _rope_tables(pos):
    inv = ROPE_THETA ** (-jnp.arange(0, ROT_DIM, 2, dtype=F32) / ROT_DIM)
    ang = pos.astype(F32)[:, None] * inv
    cos, sin = jnp.cos(ang), jnp.sin(ang)
    half = ROT_DIM // 2
    ones = jnp.ones((pos.shape[0], DQK - ROT_DIM), F32)
    zeros_h = jnp.zeros((pos.shape[0], half), F32)
    zeros_r = jnp.zeros((pos.shape[0], DQK - ROT_DIM), F32)
    c64 = jnp.concatenate([cos, cos, ones], axis=1)
    a64 = jnp.concatenate([-sin, zeros_h, zeros_r], axis=1)
    b64 = jnp.concatenate([zeros_h, sin, zeros_r], axis=1)
    rep = LANES // DQK
    return jnp.tile(c64, (1, rep)), jnp.tile(a64, (1, rep)), jnp.tile(b64, (1, rep))


def _inproj_kernel(x_ref, g_ref, sc_ref, sh_ref, cos_ref, sa_ref, sb_ref, w_ref, *out_refs, head_major):
    x = x_ref[...]
    hn = x * lax.rsqrt(jnp.mean(x * x, axis=-1, keepdims=True) + EPS) * g_ref[...]
    hn = hn * (1.0 + sc_ref[0]) + sh_ref[0]
    proj = jnp.dot(hn.astype(BF16), w_ref[...], preferred_element_type=F32)
    cos, sa, sb = cos_ref[...], sa_ref[...], sb_ref[...]

    def rope(blk):
        return blk * cos + pltpu.roll(blk, LANES - ROT_DIM // 2, 1) * sa + pltpu.roll(blk, ROT_DIM // 2, 1) * sb

    a = proj[:, 2 * QK_W + V_W: 2 * QK_W + V_W + CONV_CH]
    gl = proj[:, 2 * QK_W + V_W + CONV_CH:]
    if head_major:
        qb_ref, k32_ref, v32_ref, kb_ref, vb_ref, u_ref = out_refs
    else:
        q32_ref, k32_ref, v32_ref, u_ref = out_refs
    u_ref[...] = a * jax.nn.sigmoid(gl)
    for h in range(N_HEADS):
        lo, hi = h * LANES, (h + 1) * LANES
        qh = rope(proj[:, lo:hi]) * Q_SCALE
        kh = rope(proj[:, QK_W + lo: QK_W + hi])
        vh = proj[:, 2 * QK_W + lo: 2 * QK_W + hi]
        k32_ref[:, lo:hi] = kh
        v32_ref[:, lo:hi] = vh
        if head_major:
            qb_ref[h] = qh.astype(BF16)
            kb_ref[h] = kh.astype(BF16)
            vb_ref[h] = vh.astype(BF16)
        else:
            q32_ref[:, lo:hi] = qh


def _inproj(x, norm_g, sc, sh, tabs, w_in_bf16, tiles_per_group, tiles_per_seq, head_major):
    n = x.shape[0]
    nt = n // TM
    r = sc.shape[1]
    row = lambda i: (i, 0)
    grp = lambda i: (i // tiles_per_group, 0, 0)
    tab = lambda i: (i % tiles_per_seq, 0)
    in_specs = [
        pl.BlockSpec((TM, D_MODEL), row),
        pl.BlockSpec((1, D_MODEL), lambda i: (0, 0)),
        pl.BlockSpec((1, r, D_MODEL), grp),
        pl.BlockSpec((1, r, D_MODEL), grp),
        pl.BlockSpec((TM, LANES), tab),
        pl.BlockSpec((TM, LANES), tab),
        pl.BlockSpec((TM, LANES), tab),
        pl.BlockSpec((D_MODEL, IN_WIDTH), lambda i: (0, 0)),
    ]
    wide = pl.BlockSpec((TM, QK_W), row)
    hm = pl.BlockSpec((N_HEADS, TM, LANES), lambda i: (0, i, 0))
    if head_major:
        out_specs = [hm, wide, wide, hm, hm, wide]
        out_shape = [
            jax.ShapeDtypeStruct((N_HEADS, n, LANES), BF16),
            jax.ShapeDtypeStruct((n, QK_W), F32),
            jax.ShapeDtypeStruct((n, V_W), F32),
            jax.ShapeDtypeStruct((N_HEADS, n, LANES), BF16),
            jax.ShapeDtypeStruct((N_HEADS, n, LANES), BF16),
            jax.ShapeDtypeStruct((n, CONV_CH), F32),
        ]
    else:
        out_specs = [wide, wide, wide, wide]
        out_shape = [jax.ShapeDtypeStruct((n, QK_W), F32)] * 4
    return pl.pallas_call(
        functools.partial(_inproj_kernel, head_major=head_major),
        grid=(nt,),
        in_specs=in_specs,
        out_specs=out_specs,
        out_shape=out_shape,
        compiler_params=_cparams(("parallel",)),
        name="inproj_hm" if head_major else "inproj",
    )(x, norm_g.reshape(1, D_MODEL), sc, sh, *tabs, w_in_bf16)


def _lambda_value(lq1, lk1, lq2, lk2):
    a = jnp.exp(jnp.sum(lq1[...] * lk1[...], axis=-1, keepdims=True))
    b = jnp.exp(jnp.sum(lq2[...] * lk2[...], axis=-1, keepdims=True))
    return a - b + LAM_INIT


def _diff_merge(o1, l1, o2, l2, lam, subln_g):
    o = o1 / l1 - lam * (o2 / l2)
    o = o * lax.rsqrt(jnp.mean(o * o, axis=-1, keepdims=True) + EPS) * subln_g
    return o * (1.0 - LAM_INIT)


def _attn_kernel(qt_ref, kt_ref, q_ref, k_ref, v_ref, lq1, lk1, lq2, lk2, sg_ref, o_ref, qs, m_s, l_s, acc):
    s_idx = pl.program_id(1)
    qi = qt_ref[s_idx]
    ki = kt_ref[s_idx]

    @pl.when(ki == 0)
    def _():
        for h in range(N_HEADS):
            q = q_ref[h]
            lane = lax.broadcasted_iota(I32, q.shape, 1)
            zero = jnp.zeros_like(q)
            qs[h, 0:TQ, :] = jnp.where(lane < DQK, q, zero)
            qs[h, TQ:2 * TQ, :] = jnp.where(lane >= DQK, q, zero)
        m_s[...] = jnp.full(m_s.shape, NEG_INF, F32)
        l_s[...] = jnp.zeros(l_s.shape, F32)
        acc[...] = jnp.zeros(acc.shape, F32)

    def update(h, masked):
        s = lax.dot_general(qs[h], k_ref[h], (((1,), (1,)), ((), ())), preferred_element_type=F32)
        if masked:
            row = lax.broadcasted_iota(I32, s.shape, 0) & (TQ - 1)
            col = lax.broadcasted_iota(I32, s.shape, 1)
            s = jnp.where(row >= col, s, NEG_INF)
        m_prev = m_s[h]
        m_next = jnp.maximum(m_prev, jnp.max(s, axis=1, keepdims=True))
        p = jnp.exp2(s - jnp.tile(m_next, (1, TK // LANES)))
        alpha = jnp.exp2(m_prev - m_next)
        l_s[h] = alpha * l_s[h] + jnp.sum(p, axis=1, keepdims=True)
        acc[h] = alpha * acc[h] + jnp.dot(p.astype(BF16), v_ref[h], preferred_element_type=F32)
        m_s[h] = m_next

    @pl.when(ki < qi)
    def _():
        for h in range(N_HEADS):
            update(h, False)

    @pl.when(ki == qi)
    def _():
        lam = _lambda_value(lq1, lk1, lq2, lk2)
        for h in range(N_HEADS):
            update(h, True)
            o = _diff_merge(acc[h, 0:TQ, :], l_s[h, 0:TQ, :], acc[h, TQ:2 * TQ, :], l_s[h, TQ:2 * TQ, :],
                            lam, sg_ref[...])
            o_ref[:, h * DV:(h + 1) * DV] = o.astype(o_ref.dtype)


def _prompt_attention(q_hm, k_hm, v_hm, lams, subln_g, batch, seq):
    nq = seq // TQ
    pairs = [(qi, ki) for qi in range(nq) for ki in range(qi + 1)]
    qt = jnp.asarray([p[0] for p in pairs], I32)
    kt = jnp.asarray([p[1] for p in pairs], I32)
    n = batch * seq
    vec = lambda b, s, qt, kt: (0, 0)
    grid_spec = pltpu.PrefetchScalarGridSpec(
        num_scalar_prefetch=2,
        grid=(batch, len(pairs)),
        in_specs=[
            pl.BlockSpec((N_HEADS, TQ, LANES), lambda b, s, qt, kt: (0, b * nq + qt[s], 0)),
            pl.BlockSpec((N_HEADS, TK, LANES), lambda b, s, qt, kt: (0, b * nq + kt[s], 0)),
            pl.BlockSpec((N_HEADS, TK, LANES), lambda b, s, qt, kt: (0, b * nq + kt[s], 0)),
            pl.BlockSpec((1, DQK), vec), pl.BlockSpec((1, DQK), vec),
            pl.BlockSpec((1, DQK), vec), pl.BlockSpec((1, DQK), vec),
            pl.BlockSpec((1, DV), vec),
        ],
        out_specs=pl.BlockSpec((TQ, N_HEADS * DV), lambda b, s, qt, kt: (b * nq + qt[s], 0)),
        scratch_shapes=[
            pltpu.VMEM((N_HEADS, 2 * TQ, LANES), BF16),
            pltpu.VMEM((N_HEADS, 2 * TQ, LANES), F32),
            pltpu.VMEM((N_HEADS, 2 * TQ, LANES), F32),
            pltpu.VMEM((N_HEADS, 2 * TQ, LANES), F32),
        ],
    )
    return pl.pallas_call(
        _attn_kernel,
        grid_spec=grid_spec,
        out_shape=jax.ShapeDtypeStruct((n, N_HEADS * DV), BF16),
        compiler_params=_cparams(("parallel", "arbitrary")),
        name="prompt_attn",
    )(qt, kt, q_hm, k_hm, v_hm, *lams, subln_g.reshape(1, DV))


def _paged_attn_kernel(pt_ref, q_ref, kn_ref, vn_ref, lq1, lk1, lq2, lk2, sg_ref, *rest, n_steps, ts):
    kp = rest[:PAGES_PER_STEP]
    vp = rest[PAGES_PER_STEP:2 * PAGES_PER_STEP]
    o_ref, qe, m_s, l_s, acc = rest[2 * PAGES_PER_STEP:]
    n_maps = 2 * N_HEADS
    rows = n_maps * ts
    p_idx = pl.program_id(1)

    @pl.when(p_idx == 0)
    def _():
        qt = jnp.concatenate([q_ref[...]] * n_maps, axis=0)
        row = lax.broadcasted_iota(I32, qt.shape, 0)
        col = lax.broadcasted_iota(I32, qt.shape, 1)
        same_map = (row >> (ts.bit_length() - 1)) == (col >> (DQK.bit_length() - 1))
        qe[...] = jnp.where(same_map, qt, 0.0).astype(BF16)
        m_s[...] = jnp.full(m_s.shape, NEG_INF, F32)
        l_s[...] = jnp.zeros(l_s.shape, F32)
        acc[...] = jnp.zeros(acc.shape, F32)

    def update(kmat, vmat, causal, k_transposed):
        if k_transposed:
            s = jnp.dot(qe[...], kmat, preferred_element_type=F32)
        else:
            s = lax.dot_general(qe[...], kmat, (((1,), (1,)), ((), ())), preferred_element_type=F32)
        if causal:
            row = lax.broadcasted_iota(I32, s.shape, 0) & (ts - 1)
            col = lax.broadcasted_iota(I32, s.shape, 1)
            s = jnp.where(row >= col, s, NEG_INF)
        m_prev = m_s[...]
        m_next = jnp.maximum(m_prev, jnp.max(s, axis=1, keepdims=True))
        p = jnp.exp2(s - m_next[:, 0:1])
        alpha = jnp.exp2(m_prev - m_next)
        l_s[...] = alpha * l_s[...] + jnp.sum(p, axis=1, keepdims=True)
        acc[...] = jnp.tile(alpha, (1, V_W // LANES)) * acc[...] + jnp.dot(
            p.astype(BF16), vmat, preferred_element_type=F32)
        m_s[...] = m_next

    kmat = jnp.concatenate([r[...].reshape(QK_W, PAGE_SIZE) for r in kp], axis=1).astype(BF16)
    vmat = jnp.concatenate(
        [jnp.concatenate([r[:, h, :] for h in range(N_HEADS)], axis=1) for r in vp], axis=0).astype(BF16)
    update(kmat, vmat, False, True)

    @pl.when(p_idx == n_steps - 1)
    def _():
        update(kn_ref[...].astype(BF16), vn_ref[...].astype(BF16), True, False)
        lam = _lambda_value(lq1, lk1, lq2, lk2)
        for h in range(N_HEADS):
            r1, r2 = 2 * h * ts, (2 * h + 1) * ts
            c0, c1 = h * DV, (h + 1) * DV
            o = _diff_merge(acc[r1:r1 + ts, c0:c1], l_s[r1:r1 + ts, :],
                            acc[r2:r2 + ts, c0:c1], l_s[r2:r2 + ts, :], lam, sg_ref[...])
            o_ref[:, c0:c1] = o


def _sample_attention(q_s, k_s, v_s, cache_k, cache_v, page_table, lams, subln_g):
    db, n_pages = page_table.shape
    ts = q_s.shape[0] // db
    n_pool = cache_k.shape[1]
    ck = jnp.transpose(cache_k, (0, 1, 3, 4, 2)).reshape(n_pool, 2 * N_HEADS, DQK, PAGE_SIZE)
    cv = cache_v.reshape(n_pool, PAGE_SIZE, N_HEADS, DV)
    n_steps = n_pages // PAGES_PER_STEP
    pt = page_table.reshape(-1).astype(I32)
    vec = lambda b, p, pt: (0, 0)
    new = lambda b, p, pt: (b, 0)

    def page_spec(j, block):
        def idx(b, p, pt):
            return (pt[b * n_pages + p * PAGES_PER_STEP + j], 0, 0, 0)
        return pl.BlockSpec(block, idx)

    k_block = (None, 2 * N_HEADS, DQK, PAGE_SIZE)
    v_block = (None, PAGE_SIZE, N_HEADS, DV)

    rows = 2 * N_HEADS * ts
    grid_spec = pltpu.PrefetchScalarGridSpec(
        num_scalar_prefetch=1,
        grid=(db, n_steps),
        in_specs=[
            pl.BlockSpec((ts, QK_W), new), pl.BlockSpec((ts, QK_W), new), pl.BlockSpec((ts, V_W), new),
            pl.BlockSpec((1, DQK), vec), pl.BlockSpec((1, DQK), vec),
            pl.BlockSpec((1, DQK), vec), pl.BlockSpec((1, DQK), vec),
            pl.BlockSpec((1, DV), vec),
        ] + [page_spec(j, k_block) for j in range(PAGES_PER_STEP)]
          + [page_spec(j, v_block) for j in range(PAGES_PER_STEP)],
        out_specs=pl.BlockSpec((ts, V_W), new),
        scratch_shapes=[
            pltpu.VMEM((rows, QK_W), BF16),
            pltpu.VMEM((rows, LANES), F32),
            pltpu.VMEM((rows, LANES), F32),
            pltpu.VMEM((rows, V_W), F32),
        ],
    )
    return pl.pallas_call(
        functools.partial(_paged_attn_kernel, n_steps=n_steps, ts=ts),
        grid_spec=grid_spec,
        out_shape=jax.ShapeDtypeStruct((db * ts, V_W), F32),
        compiler_params=_cparams(("parallel", "arbitrary")),
        name="paged_attn",
    )(pt, q_s, k_s, v_s, *lams, subln_g.reshape(1, DV), *([ck] * PAGES_PER_STEP), *([cv] * PAGES_PER_STEP))


ROW_CHUNKS = D_MODEL // LANES


def _store_token_rows(ref, x):
    t = x.shape[0]
    for j in range(ROW_CHUNKS):
        ref[pl.ds(j, t, stride=ROW_CHUNKS), :] = x[:, j * LANES:(j + 1) * LANES]


def _load_token_rows(ref, t, lead=()):
    return jnp.concatenate(
        [ref[lead + (pl.ds(j, t, stride=ROW_CHUNKS), slice(None))] for j in range(ROW_CHUNKS)], axis=1)


def _conv_ln_swish(y, lng, lnb):
    mu = jnp.mean(y, axis=-1, keepdims=True)
    var = jnp.mean(jnp.square(y - mu), axis=-1, keepdims=True)
    yn = (y - mu) * lax.rsqrt(var + EPS) * lng + lnb
    return yn * jax.nn.sigmoid(yn)


N_MIX_PROMPT_INPUTS = 16


def _mix_kernel(*refs, prompt, tiles_per_seq, n_real):
    i = pl.program_id(0)

    @pl.when(i < n_real)
    def _():
        _mix_body(*refs, prompt=prompt, tiles_per_seq=tiles_per_seq)

    if prompt:
        @pl.when(i == n_real)
        def _():
            for ref in refs[N_MIX_PROMPT_INPUTS:N_MIX_PROMPT_INPUTS + 3]:
                ref[...] = jnp.zeros(ref.shape, ref.dtype)


def _mix_body(*refs, prompt, tiles_per_seq):
    if prompt:
        (o_ref, ucur_ref, uhalo_ref, x_ref, g1_ref, sc2_ref, sh2_ref, wdw_ref, bdw_ref, lng_ref, lnb_ref,
         wout_ref, n2g_ref, wr_ref, br_ref, basein_ref,
         hp1_ref, n2_ref, meta_ref, cnt_ref, ext, shifted, base) = refs
    else:
        (o_ref, uext_ref, x_ref, g1_ref, sc2_ref, sh2_ref, wdw_ref, bdw_ref, lng_ref, lnb_ref,
         wout_ref, n2g_ref, wr_ref, br_ref, basein_ref, hp1_in, n2_in, meta_in,
         hp1_ref, n2_ref, meta_ref, cnt_ref, base) = refs
    i = pl.program_id(0)

    @pl.when(i == 0)
    def _():
        base[...] = basein_ref[0:1, :]

    if prompt:
        first = (i % tiles_per_seq) == 0
        halo = uhalo_ref[...]
        ext[0:HALO, :] = jnp.where(first, jnp.zeros_like(halo), halo)
        ext[HALO:HALO + TM, :] = ucur_ref[...]
        off = HALO - (CONV_K - 1)
        chunks = []
        for phase in range(SUBLANES):
            q_max = max([(off + j) // SUBLANES for j in range(CONV_K) if (off + j) % SUBLANES == phase])
            span = TM + SUBLANES * q_max
            shifted[phase, 0:span, :] = ext[pl.ds(phase, span), :]
        for c in range(TM // CONV_CHUNK):
            a = jnp.zeros((CONV_CHUNK, CONV_CH), F32) + bdw_ref[...]
            for j in range(CONV_K):
                phase, q = (off + j) % SUBLANES, (off + j) // SUBLANES
                start = c * CONV_CHUNK + SUBLANES * q
                a = a + wdw_ref[j:j + 1, :] * shifted[phase, start:start + CONV_CHUNK, :]
            chunks.append(a)
        y = jnp.concatenate(chunks, axis=0)
    else:
        nb, text, _ = uext_ref.shape
        ts = text - (CONV_K - 1)
        a = jnp.zeros((nb, ts, CONV_CH), F32) + bdw_ref[...]
        for j in range(CONV_K):
            a = a + wdw_ref[j:j + 1, :] * uext_ref[:, j:j + ts, :]
        y = a.reshape(nb * ts, CONV_CH)
    yc = _conv_ln_swish(y, lng_ref[...], lnb_ref[...])

    proj = (jnp.dot(o_ref[...].astype(BF16), wout_ref[0:N_HEADS * DV, :], preferred_element_type=F32)
            + jnp.dot(yc.astype(BF16), wout_ref[N_HEADS * DV:, :], preferred_element_type=F32))
    hp1 = x_ref[...] + g1_ref[0] * proj
    hp1_ref[...] = hp1
    n2 = hp1 * lax.rsqrt(jnp.mean(hp1 * hp1, axis=-1, keepdims=True) + EPS) * n2g_ref[...]
    n2 = n2 * (1.0 + sc2_ref[0]) + sh2_ref[0]
    _store_token_rows(n2_ref, n2)

    logits = jnp.dot(n2, wr_ref[...], precision=HIGHEST, preferred_element_type=F32) + br_ref[...]
    lane = lax.broadcasted_iota(I32, logits.shape, 1)
    lane_f = lane.astype(F32)
    lg = jnp.where(lane < N_EXPERTS, logits, NEG_INF)
    onehots, vals, idxs = [], [], []
    for _ in range(TOP_K):
        mx = jnp.max(lg, axis=-1, keepdims=True)
        idx = jnp.min(jnp.where(lg == mx, lane_f, float(LANES)), axis=-1, keepdims=True)
        oh = lane_f == idx
        lg = jnp.where(oh, NEG_INF, lg)
        onehots.append(oh)
        vals.append(mx)
        idxs.append(idx)
    exps = [jnp.exp(v - vals[0]) for v in vals]
    denom = exps[0] + exps[1] + exps[2] + exps[3]

    sel = jnp.zeros(logits.shape, F32)
    for oh in onehots:
        sel = sel + oh.astype(F32)
    r_i = lax.broadcasted_iota(I32, (TM, TM), 0)
    c_i = lax.broadcasted_iota(I32, (TM, TM), 1)
    ltri = (r_i > c_i).astype(BF16)
    before = jnp.dot(ltri, sel.astype(BF16), preferred_element_type=F32) + base[...]
    meta = jnp.zeros(logits.shape, F32)
    for k in range(TOP_K):
        rank = jnp.sum(jnp.where(onehots[k], before, 0.0), axis=-1, keepdims=True)
        meta = meta + jnp.where(lane == k, idxs[k], 0.0)
        meta = meta + jnp.where(lane == TOP_K + k, exps[k] / denom, 0.0)
        meta = meta + jnp.where(lane == 2 * TOP_K + k, rank, 0.0)
    meta_ref[...] = meta
    new_base = base[...] + jnp.sum(sel, axis=0, keepdims=True)
    base[...] = new_base
    cnt_ref[...] = jnp.broadcast_to(new_base, cnt_ref.shape)


def _mix_prompt(o_attn, u, x, g1, sc2, sh2, conv_w, w_out_bf16, norm2_g, w_router_pad, b_router_pad,
                base_in, n_total, tiles_per_seq):
    n = x.shape[0]
    nt = n // TM
    assert n_total == n + TM
    row = lambda i: (jnp.minimum(i, nt - 1), 0)
    out_row = lambda i: (i, 0)
    const2 = lambda i: (0, 0)
    grp = lambda i: (jnp.minimum(i, nt - 1) // tiles_per_seq, 0, 0)
    halo_row = lambda i: (jnp.maximum(jnp.minimum(i, nt - 1) * (TM // HALO) - 1, 0), 0)
    wdw, bdw, lng, lnb = conv_w
    in_specs = [
        pl.BlockSpec((TM, N_HEADS * DV), row),
        pl.BlockSpec((TM, CONV_CH), row),
        pl.BlockSpec((HALO, CONV_CH), halo_row),
        pl.BlockSpec((TM, D_MODEL), row),
        pl.BlockSpec((1, 1, D_MODEL), grp), pl.BlockSpec((1, 1, D_MODEL), grp), pl.BlockSpec((1, 1, D_MODEL), grp),
        pl.BlockSpec((CONV_K, CONV_CH), const2), pl.BlockSpec((1, CONV_CH), const2),
        pl.BlockSpec((1, CONV_CH), const2), pl.BlockSpec((1, CONV_CH), const2),
        pl.BlockSpec((D_MODEL, D_MODEL), const2),
        pl.BlockSpec((1, D_MODEL), const2),
        pl.BlockSpec((D_MODEL, LANES), const2), pl.BlockSpec((1, LANES), const2),
        pl.BlockSpec((SUBLANES, LANES), const2),
    ]
    assert len(in_specs) == N_MIX_PROMPT_INPUTS
    out_specs = [
        pl.BlockSpec((TM, D_MODEL), out_row),
        pl.BlockSpec((TM * ROW_CHUNKS, LANES), out_row),
        pl.BlockSpec((TM, LANES), out_row),
        pl.BlockSpec((SUBLANES, LANES), const2),
    ]
    out_shape = [
        jax.ShapeDtypeStruct((n_total, D_MODEL), F32),
        jax.ShapeDtypeStruct((n_total * ROW_CHUNKS, LANES), F32),
        jax.ShapeDtypeStruct((n_total, LANES), F32),
        jax.ShapeDtypeStruct((SUBLANES, LANES), F32),
    ]
    return pl.pallas_call(
        functools.partial(_mix_kernel, prompt=True, tiles_per_seq=tiles_per_seq, n_real=nt),
        grid=(nt + 1,),
        in_specs=in_specs,
        out_specs=out_specs,
        out_shape=out_shape,
        scratch_shapes=[
            pltpu.VMEM((HALO + TM, CONV_CH), F32),
            pltpu.VMEM((SUBLANES, HALO + TM, CONV_CH), F32),
            pltpu.VMEM((1, LANES), F32),
        ],
        compiler_params=_cparams(("arbitrary",)),
        name="mix_prompt",
    )(o_attn, u, u, x, g1, sc2, sh2, wdw, bdw.reshape(1, -1), lng.reshape(1, -1), lnb.reshape(1, -1),
      w_out_bf16, norm2_g.reshape(1, -1), w_router_pad, b_router_pad, base_in)


def _mix_sample(o_attn, uext, x, g1, sc2, sh2, conv_w, w_out_bf16, norm2_g, w_router_pad, b_router_pad,
                base_in, hp1_all, n2_all, meta_all, tile0):
    const2 = lambda i: (0, 0)
    const3 = lambda i: (0, 0, 0)
    out_row = lambda i: (tile0, 0)
    wdw, bdw, lng, lnb = conv_w
    nb, text, _ = uext.shape
    in_specs = [
        pl.BlockSpec((TM, N_HEADS * DV), const2),
        pl.BlockSpec((nb, text, CONV_CH), const3),
        pl.BlockSpec((TM, D_MODEL), const2),
        pl.BlockSpec((1, TM, D_MODEL), const3), pl.BlockSpec((1, TM, D_MODEL), const3),
        pl.BlockSpec((1, TM, D_MODEL), const3),
        pl.BlockSpec((CONV_K, CONV_CH), const2), pl.BlockSpec((1, CONV_CH), const2),
        pl.BlockSpec((1, CONV_CH), const2), pl.BlockSpec((1, CONV_CH), const2),
        pl.BlockSpec((D_MODEL, D_MODEL), const2),
        pl.BlockSpec((1, D_MODEL), const2),
        pl.BlockSpec((D_MODEL, LANES), const2), pl.BlockSpec((1, LANES), const2),
        pl.BlockSpec((SUBLANES, LANES), const2),
        pl.BlockSpec(memory_space=pl.ANY), pl.BlockSpec(memory_space=pl.ANY), pl.BlockSpec(memory_space=pl.ANY),
    ]
    out_specs = [
        pl.BlockSpec((TM, D_MODEL), out_row),
        pl.BlockSpec((TM * ROW_CHUNKS, LANES), out_row),
        pl.BlockSpec((TM, LANES), out_row),
        pl.BlockSpec((SUBLANES, LANES), const2),
    ]
    out_shape = [
        jax.ShapeDtypeStruct(hp1_all.shape, F32),
        jax.ShapeDtypeStruct(n2_all.shape, F32),
        jax.ShapeDtypeStruct(meta_all.shape, F32),
        jax.ShapeDtypeStruct((SUBLANES, LANES), F32),
    ]
    return pl.pallas_call(
        functools.partial(_mix_kernel, prompt=False, tiles_per_seq=1, n_real=1),
        grid=(1,),
        in_specs=in_specs,
        out_specs=out_specs,
        out_shape=out_shape,
        scratch_shapes=[pltpu.VMEM((1, LANES), F32)],
        input_output_aliases={15: 0, 16: 1, 17: 2},
        compiler_params=_cparams(("arbitrary",)),
        name="mix_sample",
    )(o_attn, uext, x, g1, sc2, sh2, wdw, bdw.reshape(1, -1), lng.reshape(1, -1), lnb.reshape(1, -1),
      w_out_bf16, norm2_g.reshape(1, -1), w_router_pad, b_router_pad, base_in, hp1_all, n2_all, meta_all)


MAX_SLOT_TILES = 16


def _lane_cumsum(x, lane):
    s = 1
    while s < LANES:
        x = x + jnp.where(lane >= s, pltpu.roll(x, s, 1), 0)
        s *= 2
    return x


def _slots_kernel(meta_ref, cnt_ref, slots_ref, tmap_ref, einfo_ref, *, n_tiles_pad):
    shift = TM.bit_length() - 1
    lane8 = lax.broadcasted_iota(I32, (SUBLANES, LANES), 1)
    cnt = cnt_ref[...].astype(I32)
    padded = ((cnt + (TM - 1)) >> shift) << shift
    csum = _lane_cumsum(padded, lane8)
    gstart = csum - padded

    meta = meta_ref[...]
    lane = lax.broadcasted_iota(I32, meta.shape, 1)
    lane_f = lane.astype(F32)
    gstart_f = gstart[0:1, :].astype(F32)
    out = jnp.zeros(meta.shape, F32)
    for k in range(TOP_K):
        sel = lane_f == meta[:, k:k + 1]
        gs = jnp.sum(jnp.where(sel, gstart_f, 0.0), axis=-1, keepdims=True)
        out = out + jnp.where(lane == k, gs + meta[:, 2 * TOP_K + k: 2 * TOP_K + k + 1], 0.0)
    slots_ref[...] = out.astype(I32)

    @pl.when(pl.program_id(0) == 0)
    def _():
        ctiles = csum[0:1, :] >> shift
        n_valid = jnp.max(ctiles, axis=-1, keepdims=True)
        t = lax.broadcasted_iota(I32, (n_tiles_pad, LANES), 0)
        t = jnp.minimum(t, n_valid - 1)
        lane_t = lax.broadcasted_iota(I32, (n_tiles_pad, LANES), 1)
        hit = jnp.where((lane_t < N_EXPERTS) & (ctiles <= t), 1, 0)
        te = jnp.sum(hit, axis=-1, keepdims=True)
        tmap_ref[...] = jnp.broadcast_to(jnp.minimum(te, N_EXPERTS - 1), tmap_ref.shape)
        row8 = lax.broadcasted_iota(I32, (SUBLANES, LANES), 0)
        info = jnp.where(row8 == 0, gstart, 0)
        info = info + jnp.where(row8 == 1, padded, 0)
        info = info + jnp.where(row8 == 2, cnt, 0)
        info = info + jnp.where(row8 == 3, jnp.broadcast_to(n_valid, (SUBLANES, LANES)), 0)
        einfo_ref[...] = info


def _routing_slots(meta_all, counts, n_tiles_pad):
    n = meta_all.shape[0]
    const2 = lambda i: (0, 0)
    n_tm = n // TM
    group = max(g for g in range(1, MAX_SLOT_TILES + 1) if n_tm % g == 0)
    rows = group * TM
    return pl.pallas_call(
        functools.partial(_slots_kernel, n_tiles_pad=n_tiles_pad),
        grid=(n // rows,),
        in_specs=[pl.BlockSpec((rows, LANES), lambda i: (i, 0)), pl.BlockSpec((SUBLANES, LANES), const2)],
        out_specs=[
            pl.BlockSpec((rows, LANES), lambda i: (i, 0)),
            pl.BlockSpec((n_tiles_pad, LANES), const2),
            pl.BlockSpec((SUBLANES, LANES), const2),
        ],
        out_shape=[
            jax.ShapeDtypeStruct((n, LANES), I32),
            jax.ShapeDtypeStruct((n_tiles_pad, LANES), I32),
            jax.ShapeDtypeStruct((SUBLANES, LANES), I32),
        ],
        compiler_params=_cparams(("arbitrary",)),
        name="routing_slots",
    )(meta_all, counts)


def _token_rows(ref, row):
    return ref.at[pl.ds(pl.multiple_of(row * ROW_CHUNKS, ROW_CHUNKS), ROW_CHUNKS)]


def _dispatch_kernel(gstart_ref, padded_ref, cnt_ref, nv_ref, slots_hbm, src_ref, xs_hbm, idx, zbuf, sem, zsem):
    i = pl.program_id(0)
    tile_rows = TM * ROW_CHUNKS
    n_slot_tiles = xs_hbm.shape[0] // tile_rows

    def zero_tile_copy(tile):
        start = pl.multiple_of(tile * tile_rows, tile_rows)
        return pltpu.make_async_copy(zbuf, xs_hbm.at[pl.ds(start, tile_rows)], zsem)

    def pad_tile_copy(e):
        return zero_tile_copy((gstart_ref[e] + padded_ref[e]) // TM - 1)

    @pl.when(i == 0)
    def _():
        zbuf[...] = jnp.zeros(zbuf.shape, zbuf.dtype)
        for e in range(N_EXPERTS):
            @pl.when(cnt_ref[e] > 0)
            def _():
                pad_tile_copy(e).start()

        def start_unused(t, carry):
            zero_tile_copy(t).start()
            return carry

        def wait_unused(t, carry):
            zero_tile_copy(t).wait()
            return carry

        lax.fori_loop(nv_ref[0], n_slot_tiles, start_unused, 0)
        for e in range(N_EXPERTS):
            @pl.when(cnt_ref[e] > 0)
            def _():
                pad_tile_copy(e).wait()
        lax.fori_loop(nv_ref[0], n_slot_tiles, wait_unused, 0)

    def for_each_row(fn):
        def body(r, carry):
            for k in range(TOP_K):
                fn(pltpu.make_async_copy(_token_rows(src_ref, r), _token_rows(xs_hbm, idx[r * TOP_K + k]), sem), k)
            return carry
        lax.fori_loop(0, TM, body, 0, unroll=8)

    pltpu.sync_copy(slots_hbm.at[i], idx)
    for_each_row(lambda c, k: c.start(priority=k % N_DMA_PRIORITIES))
    for_each_row(lambda c, k: c.wait())


def _dispatch(gstart, padded, cnt, n_valid, slots2d, n2_all, s_max):
    nt = slots2d.shape[0]
    grid_spec = pltpu.PrefetchScalarGridSpec(
        num_scalar_prefetch=4,
        grid=(nt,),
        in_specs=[
            pl.BlockSpec(memory_space=pl.ANY),
            pl.BlockSpec((TM * ROW_CHUNKS, LANES), lambda i, *_: (i, 0)),
        ],
        out_specs=pl.BlockSpec(memory_space=pl.ANY),
        scratch_shapes=[
            pltpu.SMEM((TM * TOP_K,), I32),
            pltpu.VMEM((TM * ROW_CHUNKS, LANES), F32),
            pltpu.SemaphoreType.DMA,
            pltpu.SemaphoreType.DMA,
        ],
    )
    return pl.pallas_call(
        _dispatch_kernel,
        grid_spec=grid_spec,
        out_shape=jax.ShapeDtypeStruct((s_max * ROW_CHUNKS, LANES), F32),
        compiler_params=pltpu.CompilerParams(dimension_semantics=("arbitrary",), has_side_effects=True),
        name="moe_dispatch",
    )(gstart, padded, cnt, n_valid, slots2d, n2_all)


DEINT_BLOCK = 2 * LANES
EXPERT_VMEM_LIMIT = 56 * 1024 * 1024


def _expert_kernel(te_ref, nv_ref, x_ref, wgu_hbm, wd_hbm, b1g_ref, b1u_ref, b2_ref, y_ref,
                   wgu_buf, wd_buf, w1g, w1u, w2, wsem, slot_ref):
    i = pl.program_id(0)
    n_valid = nv_ref[0]
    valid = i < n_valid
    expert = te_ref[i]
    new_expert = jnp.logical_or(i == 0, expert != te_ref[jnp.maximum(i - 1, 0)])
    last_tile = te_ref.shape[0] - 1

    def weight_copies(e, s):
        return (pltpu.make_async_copy(wgu_hbm.at[e], wgu_buf.at[s], wsem.at[0, s]),
                pltpu.make_async_copy(wd_hbm.at[e], wd_buf.at[s], wsem.at[1, s]))

    @pl.when(i == 0)
    def _():
        slot_ref[0] = 0
        for c in weight_copies(expert, 0):
            c.start()

    @pl.when(jnp.logical_and(valid, new_expert))
    def _():
        s = slot_ref[0]
        nxt = lax.while_loop(
            lambda j: jnp.logical_and(j < n_valid, te_ref[jnp.minimum(j, last_tile)] == expert),
            lambda j: j + 1, i + 1)

        @pl.when(nxt < n_valid)
        def _():
            for c in weight_copies(te_ref[jnp.minimum(nxt, last_tile)], 1 - s):
                c.start()

        for c in weight_copies(expert, s):
            c.wait()
        r = lax.broadcasted_iota(I32, (DEINT_BLOCK, DEINT_BLOCK), 0)
        c = lax.broadcasted_iota(I32, (DEINT_BLOCK, DEINT_BLOCK), 1)
        src_col = jnp.where(c < LANES, 2 * c, 2 * (c - LANES) + 1)
        perm = jnp.where(r == src_col, 1.0, 0.0).astype(BF16)
        for blk in range(wgu_buf.shape[2] // DEINT_BLOCK):
            cols = wgu_buf[s, :, blk * DEINT_BLOCK:(blk + 1) * DEINT_BLOCK].astype(BF16)
            split = jnp.dot(cols, perm, preferred_element_type=F32).astype(BF16)
            w1g[:, blk * LANES:(blk + 1) * LANES] = split[:, :LANES]
            w1u[:, blk * LANES:(blk + 1) * LANES] = split[:, LANES:]
        w2[...] = wd_buf[s].astype(BF16)
        slot_ref[0] = 1 - s

    @pl.when(valid)
    def _():
        x = _load_token_rows(x_ref, TM).astype(BF16)
        hg = jnp.dot(x, w1g[...], preferred_element_type=F32) + b1g_ref[0]
        hu = jnp.dot(x, w1u[...], preferred_element_type=F32) + b1u_ref[0]
        gate = jnp.minimum(hg, SWIGLU_LIMIT)
        up = jnp.clip(hu, -SWIGLU_LIMIT, SWIGLU_LIMIT)
        act = (up + 1.0) * gate * jax.nn.sigmoid(SWIGLU_ALPHA * gate)
        y = jnp.dot(act.astype(BF16), w2[...], preferred_element_type=F32) + b2_ref[0]
        _store_token_rows(y_ref, y)

    @pl.when(jnp.logical_not(valid))
    def _():
        y_ref[...] = jnp.zeros(y_ref.shape, y_ref.dtype)


def _expert_mlp(tile_expert, n_valid, xs, w_gate_up, w_down, b1g, b1u, b2):
    tile_rows = TM * ROW_CHUNKS
    nt = xs.shape[0] // tile_rows
    ff = w_down.shape[1]
    tile = lambda i, te, nv: (jnp.minimum(i, nv[0] - 1), 0)
    wsel = lambda i, te, nv: (te[i], 0, 0)
    grid_spec = pltpu.PrefetchScalarGridSpec(
        num_scalar_prefetch=2,
        grid=(nt,),
        in_specs=[
            pl.BlockSpec((tile_rows, LANES), tile),
            pl.BlockSpec(memory_space=pl.ANY),
            pl.BlockSpec(memory_space=pl.ANY),
            pl.BlockSpec((1, 1, ff), wsel),
            pl.BlockSpec((1, 1, ff), wsel),
            pl.BlockSpec((1, 1, D_MODEL), wsel),
        ],
        out_specs=pl.BlockSpec((tile_rows, LANES), lambda i, te, nv: (i, 0)),
        scratch_shapes=[
            pltpu.VMEM((2, D_MODEL, 2 * ff), F32),
            pltpu.VMEM((2, ff, D_MODEL), F32),
            pltpu.VMEM((D_MODEL, ff), BF16),
            pltpu.VMEM((D_MODEL, ff), BF16),
            pltpu.VMEM((ff, D_MODEL), BF16),
            pltpu.SemaphoreType.DMA((2, 2)),
            pltpu.SMEM((1,), I32),
        ],
    )
    return pl.pallas_call(
        _expert_kernel,
        grid_spec=grid_spec,
        out_shape=jax.ShapeDtypeStruct(xs.shape, F32),
        compiler_params=pltpu.CompilerParams(dimension_semantics=("arbitrary",),
                                             vmem_limit_bytes=EXPERT_VMEM_LIMIT),
        name="expert_mlp",
    )(tile_expert, n_valid, xs, w_gate_up, w_down, b1g, b1u, b2)


def _combine_kernel(slots_hbm, ys_hbm, hp1_ref, meta_ref, g2_ref, nf_ref, o_ref, idx, buf, sems, *, tile0):
    i = pl.program_id(0)
    n_steps = pl.num_programs(0)
    cur = i % 2

    def for_each_row(b, fn):
        def body(r, carry):
            for k in range(TOP_K):
                fn(pltpu.make_async_copy(_token_rows(ys_hbm, idx[b, r * TOP_K + k]),
                                         _token_rows(buf.at[b, k], r), sems.at[b]), k)
            return carry
        lax.fori_loop(0, TM, body, 0, unroll=8)

    def fetch(step, b):
        pltpu.sync_copy(slots_hbm.at[tile0 + step], idx.at[b])
        for_each_row(b, lambda c, k: c.start(priority=k % N_DMA_PRIORITIES))

    @pl.when(i == 0)
    def _():
        fetch(i, cur)

    @pl.when(i + 1 < n_steps)
    def _():
        fetch(i + 1, 1 - cur)

    for_each_row(cur, lambda c, k: c.wait())

    meta = meta_ref[...]
    f = jnp.zeros((TM, D_MODEL), F32)
    for k in range(TOP_K):
        f = f + meta[:, TOP_K + k: TOP_K + k + 1] * _load_token_rows(buf, TM, (cur, k))
    hp2 = hp1_ref[...] + g2_ref[0] * f
    o_ref[...] = hp2 * lax.rsqrt(jnp.mean(hp2 * hp2, axis=-1, keepdims=True) + EPS) * nf_ref[...]


def _combine(slots2d, ys, hp1_all, meta_all, g2, normf_g, tile0, n_rows, tiles_per_group):
    nt = n_rows // TM
    r = g2.shape[1]
    row_in = lambda i: (tile0 + i, 0)
    return pl.pallas_call(
        functools.partial(_combine_kernel, tile0=tile0),
        grid=(nt,),
        in_specs=[
            pl.BlockSpec(memory_space=pl.ANY),
            pl.BlockSpec(memory_space=pl.ANY),
            pl.BlockSpec((TM, D_MODEL), row_in),
            pl.BlockSpec((TM, LANES), row_in),
            pl.BlockSpec((1, r, D_MODEL), lambda i: (i // tiles_per_group, 0, 0)),
            pl.BlockSpec((1, D_MODEL), lambda i: (0, 0)),
        ],
        out_specs=pl.BlockSpec((TM, D_MODEL), lambda i: (i, 0)),
        out_shape=jax.ShapeDtypeStruct((n_rows, D_MODEL), F32),
        scratch_shapes=[
            pltpu.SMEM((2, TM * TOP_K), I32),
            pltpu.VMEM((2, TOP_K, TM * ROW_CHUNKS, LANES), F32),
            pltpu.SemaphoreType.DMA((2,)),
        ],
        compiler_params=_cparams(("arbitrary",)),
        name="moe_combine",
    )(slots2d, ys, hp1_all, meta_all, g2, normf_g.reshape(1, D_MODEL))


def kernel(x_prompt, x_sample, cache_k, cache_v, state_conv, page_table, c_prompt, c_sample, norm1_g, norm2_g, w_ada, b_ada, w_in, lambda_q1, lambda_k1, lambda_q2, lambda_k2, subln_g, w_dw, b_dw, conv_ln_g, conv_ln_b, w_out, w_router, b_router, w_gate_up, b_gate_up, w_down, b_down, normf_g):
    depth = norm1_g.shape[0]
    assert depth == 1, "single-layer step"
    bsz, seq, d = x_prompt.shape
    db, ts, _ = x_sample.shape
    n_p, n_s = bsz * seq, db * ts
    assert d == D_MODEL and n_s == TM and seq % TQ == 0 and n_p % TM == 0
    n_all = n_p + n_s
    n_pages = page_table.shape[1]
    past = n_pages * cache_k.shape[2]
    tiles_per_seq = seq // TM
    l = 0

    n_cond = bsz + db
    c_all = jnp.concatenate([c_prompt, c_sample], axis=0)
    c_all = jnp.pad(c_all, ((0, -n_cond % SUBLANES), (0, 0)))
    mod = _adaln(c_all, w_ada[l], b_ada[l])[:n_cond]
    mod_p = mod[:bsz].reshape(bsz, 1, 6, D_MODEL)
    mod_s = jnp.repeat(mod[bsz:], ts, axis=0).reshape(1, n_s, 6, D_MODEL)
    sh1p, sc1p, g1p, sh2p, sc2p, g2p = [mod_p[:, :, j] for j in range(6)]
    sh1s, sc1s, g1s, sh2s, sc2s, g2s = [mod_s[:, :, j] for j in range(6)]

    lams = [v[l].reshape(1, DQK) for v in (lambda_q1, lambda_k1, lambda_q2, lambda_k2)]
    w_in_b = w_in[l].astype(BF16)
    w_out_b = w_out[l].astype(BF16)
    conv_w = (w_dw[l], b_dw[l], conv_ln_g[l], conv_ln_b[l])
    w_router_pad = jnp.pad(w_router[l], ((0, 0), (0, LANES - N_EXPERTS)))
    b_router_pad = jnp.pad(b_router[l], (0, LANES - N_EXPERTS)).reshape(1, LANES)

    xp = x_prompt.reshape(n_p, D_MODEL)
    xs_tok = x_sample.reshape(n_s, D_MODEL)
    tabs_p = _rope_tables(jnp.arange(seq))
    tabs_s = _rope_tables(jnp.tile(past + jnp.arange(ts), db))
    q_hm, k_p, v_p, k_hm, v_hm, u_p = _inproj(xp, norm1_g[l], sc1p, sh1p, tabs_p, w_in_b,
                                              tiles_per_seq, tiles_per_seq, True)
    q_s, k_s, v_s, u_s = _inproj(xs_tok, norm1_g[l], sc1s, sh1s, tabs_s, w_in_b, 1, 1, False)

    o_p = _prompt_attention(q_hm, k_hm, v_hm, lams, subln_g[l], bsz, seq)
    o_s = _sample_attention(q_s, k_s, v_s, cache_k, cache_v, page_table, lams, subln_g[l])

    uext_s = jnp.concatenate([state_conv[l], u_s.reshape(db, ts, CONV_CH)], axis=1)

    zeros_base = jnp.zeros((SUBLANES, LANES), F32)
    hp1_all, n2_all, meta_all, cnt_p = _mix_prompt(
        o_p, u_p, xp, g1p, sc2p, sh2p, conv_w, w_out_b, norm2_g[l], w_router_pad, b_router_pad,
        zeros_base, n_all, tiles_per_seq)
    hp1_all, n2_all, meta_all, counts = _mix_sample(
        o_s, uext_s, xs_tok, g1s, sc2s, sh2s, conv_w, w_out_b, norm2_g[l], w_router_pad, b_router_pad,
        cnt_p, hp1_all, n2_all, meta_all, n_p // TM)

    n_tok_tiles = n_all // TM
    s_max = n_all * TOP_K + N_EXPERTS * TM
    n_slot_tiles = s_max // TM
    n_tiles_pad = -(-n_slot_tiles // SUBLANES) * SUBLANES
    slots, tmap, einfo = _routing_slots(meta_all, counts, n_tiles_pad)
    slots2d = slots[:, :TOP_K].reshape(n_tok_tiles, TM * TOP_K)
    tile_expert = tmap[:n_slot_tiles, 0]
    gstart, padded, cnt = einfo[0, :N_EXPERTS], einfo[1, :N_EXPERTS], einfo[2, :N_EXPERTS]
    n_valid = einfo[3, :1]

    x_sorted = _dispatch(gstart, padded, cnt, n_valid, slots2d, n2_all, s_max)

    bgu = b_gate_up[l]
    b1g = bgu[:, 0::2].reshape(N_EXPERTS, 1, -1)
    b1u = bgu[:, 1::2].reshape(N_EXPERTS, 1, -1)
    b2 = b_down[l].reshape(N_EXPERTS, 1, D_MODEL)
    y_sorted = _expert_mlp(tile_expert, n_valid, x_sorted, w_gate_up.reshape(w_gate_up.shape[1:]),
                           w_down.reshape(w_down.shape[1:]), b1g, b1u, b2)

    y_p = _combine(slots2d, y_sorted, hp1_all, meta_all, g2p, normf_g, 0, n_p, tiles_per_seq)
    y_s = _combine(slots2d, y_sorted, hp1_all, meta_all, g2s, normf_g, n_p // TM, n_s, 1)

    y_prompt = y_p.reshape(bsz, seq, D_MODEL)
    y_sample = y_s.reshape(db, ts, D_MODEL)
    k_prompt = k_p.reshape(1, bsz, seq, 2 * N_HEADS, DQK)
    v_prompt = v_p.reshape(1, bsz, seq, N_HEADS, DV)
    conv_prompt = u_p.reshape(bsz, seq, CONV_CH)[:, seq - (CONV_K - 1):][None]
    k_sample = k_s.reshape(1, db, ts, 2 * N_HEADS, DQK)
    v_sample = v_s.reshape(1, db, ts, N_HEADS, DV)
    conv_sample = uext_s[:, ts:][None]
    return (y_prompt, y_sample, k_prompt, v_prompt, conv_prompt, k_sample, v_sample, conv_sample)
```

```python
import functools
import math

import jax
import jax.numpy as jnp
from jax import lax
from jax.experimental import pallas as pl
from jax.experimental.pallas import tpu as pltpu

F32 = jnp.float32
BF16 = jnp.bfloat16
I32 = jnp.int32
HIGHEST = lax.Precision.HIGHEST

D_MODEL = 1024
N_HEADS = 4
DV = 128
DQK = 64
ROT_DIM = 16
ROPE_THETA = 500000.0
CONV_K = 31
CONV_CH = 512
QK_W = 512
V_W = 512
IN_WIDTH = 2 * QK_W + V_W + 2 * CONV_CH
N_EXPERTS = 32
TOP_K = 4
SWIGLU_LIMIT = 7.0
SWIGLU_ALPHA = 1.702
EPS = 1e-5
LAM_INIT = 0.8 - 0.6 * math.exp(-0.3 * 0)
PAGE_SIZE = 128
Q_SCALE = DQK ** -0.5 * math.log2(math.e)

LANES = 128
SUBLANES = 8
VMEM_LIMIT = 48 * 1024 * 1024
N_DMA_PRIORITIES = 2

TM = 256
TQ = 512
TK = 512
PAGES_PER_STEP = 16
HALO = 32
CONV_CHUNK = 64
NEG_INF = float("-inf")


def _cparams(sem):
    return pltpu.CompilerParams(dimension_semantics=sem, vmem_limit_bytes=VMEM_LIMIT)


def _adaln_kernel(c_ref, w_ref, b_ref, o_ref):
    c = c_ref[...]
    s = c * jax.nn.sigmoid(c)
    o_ref[...] = jnp.dot(s, w_ref[...], precision=HIGHEST, preferred_element_type=F32) + b_ref[...]


def _adaln(c_all, w, b):
    n, d = c_all.shape
    width = w.shape[1]
    bn = 1536
    return pl.pallas_call(
        _adaln_kernel,
        grid=(width // bn,),
        in_specs=[
            pl.BlockSpec((n, d), lambda j: (0, 0)),
            pl.BlockSpec((d, bn), lambda j: (0, j)),
            pl.BlockSpec((1, bn), lambda j: (0, j)),
        ],
        out_specs=pl.BlockSpec((n, bn), lambda j: (0, j)),
        out_shape=jax.ShapeDtypeStruct((n, width), F32),
        compiler_params=_cparams(("parallel",)),
        name="adaln",
    )(c_all, w, b.reshape(1, width))


def _rope_tables(pos):
    inv = ROPE_THETA ** (-jnp.arange(0, ROT_DIM, 2, dtype=F32) / ROT_DIM)
    ang = pos.astype(F32)[:, None] * inv
    cos, sin = jnp.cos(ang), jnp.sin(ang)
    half = ROT_DIM // 2
    ones = jnp.ones((pos.shape[0], DQK - ROT_DIM), F32)
    zeros_h = jnp.zeros((pos.shape[0], half), F32)
    zeros_r = jnp.zeros((pos.shape[0], DQK - ROT_DIM), F32)
    c64 = jnp.concatenate([cos, cos, ones], axis=1)
    a64 = jnp.concatenate([-sin, zeros_h, zeros_r], axis=1)
    b64 = jnp.concatenate([zeros_h, sin, zeros_r], axis=1)
    rep = LANES // DQK
    return jnp.tile(c64, (1, rep)), jnp.tile(a64, (1, rep)), jnp.tile(b64, (1, rep))


def _inproj_kernel(x_ref, g_ref, sc_ref, sh_ref, cos_ref, sa_ref, sb_ref, w_ref, *out_refs, head_major):
    x = x_ref[...]
    hn = x * lax.rsqrt(jnp.mean(x * x, axis=-1, keepdims=True) + EPS) * g_ref[...]
    hn = hn * (1.0 + sc_ref[0]) + sh_ref[0]
    proj = jnp.dot(hn.astype(BF16), w_ref[...], preferred_element_type=F32)
    cos, sa, sb = cos_ref[...], sa_ref[...], sb_ref[...]

    def rope(blk):
        return blk * cos + pltpu.roll(blk, LANES - ROT_DIM // 2, 1) * sa + pltpu.roll(blk, ROT_DIM // 2, 1) * sb

    a = proj[:, 2 * QK_W + V_W: 2 * QK_W + V_W + CONV_CH]
    gl = proj[:, 2 * QK_W + V_W + CONV_CH:]
    if head_major:
        qb_ref, k32_ref, v32_ref, kb_ref, vb_ref, u_ref = out_refs
    else:
        q32_ref, k32_ref, v32_ref, u_ref = out_refs
    u_ref[...] = a * jax.nn.sigmoid(gl)
    for h in range(N_HEADS):
        lo, hi = h * LANES, (h + 1) * LANES
        qh = rope(proj[:, lo:hi]) * Q_SCALE
        kh = rope(proj[:, QK_W + lo: QK_W + hi])
        vh = proj[:, 2 * QK_W + lo: 2 * QK_W + hi]
        k32_ref[:, lo:hi] = kh
        v32_ref[pl.ds(h, x.shape[0], stride=N_HEADS), :] = vh
        if head_major:
            qb_ref[h] = qh.astype(BF16)
            kb_ref[h] = kh.astype(BF16)
            vb_ref[h] = vh.astype(BF16)
        else:
            q32_ref[:, lo:hi] = qh


def _inproj(x, norm_g, sc, sh, tabs, w_in_bf16, tiles_per_group, tiles_per_seq, head_major):
    n = x.shape[0]
    nt = n // TM
    r = sc.shape[1]
    row = lambda i: (i, 0)
    grp = lambda i: (i // tiles_per_group, 0, 0)
    tab = lambda i: (i % tiles_per_seq, 0)
    in_specs = [
        pl.BlockSpec((TM, D_MODEL), row),
        pl.BlockSpec((1, D_MODEL), lambda i: (0, 0)),
        pl.BlockSpec((1, r, D_MODEL), grp),
        pl.BlockSpec((1, r, D_MODEL), grp),
        pl.BlockSpec((TM, LANES), tab),
        pl.BlockSpec((TM, LANES), tab),
        pl.BlockSpec((TM, LANES), tab),
        pl.BlockSpec((D_MODEL, IN_WIDTH), lambda i: (0, 0)),
    ]
    wide = pl.BlockSpec((TM, QK_W), row)
    hm = pl.BlockSpec((N_HEADS, TM, LANES), lambda i: (0, i, 0))
    v_rows = pl.BlockSpec((TM * N_HEADS, DV), row)
    v_shape = jax.ShapeDtypeStruct((n * N_HEADS, DV), F32)
    wide_shape = jax.ShapeDtypeStruct((n, QK_W), F32)
    if head_major:
        out_specs = [hm, wide, v_rows, hm, hm, wide]
        out_shape = [
            jax.ShapeDtypeStruct((N_HEADS, n, LANES), BF16),
            wide_shape,
            v_shape,
            jax.ShapeDtypeStruct((N_HEADS, n, LANES), BF16),
            jax.ShapeDtypeStruct((N_HEADS, n, LANES), BF16),
            wide_shape,
        ]
    else:
        out_specs = [wide, wide, v_rows, wide]
        out_shape = [wide_shape, wide_shape, v_shape, wide_shape]
    return pl.pallas_call(
        functools.partial(_inproj_kernel, head_major=head_major),
        grid=(nt,),
        in_specs=in_specs,
        out_specs=out_specs,
        out_shape=out_shape,
        compiler_params=_cparams(("parallel",)),
        name="inproj_hm" if head_major else "inproj",
    )(x, norm_g.reshape(1, D_MODEL), sc, sh, *tabs, w_in_bf16)


def _lambda_value(lq1, lk1, lq2, lk2):
    a = jnp.exp(jnp.sum(lq1[...] * lk1[...], axis=-1, keepdims=True))
    b = jnp.exp(jnp.sum(lq2[...] * lk2[...], axis=-1, keepdims=True))
    return a - b + LAM_INIT


def _diff_merge(o1, l1, o2, l2, lam, subln_g):
    o = o1 / l1 - lam * (o2 / l2)
    o = o * lax.rsqrt(jnp.mean(o * o, axis=-1, keepdims=True) + EPS) * subln_g
    return o * (1.0 - LAM_INIT)


def _attn_kernel(qt_ref, kt_ref, q_ref, k_ref, v_ref, lq1, lk1, lq2, lk2, sg_ref, o_ref, qs, m_s, l_s, acc):
    s_idx = pl.program_id(1)
    qi = qt_ref[s_idx]
    ki = kt_ref[s_idx]

    @pl.when(ki == 0)
    def _():
        for h in range(N_HEADS):
            q = q_ref[h]
            lane = lax.broadcasted_iota(I32, q.shape, 1)
            zero = jnp.zeros_like(q)
            qs[h, 0:TQ, :] = jnp.where(lane < DQK, q, zero)
            qs[h, TQ:2 * TQ, :] = jnp.where(lane >= DQK, q, zero)
        m_s[...] = jnp.full(m_s.shape, NEG_INF, F32)
        l_s[...] = jnp.zeros(l_s.shape, F32)
        acc[...] = jnp.zeros(acc.shape, F32)

    def update(h, masked):
        s = lax.dot_general(qs[h], k_ref[h], (((1,), (1,)), ((), ())), preferred_element_type=F32)
        if masked:
            row = lax.broadcasted_iota(I32, s.shape, 0) & (TQ - 1)
            col = lax.broadcasted_iota(I32, s.shape, 1)
            s = jnp.where(row >= col, s, NEG_INF)
        m_prev = m_s[h]
        m_next = jnp.maximum(m_prev, jnp.max(s, axis=1, keepdims=True))
        p = jnp.exp2(s - jnp.tile(m_next, (1, TK // LANES)))
        alpha = jnp.exp2(m_prev - m_next)
        l_s[h] = alpha * l_s[h] + jnp.sum(p, axis=1, keepdims=True)
        acc[h] = alpha * acc[h] + jnp.dot(p.astype(BF16), v_ref[h], preferred_element_type=F32)
        m_s[h] = m_next

    @pl.when(ki < qi)
    def _():
        for h in range(N_HEADS):
            update(h, False)

    @pl.when(ki == qi)
    def _():
        lam = _lambda_value(lq1, lk1, lq2, lk2)
        for h in range(N_HEADS):
            update(h, True)
            o = _diff_merge(acc[h, 0:TQ, :], l_s[h, 0:TQ, :], acc[h, TQ:2 * TQ, :], l_s[h, TQ:2 * TQ, :],
                            lam, sg_ref[...])
            o_ref[:, h * DV:(h + 1) * DV] = o.astype(o_ref.dtype)


def _prompt_attention(q_hm, k_hm, v_hm, lams, subln_g, batch, seq):
    nq = seq // TQ
    pairs = [(qi, ki) for qi in range(nq) for ki in range(qi + 1)]
    qt = jnp.asarray([p[0] for p in pairs], I32)
    kt = jnp.asarray([p[1] for p in pairs], I32)
    n = batch * seq
    vec = lambda b, s, qt, kt: (0, 0)
    grid_spec = pltpu.PrefetchScalarGridSpec(
        num_scalar_prefetch=2,
        grid=(batch, len(pairs)),
        in_specs=[
            pl.BlockSpec((N_HEADS, TQ, LANES), lambda b, s, qt, kt: (0, b * nq + qt[s], 0)),
            pl.BlockSpec((N_HEADS, TK, LANES), lambda b, s, qt, kt: (0, b * nq + kt[s], 0)),
            pl.BlockSpec((N_HEADS, TK, LANES), lambda b, s, qt, kt: (0, b * nq + kt[s], 0)),
            pl.BlockSpec((1, DQK), vec), pl.BlockSpec((1, DQK), vec),
            pl.BlockSpec((1, DQK), vec), pl.BlockSpec((1, DQK), vec),
            pl.BlockSpec((1, DV), vec),
        ],
        out_specs=pl.BlockSpec((TQ, N_HEADS * DV), lambda b, s, qt, kt: (b * nq + qt[s], 0)),
        scratch_shapes=[
            pltpu.VMEM((N_HEADS, 2 * TQ, LANES), BF16),
            pltpu.VMEM((N_HEADS, 2 * TQ, LANES), F32),
            pltpu.VMEM((N_HEADS, 2 * TQ, LANES), F32),
            pltpu.VMEM((N_HEADS, 2 * TQ, LANES), F32),
        ],
    )
    return pl.pallas_call(
        _attn_kernel,
        grid_spec=grid_spec,
        out_shape=jax.ShapeDtypeStruct((n, N_HEADS * DV), BF16),
        compiler_params=_cparams(("parallel", "arbitrary")),
        name="prompt_attn",
    )(qt, kt, q_hm, k_hm, v_hm, *lams, subln_g.reshape(1, DV))


def _paged_attn_kernel(pt_ref, q_ref, kn_ref, vn_ref, lq1, lk1, lq2, lk2, sg_ref, *rest, n_steps, ts):
    kp = rest[:PAGES_PER_STEP]
    vp = rest[PAGES_PER_STEP:2 * PAGES_PER_STEP]
    o_ref, qe, m_s, l_s, acc = rest[2 * PAGES_PER_STEP:]
    n_maps = 2 * N_HEADS
    rows = n_maps * ts
    p_idx = pl.program_id(1)

    @pl.when(p_idx == 0)
    def _():
        qt = jnp.concatenate([q_ref[...]] * n_maps, axis=0)
        row = lax.broadcasted_iota(I32, qt.shape, 0)
        col = lax.broadcasted_iota(I32, qt.shape, 1)
        same_map = (row >> (ts.bit_length() - 1)) == (col >> (DQK.bit_length() - 1))
        qe[...] = jnp.where(same_map, qt, 0.0).astype(BF16)
        m_s[...] = jnp.full(m_s.shape, NEG_INF, F32)
        l_s[...] = jnp.zeros(l_s.shape, F32)
        acc[...] = jnp.zeros(acc.shape, F32)

    def update(kmat, vmat, causal, k_transposed):
        if k_transposed:
            s = jnp.dot(qe[...], kmat, preferred_element_type=F32)
        else:
            s = lax.dot_general(qe[...], kmat, (((1,), (1,)), ((), ())), preferred_element_type=F32)
        if causal:
            row = lax.broadcasted_iota(I32, s.shape, 0) & (ts - 1)
            col = lax.broadcasted_iota(I32, s.shape, 1)
            s = jnp.where(row >= col, s, NEG_INF)
        m_prev = m_s[...]
        m_next = jnp.maximum(m_prev, jnp.max(s, axis=1, keepdims=True))
        p = jnp.exp2(s - m_next[:, 0:1])
        alpha = jnp.exp2(m_prev - m_next)
        l_s[...] = alpha * l_s[...] + jnp.sum(p, axis=1, keepdims=True)
        acc[...] = jnp.tile(alpha, (1, V_W // LANES)) * acc[...] + jnp.dot(
            p.astype(BF16), vmat, preferred_element_type=F32)
        m_s[...] = m_next

    kmat = jnp.concatenate([r[...].reshape(QK_W, PAGE_SIZE) for r in kp], axis=1).astype(BF16)
    def heads_on_lanes(ref, n_pos):
        return jnp.concatenate([ref[pl.ds(h, n_pos, stride=N_HEADS), :] for h in range(N_HEADS)], axis=1)

    vmat = jnp.concatenate([heads_on_lanes(r, PAGE_SIZE) for r in vp], axis=0).astype(BF16)
    update(kmat, vmat, False, True)

    @pl.when(p_idx == n_steps - 1)
    def _():
        update(kn_ref[...].astype(BF16), heads_on_lanes(vn_ref, ts).astype(BF16), True, False)
        lam = _lambda_value(lq1, lk1, lq2, lk2)
        for h in range(N_HEADS):
            r1, r2 = 2 * h * ts, (2 * h + 1) * ts
            c0, c1 = h * DV, (h + 1) * DV
            o = _diff_merge(acc[r1:r1 + ts, c0:c1], l_s[r1:r1 + ts, :],
                            acc[r2:r2 + ts, c0:c1], l_s[r2:r2 + ts, :], lam, sg_ref[...])
            o_ref[:, c0:c1] = o


def _sample_attention(q_s, k_s, v_s, cache_k, cache_v, page_table, lams, subln_g):
    db, n_pages = page_table.shape
    ts = q_s.shape[0] // db
    n_pool = cache_k.shape[1]
    ck = jnp.transpose(cache_k, (0, 1, 3, 4, 2)).reshape(n_pool, 2 * N_HEADS, DQK, PAGE_SIZE)
    cv = cache_v.reshape(n_pool, PAGE_SIZE * N_HEADS, DV)
    n_steps = n_pages // PAGES_PER_STEP
    pt = page_table.reshape(-1).astype(I32)
    vec = lambda b, p, pt: (0, 0)
    new = lambda b, p, pt: (b, 0)

    def page_spec(j, block):
        def idx(b, p, pt):
            return (pt[b * n_pages + p * PAGES_PER_STEP + j],) + (0,) * (len(block) - 1)
        return pl.BlockSpec(block, idx)

    k_block = (None, 2 * N_HEADS, DQK, PAGE_SIZE)
    v_block = (None, PAGE_SIZE * N_HEADS, DV)

    rows = 2 * N_HEADS * ts
    grid_spec = pltpu.PrefetchScalarGridSpec(
        num_scalar_prefetch=1,
        grid=(db, n_steps),
        in_specs=[
            pl.BlockSpec((ts, QK_W), new), pl.BlockSpec((ts, QK_W), new), pl.BlockSpec((ts * N_HEADS, DV), new),
            pl.BlockSpec((1, DQK), vec), pl.BlockSpec((1, DQK), vec),
            pl.BlockSpec((1, DQK), vec), pl.BlockSpec((1, DQK), vec),
            pl.BlockSpec((1, DV), vec),
        ] + [page_spec(j, k_block) for j in range(PAGES_PER_STEP)]
          + [page_spec(j, v_block) for j in range(PAGES_PER_STEP)],
        out_specs=pl.BlockSpec((ts, V_W), new),
        scratch_shapes=[
            pltpu.VMEM((rows, QK_W), BF16),
            pltpu.VMEM((rows, LANES), F32),
            pltpu.VMEM((rows, LANES), F32),
            pltpu.VMEM((rows, V_W), F32),
        ],
    )
    return pl.pallas_call(
        functools.partial(_paged_attn_kernel, n_steps=n_steps, ts=ts),
        grid_spec=grid_spec,
        out_shape=jax.ShapeDtypeStruct((db * ts, V_W), F32),
        compiler_params=_cparams(("parallel", "arbitrary")),
        name="paged_attn",
    )(pt, q_s, k_s, v_s, *lams, subln_g.reshape(1, DV), *([ck] * PAGES_PER_STEP), *([cv] * PAGES_PER_STEP))


ROW_CHUNKS = D_MODEL // LANES


def _store_token_rows(ref, x):
    t = x.shape[0]
    for j in range(ROW_CHUNKS):
        ref[pl.ds(j, t, stride=ROW_CHUNKS), :] = x[:, j * LANES:(j + 1) * LANES]


def _load_token_rows(ref, t, lead=()):
    return jnp.concatenate(
        [ref[lead + (pl.ds(j, t, stride=ROW_CHUNKS), slice(None))] for j in range(ROW_CHUNKS)], axis=1)


def _split_bf16(x):
    hi = x.astype(BF16)
    return hi, (x - hi.astype(F32)).astype(BF16)


def _conv_ln_swish(y, lng, lnb):
    mu = jnp.mean(y, axis=-1, keepdims=True)
    var = jnp.mean(jnp.square(y - mu), axis=-1, keepdims=True)
    yn = (y - mu) * lax.rsqrt(var + EPS) * lng + lnb
    return yn * jax.nn.sigmoid(yn)


N_MIX_PROMPT_INPUTS = 16


def _mix_kernel(*refs, prompt, tiles_per_seq, n_real):
    i = pl.program_id(0)

    @pl.when(i < n_real)
    def _():
        _mix_body(*refs, prompt=prompt, tiles_per_seq=tiles_per_seq)

    if prompt:
        @pl.when(i == n_real)
        def _():
            for ref in refs[N_MIX_PROMPT_INPUTS:N_MIX_PROMPT_INPUTS + 3]:
                ref[...] = jnp.zeros(ref.shape, ref.dtype)


def _mix_body(*refs, prompt, tiles_per_seq):
    if prompt:
        (o_ref, ucur_ref, uhalo_ref, x_ref, g1_ref, sc2_ref, sh2_ref, wdw_ref, bdw_ref, lng_ref, lnb_ref,
         wout_ref, n2g_ref, wr_ref, br_ref, basein_ref,
         hp1_ref, n2_ref, meta_ref, cnt_ref, ext, shifted, base) = refs
    else:
        (o_ref, uext_ref, x_ref, g1_ref, sc2_ref, sh2_ref, wdw_ref, bdw_ref, lng_ref, lnb_ref,
         wout_ref, n2g_ref, wr_ref, br_ref, basein_ref, hp1_in, n2_in, meta_in,
         hp1_ref, n2_ref, meta_ref, cnt_ref, base) = refs
    i = pl.program_id(0)

    @pl.when(i == 0)
    def _():
        base[...] = basein_ref[0:1, :]

    if prompt:
        first = (i % tiles_per_seq) == 0
        halo = uhalo_ref[...]
        ext[0:HALO, :] = jnp.where(first, jnp.zeros_like(halo), halo)
        ext[HALO:HALO + TM, :] = ucur_ref[...]
        off = HALO - (CONV_K - 1)
        chunks = []
        for phase in range(SUBLANES):
            q_max = max([(off + j) // SUBLANES for j in range(CONV_K) if (off + j) % SUBLANES == phase])
            span = TM + SUBLANES * q_max
            shifted[phase, 0:span, :] = ext[pl.ds(phase, span), :]
        for c in range(TM // CONV_CHUNK):
            a = jnp.zeros((CONV_CHUNK, CONV_CH), F32) + bdw_ref[...]
            for j in range(CONV_K):
                phase, q = (off + j) % SUBLANES, (off + j) // SUBLANES
                start = c * CONV_CHUNK + SUBLANES * q
                a = a + wdw_ref[j:j + 1, :] * shifted[phase, start:start + CONV_CHUNK, :]
            chunks.append(a)
        y = jnp.concatenate(chunks, axis=0)
    else:
        nb, text, _ = uext_ref.shape
        ts = text - (CONV_K - 1)
        a = jnp.zeros((nb, ts, CONV_CH), F32) + bdw_ref[...]
        for j in range(CONV_K):
            a = a + wdw_ref[j:j + 1, :] * uext_ref[:, j:j + ts, :]
        y = a.reshape(nb * ts, CONV_CH)
    yc = _conv_ln_swish(y, lng_ref[...], lnb_ref[...])

    proj = (jnp.dot(o_ref[...].astype(BF16), wout_ref[0:N_HEADS * DV, :], preferred_element_type=F32)
            + jnp.dot(yc.astype(BF16), wout_ref[N_HEADS * DV:, :], preferred_element_type=F32))
    hp1 = x_ref[...] + g1_ref[0] * proj
    hp1_ref[...] = hp1
    n2 = hp1 * lax.rsqrt(jnp.mean(hp1 * hp1, axis=-1, keepdims=True) + EPS) * n2g_ref[...]
    n2 = n2 * (1.0 + sc2_ref[0]) + sh2_ref[0]
    _store_token_rows(n2_ref, n2)

    n_hi, n_lo = _split_bf16(n2)
    w_hi, w_lo = _split_bf16(wr_ref[...])
    logits = (jnp.dot(n_hi, w_hi, preferred_element_type=F32)
              + (jnp.dot(n_hi, w_lo, preferred_element_type=F32) + jnp.dot(n_lo, w_hi, preferred_element_type=F32))
              + br_ref[...])
    lane = lax.broadcasted_iota(I32, logits.shape, 1)
    lane_f = lane.astype(F32)
    lg = jnp.where(lane < N_EXPERTS, logits, NEG_INF)
    onehots, vals, idxs = [], [], []
    for _ in range(TOP_K):
        mx = jnp.max(lg, axis=-1, keepdims=True)
        idx = jnp.min(jnp.where(lg == mx, lane_f, float(LANES)), axis=-1, keepdims=True)
        oh = lane_f == idx
        lg = jnp.where(oh, NEG_INF, lg)
        onehots.append(oh)
        vals.append(mx)
        idxs.append(idx)
    exps = [jnp.exp(v - vals[0]) for v in vals]
    denom = exps[0] + exps[1] + exps[2] + exps[3]

    sel = jnp.zeros(logits.shape, F32)
    for oh in onehots:
        sel = sel + oh.astype(F32)
    r_i = lax.broadcasted_iota(I32, (TM, TM), 0)
    c_i = lax.broadcasted_iota(I32, (TM, TM), 1)
    ltri = (r_i > c_i).astype(BF16)
    before = jnp.dot(ltri, sel.astype(BF16), preferred_element_type=F32) + base[...]
    meta = jnp.zeros(logits.shape, F32)
    for k in range(TOP_K):
        rank = jnp.sum(jnp.where(onehots[k], before, 0.0), axis=-1, keepdims=True)
        meta = meta + jnp.where(lane == k, idxs[k], 0.0)
        meta = meta + jnp.where(lane == TOP_K + k, exps[k] / denom, 0.0)
        meta = meta + jnp.where(lane == 2 * TOP_K + k, rank, 0.0)
    meta_ref[...] = meta
    new_base = base[...] + jnp.sum(sel, axis=0, keepdims=True)
    base[...] = new_base
    cnt_ref[...] = jnp.broadcast_to(new_base, cnt_ref.shape)


def _mix_prompt(o_attn, u, x, g1, sc2, sh2, conv_w, w_out_bf16, norm2_g, w_router_pad, b_router_pad,
                base_in, n_total, tiles_per_seq):
    n = x.shape[0]
    nt = n // TM
    assert n_total == n + TM
    row = lambda i: (jnp.minimum(i, nt - 1), 0)
    out_row = lambda i: (i, 0)
    const2 = lambda i: (0, 0)
    grp = lambda i: (jnp.minimum(i, nt - 1) // tiles_per_seq, 0, 0)
    halo_row = lambda i: (jnp.maximum(jnp.minimum(i, nt - 1) * (TM // HALO) - 1, 0), 0)
    wdw, bdw, lng, lnb = conv_w
    in_specs = [
        pl.BlockSpec((TM, N_HEADS * DV), row),
        pl.BlockSpec((TM, CONV_CH), row),
        pl.BlockSpec((HALO, CONV_CH), halo_row),
        pl.BlockSpec((TM, D_MODEL), row),
        pl.BlockSpec((1, 1, D_MODEL), grp), pl.BlockSpec((1, 1, D_MODEL), grp), pl.BlockSpec((1, 1, D_MODEL), grp),
        pl.BlockSpec((CONV_K, CONV_CH), const2), pl.BlockSpec((1, CONV_CH), const2),
        pl.BlockSpec((1, CONV_CH), const2), pl.BlockSpec((1, CONV_CH), const2),
        pl.BlockSpec((D_MODEL, D_MODEL), const2),
        pl.BlockSpec((1, D_MODEL), const2),
        pl.BlockSpec((D_MODEL, LANES), const2), pl.BlockSpec((1, LANES), const2),
        pl.BlockSpec((SUBLANES, LANES), const2),
    ]
    assert len(in_specs) == N_MIX_PROMPT_INPUTS
    out_specs = [
        pl.BlockSpec((TM, D_MODEL), out_row),
        pl.BlockSpec((TM * ROW_CHUNKS, LANES), out_row),
        pl.BlockSpec((TM, LANES), out_row),
        pl.BlockSpec((SUBLANES, LANES), const2),
    ]
    out_shape = [
        jax.ShapeDtypeStruct((n_total, D_MODEL), F32),
        jax.ShapeDtypeStruct((n_total * ROW_CHUNKS, LANES), F32),
        jax.ShapeDtypeStruct((n_total, LANES), F32),
        jax.ShapeDtypeStruct((SUBLANES, LANES), F32),
    ]
    return pl.pallas_call(
        functools.partial(_mix_kernel, prompt=True, tiles_per_seq=tiles_per_seq, n_real=nt),
        grid=(nt + 1,),
        in_specs=in_specs,
        out_specs=out_specs,
        out_shape=out_shape,
        scratch_shapes=[
            pltpu.VMEM((HALO + TM, CONV_CH), F32),
            pltpu.VMEM((SUBLANES, HALO + TM, CONV_CH), F32),
            pltpu.VMEM((1, LANES), F32),
        ],
        compiler_params=_cparams(("arbitrary",)),
        name="mix_prompt",
    )(o_attn, u, u, x, g1, sc2, sh2, wdw, bdw.reshape(1, -1), lng.reshape(1, -1), lnb.reshape(1, -1),
      w_out_bf16, norm2_g.reshape(1, -1), w_router_pad, b_router_pad, base_in)


def _mix_sample(o_attn, uext, x, g1, sc2, sh2, conv_w, w_out_bf16, norm2_g, w_router_pad, b_router_pad,
                base_in, hp1_all, n2_all, meta_all, tile0):
    const2 = lambda i: (0, 0)
    const3 = lambda i: (0, 0, 0)
    out_row = lambda i: (tile0, 0)
    wdw, bdw, lng, lnb = conv_w
    nb, text, _ = uext.shape
    in_specs = [
        pl.BlockSpec((TM, N_HEADS * DV), const2),
        pl.BlockSpec((nb, text, CONV_CH), const3),
        pl.BlockSpec((TM, D_MODEL), const2),
        pl.BlockSpec((1, TM, D_MODEL), const3), pl.BlockSpec((1, TM, D_MODEL), const3),
        pl.BlockSpec((1, TM, D_MODEL), const3),
        pl.BlockSpec((CONV_K, CONV_CH), const2), pl.BlockSpec((1, CONV_CH), const2),
        pl.BlockSpec((1, CONV_CH), const2), pl.BlockSpec((1, CONV_CH), const2),
        pl.BlockSpec((D_MODEL, D_MODEL), const2),
        pl.BlockSpec((1, D_MODEL), const2),
        pl.BlockSpec((D_MODEL, LANES), const2), pl.BlockSpec((1, LANES), const2),
        pl.BlockSpec((SUBLANES, LANES), const2),
        pl.BlockSpec(memory_space=pl.ANY), pl.BlockSpec(memory_space=pl.ANY), pl.BlockSpec(memory_space=pl.ANY),
    ]
    out_specs = [
        pl.BlockSpec((TM, D_MODEL), out_row),
        pl.BlockSpec((TM * ROW_CHUNKS, LANES), out_row),
        pl.BlockSpec((TM, LANES), out_row),
        pl.BlockSpec((SUBLANES, LANES), const2),
    ]
    out_shape = [
        jax.ShapeDtypeStruct(hp1_all.shape, F32),
        jax.ShapeDtypeStruct(n2_all.shape, F32),
        jax.ShapeDtypeStruct(meta_all.shape, F32),
        jax.ShapeDtypeStruct((SUBLANES, LANES), F32),
    ]
    return pl.pallas_call(
        functools.partial(_mix_kernel, prompt=False, tiles_per_seq=1, n_real=1),
        grid=(1,),
        in_specs=in_specs,
        out_specs=out_specs,
        out_shape=out_shape,
        scratch_shapes=[pltpu.VMEM((1, LANES), F32)],
        input_output_aliases={15: 0, 16: 1, 17: 2},
        compiler_params=_cparams(("arbitrary",)),
        name="mix_sample",
    )(o_attn, uext, x, g1, sc2, sh2, wdw, bdw.reshape(1, -1), lng.reshape(1, -1), lnb.reshape(1, -1),
      w_out_bf16, norm2_g.reshape(1, -1), w_router_pad, b_router_pad, base_in, hp1_all, n2_all, meta_all)


MAX_SLOT_TILES = 16


def _lane_cumsum(x, lane):
    s = 1
    while s < LANES:
        x = x + jnp.where(lane >= s, pltpu.roll(x, s, 1), 0)
        s *= 2
    return x


def _slots_kernel(meta_ref, cnt_ref, slots_ref, tmap_ref, einfo_ref, *, n_tiles_pad):
    shift = TM.bit_length() - 1
    lane8 = lax.broadcasted_iota(I32, (SUBLANES, LANES), 1)
    cnt = cnt_ref[...].astype(I32)
    padded = ((cnt + (TM - 1)) >> shift) << shift
    csum = _lane_cumsum(padded, lane8)
    gstart = csum - padded

    meta = meta_ref[...]
    lane = lax.broadcasted_iota(I32, meta.shape, 1)
    lane_f = lane.astype(F32)
    gstart_f = gstart[0:1, :].astype(F32)
    out = jnp.zeros(meta.shape, F32)
    for k in range(TOP_K):
        sel = lane_f == meta[:, k:k + 1]
        gs = jnp.sum(jnp.where(sel, gstart_f, 0.0), axis=-1, keepdims=True)
        out = out + jnp.where(lane == k, gs + meta[:, 2 * TOP_K + k: 2 * TOP_K + k + 1], 0.0)
    slots_ref[...] = out.astype(I32)

    @pl.when(pl.program_id(0) == 0)
    def _():
        ctiles = csum[0:1, :] >> shift
        n_valid = jnp.max(ctiles, axis=-1, keepdims=True)
        t = lax.broadcasted_iota(I32, (n_tiles_pad, LANES), 0)
        t = jnp.minimum(t, n_valid - 1)
        lane_t = lax.broadcasted_iota(I32, (n_tiles_pad, LANES), 1)
        hit = jnp.where((lane_t < N_EXPERTS) & (ctiles <= t), 1, 0)
        te = jnp.sum(hit, axis=-1, keepdims=True)
        tmap_ref[...] = jnp.broadcast_to(jnp.minimum(te, N_EXPERTS - 1), tmap_ref.shape)
        row8 = lax.broadcasted_iota(I32, (SUBLANES, LANES), 0)
        info = jnp.where(row8 == 0, gstart, 0)
        info = info + jnp.where(row8 == 1, padded, 0)
        info = info + jnp.where(row8 == 2, cnt, 0)
        info = info + jnp.where(row8 == 3, jnp.broadcast_to(n_valid, (SUBLANES, LANES)), 0)
        einfo_ref[...] = info


def _routing_slots(meta_all, counts, n_tiles_pad):
    n = meta_all.shape[0]
    const2 = lambda i: (0, 0)
    n_tm = n // TM
    group = max(g for g in range(1, MAX_SLOT_TILES + 1) if n_tm % g == 0)
    rows = group * TM
    return pl.pallas_call(
        functools.partial(_slots_kernel, n_tiles_pad=n_tiles_pad),
        grid=(n // rows,),
        in_specs=[pl.BlockSpec((rows, LANES), lambda i: (i, 0)), pl.BlockSpec((SUBLANES, LANES), const2)],
        out_specs=[
            pl.BlockSpec((rows, LANES), lambda i: (i, 0)),
            pl.BlockSpec((n_tiles_pad, LANES), const2),
            pl.BlockSpec((SUBLANES, LANES), const2),
        ],
        out_shape=[
            jax.ShapeDtypeStruct((n, LANES), I32),
            jax.ShapeDtypeStruct((n_tiles_pad, LANES), I32),
            jax.ShapeDtypeStruct((SUBLANES, LANES), I32),
        ],
        compiler_params=_cparams(("arbitrary",)),
        name="routing_slots",
    )(meta_all, counts)


def _token_rows(ref, row):
    return ref.at[pl.ds(pl.multiple_of(row * ROW_CHUNKS, ROW_CHUNKS), ROW_CHUNKS)]


def _dispatch_kernel(gstart_ref, padded_ref, cnt_ref, nv_ref, slots_hbm, src_ref, xs_hbm, idx, zbuf, sem, zsem):
    i = pl.program_id(0)
    tile_rows = TM * ROW_CHUNKS
    n_slot_tiles = xs_hbm.shape[0] // tile_rows

    def zero_tile_copy(tile):
        start = pl.multiple_of(tile * tile_rows, tile_rows)
        return pltpu.make_async_copy(zbuf, xs_hbm.at[pl.ds(start, tile_rows)], zsem)

    def pad_tile_copy(e):
        return zero_tile_copy((gstart_ref[e] + padded_ref[e]) // TM - 1)

    @pl.when(i == 0)
    def _():
        zbuf[...] = jnp.zeros(zbuf.shape, zbuf.dtype)
        for e in range(N_EXPERTS):
            @pl.when(cnt_ref[e] > 0)
            def _():
                pad_tile_copy(e).start()

        def start_unused(t, carry):
            zero_tile_copy(t).start()
            return carry

        def wait_unused(t, carry):
            zero_tile_copy(t).wait()
            return carry

        lax.fori_loop(nv_ref[0], n_slot_tiles, start_unused, 0)
        for e in range(N_EXPERTS):
            @pl.when(cnt_ref[e] > 0)
            def _():
                pad_tile_copy(e).wait()
        lax.fori_loop(nv_ref[0], n_slot_tiles, wait_unused, 0)

    def for_each_row(fn):
        def body(r, carry):
            for k in range(TOP_K):
                fn(pltpu.make_async_copy(_token_rows(src_ref, r), _token_rows(xs_hbm, idx[0, r * TOP_K + k]), sem), k)
            return carry
        lax.fori_loop(0, TM, body, 0, unroll=8)

    pltpu.sync_copy(slots_hbm.at[i], idx)
    for_each_row(lambda c, k: c.start(priority=k % N_DMA_PRIORITIES))
    for_each_row(lambda c, k: c.wait())


def _dispatch(gstart, padded, cnt, n_valid, slots2d, n2_all, s_max):
    nt = slots2d.shape[0]
    grid_spec = pltpu.PrefetchScalarGridSpec(
        num_scalar_prefetch=4,
        grid=(nt,),
        in_specs=[
            pl.BlockSpec(memory_space=pl.ANY),
            pl.BlockSpec((TM * ROW_CHUNKS, LANES), lambda i, *_: (i, 0)),
        ],
        out_specs=pl.BlockSpec(memory_space=pl.ANY),
        scratch_shapes=[
            pltpu.SMEM((1, TM * TOP_K), I32),
            pltpu.VMEM((TM * ROW_CHUNKS, LANES), F32),
            pltpu.SemaphoreType.DMA,
            pltpu.SemaphoreType.DMA,
        ],
    )
    return pl.pallas_call(
        _dispatch_kernel,
        grid_spec=grid_spec,
        out_shape=jax.ShapeDtypeStruct((s_max * ROW_CHUNKS, LANES), F32),
        compiler_params=pltpu.CompilerParams(dimension_semantics=("arbitrary",), has_side_effects=True),
        name="moe_dispatch",
    )(gstart, padded, cnt, n_valid, slots2d, n2_all)


DEINT_BLOCK = 2 * LANES
EXPERT_VMEM_LIMIT = 56 * 1024 * 1024


def _expert_kernel(te_ref, nv_ref, x_ref, wgu_hbm, wd_hbm, b1g_ref, b1u_ref, b2_ref, y_ref,
                   wgu_buf, wd_buf, w1g, w1u, w2, wsem, slot_ref):
    i = pl.program_id(0)
    n_valid = nv_ref[0]
    valid = i < n_valid
    expert = te_ref[i]
    new_expert = jnp.logical_or(i == 0, expert != te_ref[jnp.maximum(i - 1, 0)])
    last_tile = te_ref.shape[0] - 1

    def weight_copies(e, s):
        return (pltpu.make_async_copy(wgu_hbm.at[e], wgu_buf.at[s], wsem.at[0, s]),
                pltpu.make_async_copy(wd_hbm.at[e], wd_buf.at[s], wsem.at[1, s]))

    @pl.when(i == 0)
    def _():
        slot_ref[0] = 0
        for c in weight_copies(expert, 0):
            c.start()

    @pl.when(jnp.logical_and(valid, new_expert))
    def _():
        s = slot_ref[0]
        nxt = lax.while_loop(
            lambda j: jnp.logical_and(j < n_valid, te_ref[jnp.minimum(j, last_tile)] == expert),
            lambda j: j + 1, i + 1)

        @pl.when(nxt < n_valid)
        def _():
            for c in weight_copies(te_ref[jnp.minimum(nxt, last_tile)], 1 - s):
                c.start()

        for c in weight_copies(expert, s):
            c.wait()
        r = lax.broadcasted_iota(I32, (DEINT_BLOCK, DEINT_BLOCK), 0)
        c = lax.broadcasted_iota(I32, (DEINT_BLOCK, DEINT_BLOCK), 1)
        src_col = jnp.where(c < LANES, 2 * c, 2 * (c - LANES) + 1)
        perm = jnp.where(r == src_col, 1.0, 0.0).astype(BF16)
        for blk in range(wgu_buf.shape[2] // DEINT_BLOCK):
            cols = wgu_buf[s, :, blk * DEINT_BLOCK:(blk + 1) * DEINT_BLOCK].astype(BF16)
            split = jnp.dot(cols, perm, preferred_element_type=F32).astype(BF16)
            w1g[:, blk * LANES:(blk + 1) * LANES] = split[:, :LANES]
            w1u[:, blk * LANES:(blk + 1) * LANES] = split[:, LANES:]
        w2[...] = wd_buf[s].astype(BF16)
        slot_ref[0] = 1 - s

    @pl.when(valid)
    def _():
        x = _load_token_rows(x_ref, TM).astype(BF16)
        hg = jnp.dot(x, w1g[...], preferred_element_type=F32) + b1g_ref[0]
        hu = jnp.dot(x, w1u[...], preferred_element_type=F32) + b1u_ref[0]
        gate = jnp.minimum(hg, SWIGLU_LIMIT)
        up = jnp.clip(hu, -SWIGLU_LIMIT, SWIGLU_LIMIT)
        act = (up + 1.0) * gate * jax.nn.sigmoid(SWIGLU_ALPHA * gate)
        y = jnp.dot(act.astype(BF16), w2[...], preferred_element_type=F32) + b2_ref[0]
        _store_token_rows(y_ref, y)

    @pl.when(jnp.logical_not(valid))
    def _():
        y_ref[...] = jnp.zeros(y_ref.shape, y_ref.dtype)


def _expert_mlp(tile_expert, n_valid, xs, w_gate_up, w_down, b1g, b1u, b2):
    tile_rows = TM * ROW_CHUNKS
    nt = xs.shape[0] // tile_rows
    ff = w_down.shape[1]
    tile = lambda i, te, nv: (jnp.minimum(i, nv[0] - 1), 0)
    wsel = lambda i, te, nv: (te[i], 0, 0)
    grid_spec = pltpu.PrefetchScalarGridSpec(
        num_scalar_prefetch=2,
        grid=(nt,),
        in_specs=[
            pl.BlockSpec((tile_rows, LANES), tile),
            pl.BlockSpec(memory_space=pl.ANY),
            pl.BlockSpec(memory_space=pl.ANY),
            pl.BlockSpec((1, 1, ff), wsel),
            pl.BlockSpec((1, 1, ff), wsel),
            pl.BlockSpec((1, 1, D_MODEL), wsel),
        ],
        out_specs=pl.BlockSpec((tile_rows, LANES), lambda i, te, nv: (i, 0)),
        scratch_shapes=[
            pltpu.VMEM((2, D_MODEL, 2 * ff), F32),
            pltpu.VMEM((2, ff, D_MODEL), F32),
            pltpu.VMEM((D_MODEL, ff), BF16),
            pltpu.VMEM((D_MODEL, ff), BF16),
            pltpu.VMEM((ff, D_MODEL), BF16),
            pltpu.SemaphoreType.DMA((2, 2)),
            pltpu.SMEM((1,), I32),
        ],
    )
    return pl.pallas_call(
        _expert_kernel,
        grid_spec=grid_spec,
        out_shape=jax.ShapeDtypeStruct(xs.shape, F32),
        compiler_params=pltpu.CompilerParams(dimension_semantics=("arbitrary",),
                                             vmem_limit_bytes=EXPERT_VMEM_LIMIT),
        name="expert_mlp",
    )(tile_expert, n_valid, xs, w_gate_up, w_down, b1g, b1u, b2)


def _combine_kernel(slots_hbm, ys_hbm, hp1_ref, meta_ref, g2_ref, nf_ref, o_ref,
                    idx0, idx1, buf0, buf1, sems, *, tile0):
    i = pl.program_id(0)
    n_steps = pl.num_programs(0)
    bufs = (buf0, buf1)
    idxs = (idx0, idx1)

    def for_each_row(parity, fn):
        def body(r, carry):
            for k in range(TOP_K):
                fn(pltpu.make_async_copy(_token_rows(ys_hbm, idxs[parity][0, r * TOP_K + k]),
                                         _token_rows(bufs[parity].at[k], r), sems.at[parity]), k)
            return carry
        lax.fori_loop(0, TM, body, 0, unroll=8)

    def fetch(step, parity):
        pltpu.sync_copy(slots_hbm.at[tile0 + step], idxs[parity])
        for_each_row(parity, lambda c, k: c.start(priority=k % N_DMA_PRIORITIES))

    @pl.when(i == 0)
    def _():
        fetch(0, 0)

    def step(parity):
        @pl.when(i + 1 < n_steps)
        def _():
            fetch(i + 1, 1 - parity)

        for_each_row(parity, lambda c, k: c.wait())
        meta = meta_ref[...]
        f = jnp.zeros((TM, D_MODEL), F32)
        for k in range(TOP_K):
            f = f + meta[:, TOP_K + k: TOP_K + k + 1] * _load_token_rows(bufs[parity], TM, (k,))
        hp2 = hp1_ref[...] + g2_ref[0] * f
        o_ref[...] = hp2 * lax.rsqrt(jnp.mean(hp2 * hp2, axis=-1, keepdims=True) + EPS) * nf_ref[...]

    for parity in range(2):
        @pl.when(i % 2 == parity)
        def _():
            step(parity)


def _combine(slots2d, ys, hp1_all, meta_all, g2, normf_g, tile0, n_rows, tiles_per_group):
    nt = n_rows // TM
    r = g2.shape[1]
    row_in = lambda i: (tile0 + i, 0)
    return pl.pallas_call(
        functools.partial(_combine_kernel, tile0=tile0),
        grid=(nt,),
        in_specs=[
            pl.BlockSpec(memory_space=pl.ANY),
            pl.BlockSpec(memory_space=pl.ANY),
            pl.BlockSpec((TM, D_MODEL), row_in),
            pl.BlockSpec((TM, LANES), row_in),
            pl.BlockSpec((1, r, D_MODEL), lambda i: (i // tiles_per_group, 0, 0)),
            pl.BlockSpec((1, D_MODEL), lambda i: (0, 0)),
        ],
        out_specs=pl.BlockSpec((TM, D_MODEL), lambda i: (i, 0)),
        out_shape=jax.ShapeDtypeStruct((n_rows, D_MODEL), F32),
        scratch_shapes=[
            pltpu.SMEM((1, TM * TOP_K), I32),
            pltpu.SMEM((1, TM * TOP_K), I32),
            pltpu.VMEM((TOP_K, TM * ROW_CHUNKS, LANES), F32),
            pltpu.VMEM((TOP_K, TM * ROW_CHUNKS, LANES), F32),
            pltpu.SemaphoreType.DMA((2,)),
        ],
        compiler_params=_cparams(("arbitrary",)),
        name="moe_combine",
    )(slots2d, ys, hp1_all, meta_all, g2, normf_g.reshape(1, D_MODEL))


def kernel(x_prompt, x_sample, cache_k, cache_v, state_conv, page_table, c_prompt, c_sample, norm1_g, norm2_g, w_ada, b_ada, w_in, lambda_q1, lambda_k1, lambda_q2, lambda_k2, subln_g, w_dw, b_dw, conv_ln_g, conv_ln_b, w_out, w_router, b_router, w_gate_up, b_gate_up, w_down, b_down, normf_g):
    depth = norm1_g.shape[0]
    assert depth == 1, "single-layer step"
    bsz, seq, d = x_prompt.shape
    db, ts, _ = x_sample.shape
    n_p, n_s = bsz * seq, db * ts
    assert d == D_MODEL and n_s == TM and seq % TQ == 0 and n_p % TM == 0
    n_all = n_p + n_s
    n_pages = page_table.shape[1]
    past = n_pages * cache_k.shape[2]
    tiles_per_seq = seq // TM
    l = 0

    n_cond = bsz + db
    c_all = jnp.concatenate([c_prompt, c_sample], axis=0)
    c_all = jnp.pad(c_all, ((0, -n_cond % SUBLANES), (0, 0)))
    mod = _adaln(c_all, w_ada[l], b_ada[l])[:n_cond]
    mod_p = mod[:bsz].reshape(bsz, 1, 6, D_MODEL)
    mod_s = jnp.repeat(mod[bsz:], ts, axis=0).reshape(1, n_s, 6, D_MODEL)
    sh1p, sc1p, g1p, sh2p, sc2p, g2p = [mod_p[:, :, j] for j in range(6)]
    sh1s, sc1s, g1s, sh2s, sc2s, g2s = [mod_s[:, :, j] for j in range(6)]

    lams = [v[l].reshape(1, DQK) for v in (lambda_q1, lambda_k1, lambda_q2, lambda_k2)]
    w_in_b = w_in[l].astype(BF16)
    w_out_b = w_out[l].astype(BF16)
    conv_w = (w_dw[l], b_dw[l], conv_ln_g[l], conv_ln_b[l])
    w_router_pad = jnp.pad(w_router[l], ((0, 0), (0, LANES - N_EXPERTS)))
    b_router_pad = jnp.pad(b_router[l], (0, LANES - N_EXPERTS)).reshape(1, LANES)

    xp = x_prompt.reshape(n_p, D_MODEL)
    xs_tok = x_sample.reshape(n_s, D_MODEL)
    tabs_p = _rope_tables(jnp.arange(seq))
    tabs_s = _rope_tables(jnp.tile(past + jnp.arange(ts), db))
    q_hm, k_p, v_p, k_hm, v_hm, u_p = _inproj(xp, norm1_g[l], sc1p, sh1p, tabs_p, w_in_b,
                                              tiles_per_seq, tiles_per_seq, True)
    q_s, k_s, v_s, u_s = _inproj(xs_tok, norm1_g[l], sc1s, sh1s, tabs_s, w_in_b, 1, 1, False)

    o_p = _prompt_attention(q_hm, k_hm, v_hm, lams, subln_g[l], bsz, seq)
    o_s = _sample_attention(q_s, k_s, v_s, cache_k, cache_v, page_table, lams, subln_g[l])

    uext_s = jnp.concatenate([state_conv[l], u_s.reshape(db, ts, CONV_CH)], axis=1)

    zeros_base = jnp.zeros((SUBLANES, LANES), F32)
    hp1_all, n2_all, meta_all, cnt_p = _mix_prompt(
        o_p, u_p, xp, g1p, sc2p, sh2p, conv_w, w_out_b, norm2_g[l], w_router_pad, b_router_pad,
        zeros_base, n_all, tiles_per_seq)
    hp1_all, n2_all, meta_all, counts = _mix_sample(
        o_s, uext_s, xs_tok, g1s, sc2s, sh2s, conv_w, w_out_b, norm2_g[l], w_router_pad, b_router_pad,
        cnt_p, hp1_all, n2_all, meta_all, n_p // TM)

    n_tok_tiles = n_all // TM
    s_max = n_all * TOP_K + N_EXPERTS * TM
    n_slot_tiles = s_max // TM
    n_tiles_pad = -(-n_slot_tiles // SUBLANES) * SUBLANES
    slots, tmap, einfo = _routing_slots(meta_all, counts, n_tiles_pad)
    slots2d = slots[:, :TOP_K].reshape(n_tok_tiles, 1, TM * TOP_K)
    tile_expert = tmap[:n_slot_tiles, 0]
    gstart, padded, cnt = einfo[0, :N_EXPERTS], einfo[1, :N_EXPERTS], einfo[2, :N_EXPERTS]
    n_valid = einfo[3, :1]

    x_sorted = _dispatch(gstart, padded, cnt, n_valid, slots2d, n2_all, s_max)

    bgu = b_gate_up[l]
    b1g = bgu[:, 0::2].reshape(N_EXPERTS, 1, -1)
    b1u = bgu[:, 1::2].reshape(N_EXPERTS, 1, -1)
    b2 = b_down[l].reshape(N_EXPERTS, 1, D_MODEL)
    y_sorted = _expert_mlp(tile_expert, n_valid, x_sorted, w_gate_up.reshape(w_gate_up.shape[1:]),
                           w_down.reshape(w_down.shape[1:]), b1g, b1u, b2)

    y_p = _combine(slots2d, y_sorted, hp1_all, meta_all, g2p, normf_g, 0, n_p, tiles_per_seq)
    y_s = _combine(slots2d, y_sorted, hp1_all, meta_all, g2s, normf_g, n_p // TM, n_s, 1)

    y_prompt = y_p.reshape(bsz, seq, D_MODEL)
    y_sample = y_s.reshape(db, ts, D_MODEL)
    k_prompt = k_p.reshape(1, bsz, seq, 2 * N_HEADS, DQK)
    v_prompt = v_p.reshape(1, bsz, seq, N_HEADS, DV)
    conv_prompt = u_p.reshape(bsz, seq, CONV_CH)[:, seq - (CONV_K - 1):][None]
    k_sample = k_s.reshape(1, db, ts, 2 * N_HEADS, DQK)
    v_sample = v_s.reshape(1, db, ts, N_HEADS, DV)
    conv_sample = uext_s[:, ts:][None]
    return (y_prompt, y_sample, k_prompt, v_prompt, conv_prompt, k_sample, v_sample, conv_sample)
```

```python
import functools
import math

import jax
import jax.numpy as jnp
from jax import lax
from jax.experimental import pallas as pl
from jax.experimental.pallas import tpu as pltpu

F32 = jnp.float32
BF16 = jnp.bfloat16
I32 = jnp.int32
HIGHEST = lax.Precision.HIGHEST

D_MODEL = 1024
N_HEADS = 4
DV = 128
DQK = 64
ROT_DIM = 16
ROPE_THETA = 500000.0
CONV_K = 31
CONV_CH = 512
QK_W = 512
V_W = 512
IN_WIDTH = 2 * QK_W + V_W + 2 * CONV_CH
N_EXPERTS = 32
TOP_K = 4
SWIGLU_LIMIT = 7.0
SWIGLU_ALPHA = 1.702
EPS = 1e-5
LAM_INIT = 0.8 - 0.6 * math.exp(-0.3 * 0)
PAGE_SIZE = 128
Q_SCALE = DQK ** -0.5 * math.log2(math.e)

LANES = 128
SUBLANES = 8
VMEM_LIMIT = 48 * 1024 * 1024
N_DMA_PRIORITIES = 2

TM = 256
TQ = 512
TK = 512
PAGES_PER_STEP = 16
HALO = 32
CONV_CHUNK = 64
NEG_INF = float("-inf")


def _cparams(sem):
    return pltpu.CompilerParams(dimension_semantics=sem, vmem_limit_bytes=VMEM_LIMIT)


def _adaln_kernel(c_ref, w_ref, b_ref, o_ref):
    c = c_ref[...]
    s = c * jax.nn.sigmoid(c)
    o_ref[...] = jnp.dot(s, w_ref[...], precision=HIGHEST, preferred_element_type=F32) + b_ref[...]


def _adaln(c_all, w, b):
    n, d = c_all.shape
    width = w.shape[1]
    bn = 1536
    return pl.pallas_call(
        _adaln_kernel,
        grid=(width // bn,),
        in_specs=[
            pl.BlockSpec((n, d), lambda j: (0, 0)),
            pl.BlockSpec((d, bn), lambda j: (0, j)),
            pl.BlockSpec((1, bn), lambda j: (0, j)),
        ],
        out_specs=pl.BlockSpec((n, bn), lambda j: (0, j)),
        out_shape=jax.ShapeDtypeStruct((n, width), F32),
        compiler_params=_cparams(("parallel",)),
        name="adaln",
    )(c_all, w, b.reshape(1, width))


def _rope_tables(pos):
    inv = ROPE_THETA ** (-jnp.arange(0, ROT_DIM, 2, dtype=F32) / ROT_DIM)
    ang = pos.astype(F32)[:, None] * inv
    cos, sin = jnp.cos(ang), jnp.sin(ang)
    half = ROT_DIM // 2
    ones = jnp.ones((pos.shape[0], DQK - ROT_DIM), F32)
    zeros_h = jnp.zeros((pos.shape[0], half), F32)
    zeros_r = jnp.zeros((pos.shape[0], DQK - ROT_DIM), F32)
    c64 = jnp.concatenate([cos, cos, ones], axis=1)
    a64 = jnp.concatenate([-sin, zeros_h, zeros_r], axis=1)
    b64 = jnp.concatenate([zeros_h, sin, zeros_r], axis=1)
    rep = LANES // DQK
    return jnp.tile(c64, (1, rep)), jnp.tile(a64, (1, rep)), jnp.tile(b64, (1, rep))


def _inproj_kernel(x_ref, g_ref, sc_ref, sh_ref, cos_ref, sa_ref, sb_ref, w_ref, *out_refs, head_major):
    x = x_ref[...]
    hn = x * lax.rsqrt(jnp.mean(x * x, axis=-1, keepdims=True) + EPS) * g_ref[...]
    hn = hn * (1.0 + sc_ref[0]) + sh_ref[0]
    proj = jnp.dot(hn.astype(BF16), w_ref[...], preferred_element_type=F32)
    cos, sa, sb = cos_ref[...], sa_ref[...], sb_ref[...]

    def rope(blk):
        return blk * cos + pltpu.roll(blk, LANES - ROT_DIM // 2, 1) * sa + pltpu.roll(blk, ROT_DIM // 2, 1) * sb

    a = proj[:, 2 * QK_W + V_W: 2 * QK_W + V_W + CONV_CH]
    gl = proj[:, 2 * QK_W + V_W + CONV_CH:]
    if head_major:
        qb_ref, k32_ref, v32_ref, kb_ref, vb_ref, u_ref = out_refs
    else:
        q32_ref, k32_ref, v32_ref, u_ref = out_refs
    u_ref[...] = a * jax.nn.sigmoid(gl)
    for h in range(N_HEADS):
        lo, hi = h * LANES, (h + 1) * LANES
        qh = rope(proj[:, lo:hi]) * Q_SCALE
        kh = rope(proj[:, QK_W + lo: QK_W + hi])
        vh = proj[:, 2 * QK_W + lo: 2 * QK_W + hi]
        v32_ref[pl.ds(h, x.shape[0], stride=N_HEADS), :] = vh
        if head_major:
            k32_ref[0, 2 * h:2 * h + 2] = kh.T.reshape(2, DQK, x.shape[0])
            qb_ref[h] = qh.astype(BF16)
            kb_ref[h] = kh.astype(BF16)
            vb_ref[h] = vh.astype(BF16)
        else:
            k32_ref[:, lo:hi] = kh
            q32_ref[:, lo:hi] = qh


def _inproj(x, norm_g, sc, sh, tabs, w_in_bf16, tiles_per_group, tiles_per_seq, head_major):
    n = x.shape[0]
    nt = n // TM
    r = sc.shape[1]
    row = lambda i: (i, 0)
    grp = lambda i: (i // tiles_per_group, 0, 0)
    tab = lambda i: (i % tiles_per_seq, 0)
    in_specs = [
        pl.BlockSpec((TM, D_MODEL), row),
        pl.BlockSpec((1, D_MODEL), lambda i: (0, 0)),
        pl.BlockSpec((1, r, D_MODEL), grp),
        pl.BlockSpec((1, r, D_MODEL), grp),
        pl.BlockSpec((TM, LANES), tab),
        pl.BlockSpec((TM, LANES), tab),
        pl.BlockSpec((TM, LANES), tab),
        pl.BlockSpec((D_MODEL, IN_WIDTH), lambda i: (0, 0)),
    ]
    wide = pl.BlockSpec((TM, QK_W), row)
    hm = pl.BlockSpec((N_HEADS, TM, LANES), lambda i: (0, i, 0))
    v_rows = pl.BlockSpec((TM * N_HEADS, DV), row)
    v_shape = jax.ShapeDtypeStruct((n * N_HEADS, DV), F32)
    wide_shape = jax.ShapeDtypeStruct((n, QK_W), F32)
    if head_major:
        n_seq = nt // tiles_per_seq
        k_t = pl.BlockSpec((1, 2 * N_HEADS, DQK, TM), lambda i: (i // tiles_per_seq, 0, 0, i % tiles_per_seq))
        out_specs = [hm, k_t, v_rows, hm, hm, wide]
        out_shape = [
            jax.ShapeDtypeStruct((N_HEADS, n, LANES), BF16),
            jax.ShapeDtypeStruct((n_seq, 2 * N_HEADS, DQK, tiles_per_seq * TM), F32),
            v_shape,
            jax.ShapeDtypeStruct((N_HEADS, n, LANES), BF16),
            jax.ShapeDtypeStruct((N_HEADS, n, LANES), BF16),
            wide_shape,
        ]
    else:
        out_specs = [wide, wide, v_rows, wide]
        out_shape = [wide_shape, wide_shape, v_shape, wide_shape]
    return pl.pallas_call(
        functools.partial(_inproj_kernel, head_major=head_major),
        grid=(nt,),
        in_specs=in_specs,
        out_specs=out_specs,
        out_shape=out_shape,
        compiler_params=_cparams(("parallel",)),
        name="inproj_hm" if head_major else "inproj",
    )(x, norm_g.reshape(1, D_MODEL), sc, sh, *tabs, w_in_bf16)


def _lambda_value(lq1, lk1, lq2, lk2):
    a = jnp.exp(jnp.sum(lq1[...] * lk1[...], axis=-1, keepdims=True))
    b = jnp.exp(jnp.sum(lq2[...] * lk2[...], axis=-1, keepdims=True))
    return a - b + LAM_INIT


def _diff_merge(o1, l1, o2, l2, lam, subln_g):
    o = o1 / l1 - lam * (o2 / l2)
    o = o * lax.rsqrt(jnp.mean(o * o, axis=-1, keepdims=True) + EPS) * subln_g
    return o * (1.0 - LAM_INIT)


def _attn_kernel(qt_ref, kt_ref, q_ref, k_ref, v_ref, lq1, lk1, lq2, lk2, sg_ref, o_ref, qs, m_s, l_s, acc):
    s_idx = pl.program_id(1)
    qi = qt_ref[s_idx]
    ki = kt_ref[s_idx]

    @pl.when(ki == 0)
    def _():
        for h in range(N_HEADS):
            q = q_ref[h]
            lane = lax.broadcasted_iota(I32, q.shape, 1)
            zero = jnp.zeros_like(q)
            qs[h, 0:TQ, :] = jnp.where(lane < DQK, q, zero)
            qs[h, TQ:2 * TQ, :] = jnp.where(lane >= DQK, q, zero)
        m_s[...] = jnp.full(m_s.shape, NEG_INF, F32)
        l_s[...] = jnp.zeros(l_s.shape, F32)
        acc[...] = jnp.zeros(acc.shape, F32)

    def update(h, masked):
        s = lax.dot_general(qs[h], k_ref[h], (((1,), (1,)), ((), ())), preferred_element_type=F32)
        if masked:
            row = lax.broadcasted_iota(I32, s.shape, 0) & (TQ - 1)
            col = lax.broadcasted_iota(I32, s.shape, 1)
            s = jnp.where(row >= col, s, NEG_INF)
        m_prev = m_s[h]
        m_next = jnp.maximum(m_prev, jnp.max(s, axis=1, keepdims=True))
        p = jnp.exp2(s - jnp.tile(m_next, (1, TK // LANES)))
        alpha = jnp.exp2(m_prev - m_next)
        l_s[h] = alpha * l_s[h] + jnp.sum(p, axis=1, keepdims=True)
        acc[h] = alpha * acc[h] + jnp.dot(p.astype(BF16), v_ref[h], preferred_element_type=F32)
        m_s[h] = m_next

    @pl.when(ki < qi)
    def _():
        for h in range(N_HEADS):
            update(h, False)

    @pl.when(ki == qi)
    def _():
        lam = _lambda_value(lq1, lk1, lq2, lk2)
        for h in range(N_HEADS):
            update(h, True)
            o = _diff_merge(acc[h, 0:TQ, :], l_s[h, 0:TQ, :], acc[h, TQ:2 * TQ, :], l_s[h, TQ:2 * TQ, :],
                            lam, sg_ref[...])
            o_ref[:, h * DV:(h + 1) * DV] = o.astype(o_ref.dtype)


def _prompt_attention(q_hm, k_hm, v_hm, lams, subln_g, batch, seq):
    nq = seq // TQ
    pairs = [(qi, ki) for qi in range(nq) for ki in range(qi + 1)]
    qt = jnp.asarray([p[0] for p in pairs], I32)
    kt = jnp.asarray([p[1] for p in pairs], I32)
    n = batch * seq
    vec = lambda b, s, qt, kt: (0, 0)
    grid_spec = pltpu.PrefetchScalarGridSpec(
        num_scalar_prefetch=2,
        grid=(batch, len(pairs)),
        in_specs=[
            pl.BlockSpec((N_HEADS, TQ, LANES), lambda b, s, qt, kt: (0, b * nq + qt[s], 0)),
            pl.BlockSpec((N_HEADS, TK, LANES), lambda b, s, qt, kt: (0, b * nq + kt[s], 0)),
            pl.BlockSpec((N_HEADS, TK, LANES), lambda b, s, qt, kt: (0, b * nq + kt[s], 0)),
            pl.BlockSpec((1, DQK), vec), pl.BlockSpec((1, DQK), vec),
            pl.BlockSpec((1, DQK), vec), pl.BlockSpec((1, DQK), vec),
            pl.BlockSpec((1, DV), vec),
        ],
        out_specs=pl.BlockSpec((TQ, N_HEADS * DV), lambda b, s, qt, kt: (b * nq + qt[s], 0)),
        scratch_shapes=[
            pltpu.VMEM((N_HEADS, 2 * TQ, LANES), BF16),
            pltpu.VMEM((N_HEADS, 2 * TQ, LANES), F32),
            pltpu.VMEM((N_HEADS, 2 * TQ, LANES), F32),
            pltpu.VMEM((N_HEADS, 2 * TQ, LANES), F32),
        ],
    )
    return pl.pallas_call(
        _attn_kernel,
        grid_spec=grid_spec,
        out_shape=jax.ShapeDtypeStruct((n, N_HEADS * DV), BF16),
        compiler_params=_cparams(("parallel", "arbitrary")),
        name="prompt_attn",
    )(qt, kt, q_hm, k_hm, v_hm, *lams, subln_g.reshape(1, DV))


def _paged_attn_kernel(pt_ref, q_ref, kn_ref, vn_ref, lq1, lk1, lq2, lk2, sg_ref, *rest, n_steps, ts):
    kp = rest[:PAGES_PER_STEP]
    vp = rest[PAGES_PER_STEP:2 * PAGES_PER_STEP]
    o_ref, qe, m_s, l_s, acc = rest[2 * PAGES_PER_STEP:]
    n_maps = 2 * N_HEADS
    rows = n_maps * ts
    p_idx = pl.program_id(1)

    @pl.when(p_idx == 0)
    def _():
        qt = jnp.concatenate([q_ref[...]] * n_maps, axis=0)
        row = lax.broadcasted_iota(I32, qt.shape, 0)
        col = lax.broadcasted_iota(I32, qt.shape, 1)
        same_map = (row >> (ts.bit_length() - 1)) == (col >> (DQK.bit_length() - 1))
        qe[...] = jnp.where(same_map, qt, 0.0).astype(BF16)
        m_s[...] = jnp.full(m_s.shape, NEG_INF, F32)
        l_s[...] = jnp.zeros(l_s.shape, F32)
        acc[...] = jnp.zeros(acc.shape, F32)

    def update(kmat, vmat, causal, k_transposed):
        if k_transposed:
            s = jnp.dot(qe[...], kmat, preferred_element_type=F32)
        else:
            s = lax.dot_general(qe[...], kmat, (((1,), (1,)), ((), ())), preferred_element_type=F32)
        if causal:
            row = lax.broadcasted_iota(I32, s.shape, 0) & (ts - 1)
            col = lax.broadcasted_iota(I32, s.shape, 1)
            s = jnp.where(row >= col, s, NEG_INF)
        m_prev = m_s[...]
        m_next = jnp.maximum(m_prev, jnp.max(s, axis=1, keepdims=True))
        p = jnp.exp2(s - m_next[:, 0:1])
        alpha = jnp.exp2(m_prev - m_next)
        l_s[...] = alpha * l_s[...] + jnp.sum(p, axis=1, keepdims=True)
        acc[...] = jnp.tile(alpha, (1, V_W // LANES)) * acc[...] + jnp.dot(
            p.astype(BF16), vmat, preferred_element_type=F32)
        m_s[...] = m_next

    kmat = jnp.concatenate([r[...].reshape(QK_W, PAGE_SIZE) for r in kp], axis=1).astype(BF16)
    def heads_on_lanes(ref, n_pos):
        return jnp.concatenate([ref[pl.ds(h, n_pos, stride=N_HEADS), :] for h in range(N_HEADS)], axis=1)

    vmat = jnp.concatenate([heads_on_lanes(r, PAGE_SIZE) for r in vp], axis=0).astype(BF16)
    update(kmat, vmat, False, True)

    @pl.when(p_idx == n_steps - 1)
    def _():
        update(kn_ref[...].astype(BF16), heads_on_lanes(vn_ref, ts).astype(BF16), True, False)
        lam = _lambda_value(lq1, lk1, lq2, lk2)
        for h in range(N_HEADS):
            r1, r2 = 2 * h * ts, (2 * h + 1) * ts
            c0, c1 = h * DV, (h + 1) * DV
            o = _diff_merge(acc[r1:r1 + ts, c0:c1], l_s[r1:r1 + ts, :],
                            acc[r2:r2 + ts, c0:c1], l_s[r2:r2 + ts, :], lam, sg_ref[...])
            o_ref[:, c0:c1] = o


def _sample_attention(q_s, k_s, v_s, cache_k, cache_v, page_table, lams, subln_g):
    db, n_pages = page_table.shape
    ts = q_s.shape[0] // db
    n_pool = cache_k.shape[1]
    ck = jnp.transpose(cache_k, (0, 1, 3, 4, 2)).reshape(n_pool, 2 * N_HEADS, DQK, PAGE_SIZE)
    cv = cache_v.reshape(n_pool, PAGE_SIZE * N_HEADS, DV)
    n_steps = n_pages // PAGES_PER_STEP
    pt = page_table.reshape(-1).astype(I32)
    vec = lambda b, p, pt: (0, 0)
    new = lambda b, p, pt: (b, 0)

    def page_spec(j, block):
        def idx(b, p, pt):
            return (pt[b * n_pages + p * PAGES_PER_STEP + j],) + (0,) * (len(block) - 1)
        return pl.BlockSpec(block, idx)

    k_block = (None, 2 * N_HEADS, DQK, PAGE_SIZE)
    v_block = (None, PAGE_SIZE * N_HEADS, DV)

    rows = 2 * N_HEADS * ts
    grid_spec = pltpu.PrefetchScalarGridSpec(
        num_scalar_prefetch=1,
        grid=(db, n_steps),
        in_specs=[
            pl.BlockSpec((ts, QK_W), new), pl.BlockSpec((ts, QK_W), new), pl.BlockSpec((ts * N_HEADS, DV), new),
            pl.BlockSpec((1, DQK), vec), pl.BlockSpec((1, DQK), vec),
            pl.BlockSpec((1, DQK), vec), pl.BlockSpec((1, DQK), vec),
            pl.BlockSpec((1, DV), vec),
        ] + [page_spec(j, k_block) for j in range(PAGES_PER_STEP)]
          + [page_spec(j, v_block) for j in range(PAGES_PER_STEP)],
        out_specs=pl.BlockSpec((ts, V_W), new),
        scratch_shapes=[
            pltpu.VMEM((rows, QK_W), BF16),
            pltpu.VMEM((rows, LANES), F32),
            pltpu.VMEM((rows, LANES), F32),
            pltpu.VMEM((rows, V_W), F32),
        ],
    )
    return pl.pallas_call(
        functools.partial(_paged_attn_kernel, n_steps=n_steps, ts=ts),
        grid_spec=grid_spec,
        out_shape=jax.ShapeDtypeStruct((db * ts, V_W), F32),
        compiler_params=_cparams(("parallel", "arbitrary")),
        name="paged_attn",
    )(pt, q_s, k_s, v_s, *lams, subln_g.reshape(1, DV), *([ck] * PAGES_PER_STEP), *([cv] * PAGES_PER_STEP))


ROW_CHUNKS = D_MODEL // LANES


def _store_token_rows(ref, x):
    t = x.shape[0]
    for j in range(ROW_CHUNKS):
        ref[pl.ds(j, t, stride=ROW_CHUNKS), :] = x[:, j * LANES:(j + 1) * LANES]


def _load_token_rows(ref, t, lead=()):
    return jnp.concatenate(
        [ref[lead + (pl.ds(j, t, stride=ROW_CHUNKS), slice(None))] for j in range(ROW_CHUNKS)], axis=1)


def _split_bf16(x):
    hi = x.astype(BF16)
    return hi, (x - hi.astype(F32)).astype(BF16)


def _conv_ln_swish(y, lng, lnb):
    mu = jnp.mean(y, axis=-1, keepdims=True)
    var = jnp.mean(jnp.square(y - mu), axis=-1, keepdims=True)
    yn = (y - mu) * lax.rsqrt(var + EPS) * lng + lnb
    return yn * jax.nn.sigmoid(yn)


N_MIX_PROMPT_INPUTS = 16


def _mix_kernel(*refs, prompt, tiles_per_seq, n_real):
    i = pl.program_id(0)

    @pl.when(i < n_real)
    def _():
        _mix_body(*refs, prompt=prompt, tiles_per_seq=tiles_per_seq)

    if prompt:
        @pl.when(i == n_real)
        def _():
            for ref in refs[N_MIX_PROMPT_INPUTS:N_MIX_PROMPT_INPUTS + 3]:
                ref[...] = jnp.zeros(ref.shape, ref.dtype)


def _mix_body(*refs, prompt, tiles_per_seq):
    if prompt:
        (o_ref, ucur_ref, uhalo_ref, x_ref, g1_ref, sc2_ref, sh2_ref, wdw_ref, bdw_ref, lng_ref, lnb_ref,
         wout_ref, n2g_ref, wr_ref, br_ref, basein_ref,
         hp1_ref, n2_ref, meta_ref, cnt_ref, ext, shifted, base) = refs
    else:
        (o_ref, uext_ref, x_ref, g1_ref, sc2_ref, sh2_ref, wdw_ref, bdw_ref, lng_ref, lnb_ref,
         wout_ref, n2g_ref, wr_ref, br_ref, basein_ref, hp1_in, n2_in, meta_in,
         hp1_ref, n2_ref, meta_ref, cnt_ref, base) = refs
    i = pl.program_id(0)

    @pl.when(i == 0)
    def _():
        base[...] = basein_ref[0:1, :]

    if prompt:
        first = (i % tiles_per_seq) == 0
        halo = uhalo_ref[...]
        ext[0:HALO, :] = jnp.where(first, jnp.zeros_like(halo), halo)
        ext[HALO:HALO + TM, :] = ucur_ref[...]
        off = HALO - (CONV_K - 1)
        chunks = []
        for phase in range(SUBLANES):
            q_max = max([(off + j) // SUBLANES for j in range(CONV_K) if (off + j) % SUBLANES == phase])
            span = TM + SUBLANES * q_max
            shifted[phase, 0:span, :] = ext[pl.ds(phase, span), :]
        for c in range(TM // CONV_CHUNK):
            a = jnp.zeros((CONV_CHUNK, CONV_CH), F32) + bdw_ref[...]
            for j in range(CONV_K):
                phase, q = (off + j) % SUBLANES, (off + j) // SUBLANES
                start = c * CONV_CHUNK + SUBLANES * q
                a = a + wdw_ref[j:j + 1, :] * shifted[phase, start:start + CONV_CHUNK, :]
            chunks.append(a)
        y = jnp.concatenate(chunks, axis=0)
    else:
        nb, text, _ = uext_ref.shape
        ts = text - (CONV_K - 1)
        a = jnp.zeros((nb, ts, CONV_CH), F32) + bdw_ref[...]
        for j in range(CONV_K):
            a = a + wdw_ref[j:j + 1, :] * uext_ref[:, j:j + ts, :]
        y = a.reshape(nb * ts, CONV_CH)
    yc = _conv_ln_swish(y, lng_ref[...], lnb_ref[...])

    proj = (jnp.dot(o_ref[...].astype(BF16), wout_ref[0:N_HEADS * DV, :], preferred_element_type=F32)
            + jnp.dot(yc.astype(BF16), wout_ref[N_HEADS * DV:, :], preferred_element_type=F32))
    hp1 = x_ref[...] + g1_ref[0] * proj
    hp1_ref[...] = hp1
    n2 = hp1 * lax.rsqrt(jnp.mean(hp1 * hp1, axis=-1, keepdims=True) + EPS) * n2g_ref[...]
    n2 = n2 * (1.0 + sc2_ref[0]) + sh2_ref[0]
    _store_token_rows(n2_ref, n2)

    n_hi, n_lo = _split_bf16(n2)
    w_hi, w_lo = _split_bf16(wr_ref[...])
    logits = (jnp.dot(n_hi, w_hi, preferred_element_type=F32)
              + (jnp.dot(n_hi, w_lo, preferred_element_type=F32) + jnp.dot(n_lo, w_hi, preferred_element_type=F32))
              + br_ref[...])
    lane = lax.broadcasted_iota(I32, logits.shape, 1)
    lane_f = lane.astype(F32)
    lg = jnp.where(lane < N_EXPERTS, logits, NEG_INF)
    onehots, vals, idxs = [], [], []
    for _ in range(TOP_K):
        mx = jnp.max(lg, axis=-1, keepdims=True)
        idx = jnp.min(jnp.where(lg == mx, lane_f, float(LANES)), axis=-1, keepdims=True)
        oh = lane_f == idx
        lg = jnp.where(oh, NEG_INF, lg)
        onehots.append(oh)
        vals.append(mx)
        idxs.append(idx)
    exps = [jnp.exp(v - vals[0]) for v in vals]
    denom = exps[0] + exps[1] + exps[2] + exps[3]

    sel = jnp.zeros(logits.shape, F32)
    for oh in onehots:
        sel = sel + oh.astype(F32)
    r_i = lax.broadcasted_iota(I32, (TM, TM), 0)
    c_i = lax.broadcasted_iota(I32, (TM, TM), 1)
    ltri = (r_i > c_i).astype(BF16)
    before = jnp.dot(ltri, sel.astype(BF16), preferred_element_type=F32) + base[...]
    meta = jnp.zeros(logits.shape, F32)
    for k in range(TOP_K):
        rank = jnp.sum(jnp.where(onehots[k], before, 0.0), axis=-1, keepdims=True)
        meta = meta + jnp.where(lane == k, idxs[k], 0.0)
        meta = meta + jnp.where(lane == TOP_K + k, exps[k] / denom, 0.0)
        meta = meta + jnp.where(lane == 2 * TOP_K + k, rank, 0.0)
    meta_ref[...] = meta
    new_base = base[...] + jnp.sum(sel, axis=0, keepdims=True)
    base[...] = new_base
    cnt_ref[...] = jnp.broadcast_to(new_base, cnt_ref.shape)


def _mix_prompt(o_attn, u, x, g1, sc2, sh2, conv_w, w_out_bf16, norm2_g, w_router_pad, b_router_pad,
                base_in, n_total, tiles_per_seq):
    n = x.shape[0]
    nt = n // TM
    assert n_total == n + TM
    row = lambda i: (jnp.minimum(i, nt - 1), 0)
    out_row = lambda i: (i, 0)
    const2 = lambda i: (0, 0)
    grp = lambda i: (jnp.minimum(i, nt - 1) // tiles_per_seq, 0, 0)
    halo_row = lambda i: (jnp.maximum(jnp.minimum(i, nt - 1) * (TM // HALO) - 1, 0), 0)
    wdw, bdw, lng, lnb = conv_w
    in_specs = [
        pl.BlockSpec((TM, N_HEADS * DV), row),
        pl.BlockSpec((TM, CONV_CH), row),
        pl.BlockSpec((HALO, CONV_CH), halo_row),
        pl.BlockSpec((TM, D_MODEL), row),
        pl.BlockSpec((1, 1, D_MODEL), grp), pl.BlockSpec((1, 1, D_MODEL), grp), pl.BlockSpec((1, 1, D_MODEL), grp),
        pl.BlockSpec((CONV_K, CONV_CH), const2), pl.BlockSpec((1, CONV_CH), const2),
        pl.BlockSpec((1, CONV_CH), const2), pl.BlockSpec((1, CONV_CH), const2),
        pl.BlockSpec((D_MODEL, D_MODEL), const2),
        pl.BlockSpec((1, D_MODEL), const2),
        pl.BlockSpec((D_MODEL, LANES), const2), pl.BlockSpec((1, LANES), const2),
        pl.BlockSpec((SUBLANES, LANES), const2),
    ]
    assert len(in_specs) == N_MIX_PROMPT_INPUTS
    out_specs = [
        pl.BlockSpec((TM, D_MODEL), out_row),
        pl.BlockSpec((TM * ROW_CHUNKS, LANES), out_row),
        pl.BlockSpec((TM, LANES), out_row),
        pl.BlockSpec((SUBLANES, LANES), const2),
    ]
    out_shape = [
        jax.ShapeDtypeStruct((n_total, D_MODEL), F32),
        jax.ShapeDtypeStruct((n_total * ROW_CHUNKS, LANES), F32),
        jax.ShapeDtypeStruct((n_total, LANES), F32),
        jax.ShapeDtypeStruct((SUBLANES, LANES), F32),
    ]
    return pl.pallas_call(
        functools.partial(_mix_kernel, prompt=True, tiles_per_seq=tiles_per_seq, n_real=nt),
        grid=(nt + 1,),
        in_specs=in_specs,
        out_specs=out_specs,
        out_shape=out_shape,
        scratch_shapes=[
            pltpu.VMEM((HALO + TM, CONV_CH), F32),
            pltpu.VMEM((SUBLANES, HALO + TM, CONV_CH), F32),
            pltpu.VMEM((1, LANES), F32),
        ],
        compiler_params=_cparams(("arbitrary",)),
        name="mix_prompt",
    )(o_attn, u, u, x, g1, sc2, sh2, wdw, bdw.reshape(1, -1), lng.reshape(1, -1), lnb.reshape(1, -1),
      w_out_bf16, norm2_g.reshape(1, -1), w_router_pad, b_router_pad, base_in)


def _mix_sample(o_attn, uext, x, g1, sc2, sh2, conv_w, w_out_bf16, norm2_g, w_router_pad, b_router_pad,
                base_in, hp1_all, n2_all, meta_all, tile0):
    const2 = lambda i: (0, 0)
    const3 = lambda i: (0, 0, 0)
    out_row = lambda i: (tile0, 0)
    wdw, bdw, lng, lnb = conv_w
    nb, text, _ = uext.shape
    in_specs = [
        pl.BlockSpec((TM, N_HEADS * DV), const2),
        pl.BlockSpec((nb, text, CONV_CH), const3),
        pl.BlockSpec((TM, D_MODEL), const2),
        pl.BlockSpec((1, TM, D_MODEL), const3), pl.BlockSpec((1, TM, D_MODEL), const3),
        pl.BlockSpec((1, TM, D_MODEL), const3),
        pl.BlockSpec((CONV_K, CONV_CH), const2), pl.BlockSpec((1, CONV_CH), const2),
        pl.BlockSpec((1, CONV_CH), const2), pl.BlockSpec((1, CONV_CH), const2),
        pl.BlockSpec((D_MODEL, D_MODEL), const2),
        pl.BlockSpec((1, D_MODEL), const2),
        pl.BlockSpec((D_MODEL, LANES), const2), pl.BlockSpec((1, LANES), const2),
        pl.BlockSpec((SUBLANES, LANES), const2),
        pl.BlockSpec(memory_space=pl.ANY), pl.BlockSpec(memory_space=pl.ANY), pl.BlockSpec(memory_space=pl.ANY),
    ]
    out_specs = [
        pl.BlockSpec((TM, D_MODEL), out_row),
        pl.BlockSpec((TM * ROW_CHUNKS, LANES), out_row),
        pl.BlockSpec((TM, LANES), out_row),
        pl.BlockSpec((SUBLANES, LANES), const2),
    ]
    out_shape = [
        jax.ShapeDtypeStruct(hp1_all.shape, F32),
        jax.ShapeDtypeStruct(n2_all.shape, F32),
        jax.ShapeDtypeStruct(meta_all.shape, F32),
        jax.ShapeDtypeStruct((SUBLANES, LANES), F32),
    ]
    return pl.pallas_call(
        functools.partial(_mix_kernel, prompt=False, tiles_per_seq=1, n_real=1),
        grid=(1,),
        in_specs=in_specs,
        out_specs=out_specs,
        out_shape=out_shape,
        scratch_shapes=[pltpu.VMEM((1, LANES), F32)],
        input_output_aliases={15: 0, 16: 1, 17: 2},
        compiler_params=_cparams(("arbitrary",)),
        name="mix_sample",
    )(o_attn, uext, x, g1, sc2, sh2, wdw, bdw.reshape(1, -1), lng.reshape(1, -1), lnb.reshape(1, -1),
      w_out_bf16, norm2_g.reshape(1, -1), w_router_pad, b_router_pad, base_in, hp1_all, n2_all, meta_all)


MAX_SLOT_TILES = 16


def _lane_cumsum(x, lane):
    s = 1
    while s < LANES:
        x = x + jnp.where(lane >= s, pltpu.roll(x, s, 1), 0)
        s *= 2
    return x


def _slots_kernel(meta_ref, cnt_ref, slots_ref, tmap_ref, einfo_ref, *, n_tiles_pad):
    shift = TM.bit_length() - 1
    lane8 = lax.broadcasted_iota(I32, (SUBLANES, LANES), 1)
    cnt = cnt_ref[...].astype(I32)
    padded = ((cnt + (TM - 1)) >> shift) << shift
    csum = _lane_cumsum(padded, lane8)
    gstart = csum - padded

    meta = meta_ref[...]
    lane = lax.broadcasted_iota(I32, meta.shape, 1)
    lane_f = lane.astype(F32)
    gstart_f = gstart[0:1, :].astype(F32)
    out = jnp.zeros(meta.shape, F32)
    for k in range(TOP_K):
        sel = lane_f == meta[:, k:k + 1]
        gs = jnp.sum(jnp.where(sel, gstart_f, 0.0), axis=-1, keepdims=True)
        out = out + jnp.where(lane == k, gs + meta[:, 2 * TOP_K + k: 2 * TOP_K + k + 1], 0.0)
    slots_ref[...] = out.astype(I32)

    @pl.when(pl.program_id(0) == 0)
    def _():
        ctiles = csum[0:1, :] >> shift
        n_valid = jnp.max(ctiles, axis=-1, keepdims=True)
        t = lax.broadcasted_iota(I32, (n_tiles_pad, LANES), 0)
        t = jnp.minimum(t, n_valid - 1)
        lane_t = lax.broadcasted_iota(I32, (n_tiles_pad, LANES), 1)
        hit = jnp.where((lane_t < N_EXPERTS) & (ctiles <= t), 1, 0)
        te = jnp.sum(hit, axis=-1, keepdims=True)
        tmap_ref[...] = jnp.broadcast_to(jnp.minimum(te, N_EXPERTS - 1), tmap_ref.shape)
        row8 = lax.broadcasted_iota(I32, (SUBLANES, LANES), 0)
        info = jnp.where(row8 == 0, gstart, 0)
        info = info + jnp.where(row8 == 1, padded, 0)
        info = info + jnp.where(row8 == 2, cnt, 0)
        info = info + jnp.where(row8 == 3, jnp.broadcast_to(n_valid, (SUBLANES, LANES)), 0)
        einfo_ref[...] = info


def _routing_slots(meta_all, counts, n_tiles_pad):
    n = meta_all.shape[0]
    const2 = lambda i: (0, 0)
    n_tm = n // TM
    group = max(g for g in range(1, MAX_SLOT_TILES + 1) if n_tm % g == 0)
    rows = group * TM
    return pl.pallas_call(
        functools.partial(_slots_kernel, n_tiles_pad=n_tiles_pad),
        grid=(n // rows,),
        in_specs=[pl.BlockSpec((rows, LANES), lambda i: (i, 0)), pl.BlockSpec((SUBLANES, LANES), const2)],
        out_specs=[
            pl.BlockSpec((rows, LANES), lambda i: (i, 0)),
            pl.BlockSpec((n_tiles_pad, LANES), const2),
            pl.BlockSpec((SUBLANES, LANES), const2),
        ],
        out_shape=[
            jax.ShapeDtypeStruct((n, LANES), I32),
            jax.ShapeDtypeStruct((n_tiles_pad, LANES), I32),
            jax.ShapeDtypeStruct((SUBLANES, LANES), I32),
        ],
        compiler_params=_cparams(("arbitrary",)),
        name="routing_slots",
    )(meta_all, counts)


def _token_rows(ref, row):
    return ref.at[pl.ds(pl.multiple_of(row * ROW_CHUNKS, ROW_CHUNKS), ROW_CHUNKS)]


def _dispatch_kernel(gstart_ref, padded_ref, cnt_ref, nv_ref, slots_hbm, src_ref, xs_hbm,
                     idx0, idx1, zbuf, sem, zsem, isems):
    i = pl.program_id(0)
    tile_rows = TM * ROW_CHUNKS
    n_slot_tiles = xs_hbm.shape[0] // tile_rows

    def zero_tile_copy(tile):
        start = pl.multiple_of(tile * tile_rows, tile_rows)
        return pltpu.make_async_copy(zbuf, xs_hbm.at[pl.ds(start, tile_rows)], zsem)

    def pad_tile_copy(e):
        return zero_tile_copy((gstart_ref[e] + padded_ref[e]) // TM - 1)

    @pl.when(i == 0)
    def _():
        zbuf[...] = jnp.zeros(zbuf.shape, zbuf.dtype)
        for e in range(N_EXPERTS):
            @pl.when(cnt_ref[e] > 0)
            def _():
                pad_tile_copy(e).start()

        def start_unused(t, carry):
            zero_tile_copy(t).start()
            return carry

        def wait_unused(t, carry):
            zero_tile_copy(t).wait()
            return carry

        lax.fori_loop(nv_ref[0], n_slot_tiles, start_unused, 0)
        for e in range(N_EXPERTS):
            @pl.when(cnt_ref[e] > 0)
            def _():
                pad_tile_copy(e).wait()
        lax.fori_loop(nv_ref[0], n_slot_tiles, wait_unused, 0)

    n_steps = pl.num_programs(0)
    idxs = (idx0, idx1)

    def idx_copy(step, parity):
        return pltpu.make_async_copy(slots_hbm.at[step], idxs[parity], isems.at[parity])

    def for_each_row(parity, fn):
        def body(r, carry):
            for k in range(TOP_K):
                fn(pltpu.make_async_copy(_token_rows(src_ref, r),
                                         _token_rows(xs_hbm, idxs[parity][0, r * TOP_K + k]), sem), k)
            return carry
        lax.fori_loop(0, TM, body, 0, unroll=8)

    @pl.when(i == 0)
    def _():
        idx_copy(0, 0).start()

    def step(parity):
        @pl.when(i + 1 < n_steps)
        def _():
            idx_copy(i + 1, 1 - parity).start()

        idx_copy(i, parity).wait()
        for_each_row(parity, lambda c, k: c.start(priority=k % N_DMA_PRIORITIES))
        for_each_row(parity, lambda c, k: c.wait())

    for parity in range(2):
        @pl.when(i % 2 == parity)
        def _():
            step(parity)


def _dispatch(gstart, padded, cnt, n_valid, slots2d, n2_all, s_max):
    nt = slots2d.shape[0]
    grid_spec = pltpu.PrefetchScalarGridSpec(
        num_scalar_prefetch=4,
        grid=(nt,),
        in_specs=[
            pl.BlockSpec(memory_space=pl.ANY),
            pl.BlockSpec((TM * ROW_CHUNKS, LANES), lambda i, *_: (i, 0)),
        ],
        out_specs=pl.BlockSpec(memory_space=pl.ANY),
        scratch_shapes=[
            pltpu.SMEM((1, TM * TOP_K), I32),
            pltpu.SMEM((1, TM * TOP_K), I32),
            pltpu.VMEM((TM * ROW_CHUNKS, LANES), F32),
            pltpu.SemaphoreType.DMA,
            pltpu.SemaphoreType.DMA,
            pltpu.SemaphoreType.DMA((2,)),
        ],
    )
    return pl.pallas_call(
        _dispatch_kernel,
        grid_spec=grid_spec,
        out_shape=jax.ShapeDtypeStruct((s_max * ROW_CHUNKS, LANES), F32),
        compiler_params=pltpu.CompilerParams(dimension_semantics=("arbitrary",), has_side_effects=True),
        name="moe_dispatch",
    )(gstart, padded, cnt, n_valid, slots2d, n2_all)


DEINT_BLOCK = 2 * LANES
EXPERT_VMEM_LIMIT = 56 * 1024 * 1024


def _expert_kernel(te_ref, nv_ref, x_ref, wgu_hbm, wd_hbm, b1g_ref, b1u_ref, b2_ref, y_ref,
                   wgu_buf, wd_buf, w1g, w1u, w2, wsem, slot_ref):
    i = pl.program_id(0)
    n_valid = nv_ref[0]
    valid = i < n_valid
    expert = te_ref[i]
    new_expert = jnp.logical_or(i == 0, expert != te_ref[jnp.maximum(i - 1, 0)])
    last_tile = te_ref.shape[0] - 1

    def weight_copies(e, s):
        return (pltpu.make_async_copy(wgu_hbm.at[e], wgu_buf.at[s], wsem.at[0, s]),
                pltpu.make_async_copy(wd_hbm.at[e], wd_buf.at[s], wsem.at[1, s]))

    @pl.when(i == 0)
    def _():
        slot_ref[0] = 0
        for c in weight_copies(expert, 0):
            c.start()

    @pl.when(jnp.logical_and(valid, new_expert))
    def _():
        s = slot_ref[0]
        nxt = lax.while_loop(
            lambda j: jnp.logical_and(j < n_valid, te_ref[jnp.minimum(j, last_tile)] == expert),
            lambda j: j + 1, i + 1)

        @pl.when(nxt < n_valid)
        def _():
            for c in weight_copies(te_ref[jnp.minimum(nxt, last_tile)], 1 - s):
                c.start()

        for c in weight_copies(expert, s):
            c.wait()
        r = lax.broadcasted_iota(I32, (DEINT_BLOCK, DEINT_BLOCK), 0)
        c = lax.broadcasted_iota(I32, (DEINT_BLOCK, DEINT_BLOCK), 1)
        src_col = jnp.where(c < LANES, 2 * c, 2 * (c - LANES) + 1)
        perm = jnp.where(r == src_col, 1.0, 0.0).astype(BF16)
        for blk in range(wgu_buf.shape[2] // DEINT_BLOCK):
            cols = wgu_buf[s, :, blk * DEINT_BLOCK:(blk + 1) * DEINT_BLOCK].astype(BF16)
            split = jnp.dot(cols, perm, preferred_element_type=F32).astype(BF16)
            w1g[:, blk * LANES:(blk + 1) * LANES] = split[:, :LANES]
            w1u[:, blk * LANES:(blk + 1) * LANES] = split[:, LANES:]
        w2[...] = wd_buf[s].astype(BF16)
        slot_ref[0] = 1 - s

    @pl.when(valid)
    def _():
        x = _load_token_rows(x_ref, TM).astype(BF16)
        hg = jnp.dot(x, w1g[...], preferred_element_type=F32) + b1g_ref[0]
        hu = jnp.dot(x, w1u[...], preferred_element_type=F32) + b1u_ref[0]
        gate = jnp.minimum(hg, SWIGLU_LIMIT)
        up = jnp.clip(hu, -SWIGLU_LIMIT, SWIGLU_LIMIT)
        act = (up + 1.0) * gate * jax.nn.sigmoid(SWIGLU_ALPHA * gate)
        y = jnp.dot(act.astype(BF16), w2[...], preferred_element_type=F32) + b2_ref[0]
        _store_token_rows(y_ref, y)

    @pl.when(jnp.logical_not(valid))
    def _():
        y_ref[...] = jnp.zeros(y_ref.shape, y_ref.dtype)


def _expert_mlp(tile_expert, n_valid, xs, w_gate_up, w_down, b1g, b1u, b2):
    tile_rows = TM * ROW_CHUNKS
    nt = xs.shape[0] // tile_rows
    ff = w_down.shape[1]
    tile = lambda i, te, nv: (jnp.minimum(i, nv[0] - 1), 0)
    wsel = lambda i, te, nv: (te[i], 0, 0)
    grid_spec = pltpu.PrefetchScalarGridSpec(
        num_scalar_prefetch=2,
        grid=(nt,),
        in_specs=[
            pl.BlockSpec((tile_rows, LANES), tile),
            pl.BlockSpec(memory_space=pl.ANY),
            pl.BlockSpec(memory_space=pl.ANY),
            pl.BlockSpec((1, 1, ff), wsel),
            pl.BlockSpec((1, 1, ff), wsel),
            pl.BlockSpec((1, 1, D_MODEL), wsel),
        ],
        out_specs=pl.BlockSpec((tile_rows, LANES), lambda i, te, nv: (i, 0)),
        scratch_shapes=[
            pltpu.VMEM((2, D_MODEL, 2 * ff), F32),
            pltpu.VMEM((2, ff, D_MODEL), F32),
            pltpu.VMEM((D_MODEL, ff), BF16),
            pltpu.VMEM((D_MODEL, ff), BF16),
            pltpu.VMEM((ff, D_MODEL), BF16),
            pltpu.SemaphoreType.DMA((2, 2)),
            pltpu.SMEM((1,), I32),
        ],
    )
    return pl.pallas_call(
        _expert_kernel,
        grid_spec=grid_spec,
        out_shape=jax.ShapeDtypeStruct(xs.shape, F32),
        compiler_params=pltpu.CompilerParams(dimension_semantics=("arbitrary",),
                                             vmem_limit_bytes=EXPERT_VMEM_LIMIT),
        name="expert_mlp",
    )(tile_expert, n_valid, xs, w_gate_up, w_down, b1g, b1u, b2)


def _combine_kernel(slots_hbm, ys_hbm, hp1_ref, meta_ref, g2_ref, nf_ref, o_ref,
                    idx0, idx1, buf0, buf1, sems, isems, *, tile0):
    i = pl.program_id(0)
    n_steps = pl.num_programs(0)
    bufs = (buf0, buf1)
    idxs = (idx0, idx1)

    def idx_copy(step, parity):
        return pltpu.make_async_copy(slots_hbm.at[tile0 + step], idxs[parity], isems.at[parity])

    def for_each_row(parity, fn):
        def body(r, carry):
            for k in range(TOP_K):
                fn(pltpu.make_async_copy(_token_rows(ys_hbm, idxs[parity][0, r * TOP_K + k]),
                                         _token_rows(bufs[parity].at[k], r), sems.at[parity]), k)
            return carry
        lax.fori_loop(0, TM, body, 0, unroll=8)

    def start_rows(parity):
        for_each_row(parity, lambda c, k: c.start(priority=k % N_DMA_PRIORITIES))

    @pl.when(i == 0)
    def _():
        first = idx_copy(0, 0)
        first.start()
        first.wait()
        start_rows(0)

        @pl.when(n_steps > 1)
        def _():
            idx_copy(1, 1).start()

    def step(parity):
        @pl.when(i + 1 < n_steps)
        def _():
            idx_copy(i + 1, 1 - parity).wait()
            start_rows(1 - parity)

        for_each_row(parity, lambda c, k: c.wait())

        @pl.when(i + 2 < n_steps)
        def _():
            idx_copy(i + 2, parity).start()

        meta = meta_ref[...]
        f = jnp.zeros((TM, D_MODEL), F32)
        for k in range(TOP_K):
            f = f + meta[:, TOP_K + k: TOP_K + k + 1] * _load_token_rows(bufs[parity], TM, (k,))
        hp2 = hp1_ref[...] + g2_ref[0] * f
        o_ref[...] = hp2 * lax.rsqrt(jnp.mean(hp2 * hp2, axis=-1, keepdims=True) + EPS) * nf_ref[...]

    for parity in range(2):
        @pl.when(i % 2 == parity)
        def _():
            step(parity)


def _combine(slots2d, ys, hp1_all, meta_all, g2, normf_g, tile0, n_rows, tiles_per_group):
    nt = n_rows // TM
    r = g2.shape[1]
    row_in = lambda i: (tile0 + i, 0)
    return pl.pallas_call(
        functools.partial(_combine_kernel, tile0=tile0),
        grid=(nt,),
        in_specs=[
            pl.BlockSpec(memory_space=pl.ANY),
            pl.BlockSpec(memory_space=pl.ANY),
            pl.BlockSpec((TM, D_MODEL), row_in),
            pl.BlockSpec((TM, LANES), row_in),
            pl.BlockSpec((1, r, D_MODEL), lambda i: (i // tiles_per_group, 0, 0)),
            pl.BlockSpec((1, D_MODEL), lambda i: (0, 0)),
        ],
        out_specs=pl.BlockSpec((TM, D_MODEL), lambda i: (i, 0)),
        out_shape=jax.ShapeDtypeStruct((n_rows, D_MODEL), F32),
        scratch_shapes=[
            pltpu.SMEM((1, TM * TOP_K), I32),
            pltpu.SMEM((1, TM * TOP_K), I32),
            pltpu.VMEM((TOP_K, TM * ROW_CHUNKS, LANES), F32),
            pltpu.VMEM((TOP_K, TM * ROW_CHUNKS, LANES), F32),
            pltpu.SemaphoreType.DMA((2,)),
            pltpu.SemaphoreType.DMA((2,)),
        ],
        compiler_params=_cparams(("arbitrary",)),
        name="moe_combine",
    )(slots2d, ys, hp1_all, meta_all, g2, normf_g.reshape(1, D_MODEL))


def kernel(x_prompt, x_sample, cache_k, cache_v, state_conv, page_table, c_prompt, c_sample, norm1_g, norm2_g, w_ada, b_ada, w_in, lambda_q1, lambda_k1, lambda_q2, lambda_k2, subln_g, w_dw, b_dw, conv_ln_g, conv_ln_b, w_out, w_router, b_router, w_gate_up, b_gate_up, w_down, b_down, normf_g):
    depth = norm1_g.shape[0]
    assert depth == 1, "single-layer step"
    bsz, seq, d = x_prompt.shape
    db, ts, _ = x_sample.shape
    n_p, n_s = bsz * seq, db * ts
    assert d == D_MODEL and n_s == TM and seq % TQ == 0 and n_p % TM == 0
    n_all = n_p + n_s
    n_pages = page_table.shape[1]
    past = n_pages * cache_k.shape[2]
    tiles_per_seq = seq // TM
    l = 0

    n_cond = bsz + db
    c_all = jnp.concatenate([c_prompt, c_sample], axis=0)
    c_all = jnp.pad(c_all, ((0, -n_cond % SUBLANES), (0, 0)))
    mod = _adaln(c_all, w_ada[l], b_ada[l])[:n_cond]
    mod_p = mod[:bsz].reshape(bsz, 1, 6, D_MODEL)
    mod_s = jnp.repeat(mod[bsz:], ts, axis=0).reshape(1, n_s, 6, D_MODEL)
    sh1p, sc1p, g1p, sh2p, sc2p, g2p = [mod_p[:, :, j] for j in range(6)]
    sh1s, sc1s, g1s, sh2s, sc2s, g2s = [mod_s[:, :, j] for j in range(6)]

    lams = [v[l].reshape(1, DQK) for v in (lambda_q1, lambda_k1, lambda_q2, lambda_k2)]
    w_in_b = w_in[l].astype(BF16)
    w_out_b = w_out[l].astype(BF16)
    conv_w = (w_dw[l], b_dw[l], conv_ln_g[l], conv_ln_b[l])
    w_router_pad = jnp.pad(w_router[l], ((0, 0), (0, LANES - N_EXPERTS)))
    b_router_pad = jnp.pad(b_router[l], (0, LANES - N_EXPERTS)).reshape(1, LANES)

    xp = x_prompt.reshape(n_p, D_MODEL)
    xs_tok = x_sample.reshape(n_s, D_MODEL)
    tabs_p = _rope_tables(jnp.arange(seq))
    tabs_s = _rope_tables(jnp.tile(past + jnp.arange(ts), db))
    q_hm, k_p, v_p, k_hm, v_hm, u_p = _inproj(xp, norm1_g[l], sc1p, sh1p, tabs_p, w_in_b,
                                              tiles_per_seq, tiles_per_seq, True)
    q_s, k_s, v_s, u_s = _inproj(xs_tok, norm1_g[l], sc1s, sh1s, tabs_s, w_in_b, 1, 1, False)

    o_p = _prompt_attention(q_hm, k_hm, v_hm, lams, subln_g[l], bsz, seq)
    o_s = _sample_attention(q_s, k_s, v_s, cache_k, cache_v, page_table, lams, subln_g[l])

    uext_s = jnp.concatenate([state_conv[l], u_s.reshape(db, ts, CONV_CH)], axis=1)

    zeros_base = jnp.zeros((SUBLANES, LANES), F32)
    hp1_all, n2_all, meta_all, cnt_p = _mix_prompt(
        o_p, u_p, xp, g1p, sc2p, sh2p, conv_w, w_out_b, norm2_g[l], w_router_pad, b_router_pad,
        zeros_base, n_all, tiles_per_seq)
    hp1_all, n2_all, meta_all, counts = _mix_sample(
        o_s, uext_s, xs_tok, g1s, sc2s, sh2s, conv_w, w_out_b, norm2_g[l], w_router_pad, b_router_pad,
        cnt_p, hp1_all, n2_all, meta_all, n_p // TM)

    n_tok_tiles = n_all // TM
    s_max = n_all * TOP_K + N_EXPERTS * TM
    n_slot_tiles = s_max // TM
    n_tiles_pad = -(-n_slot_tiles // SUBLANES) * SUBLANES
    slots, tmap, einfo = _routing_slots(meta_all, counts, n_tiles_pad)
    slots2d = slots[:, :TOP_K].reshape(n_tok_tiles, 1, TM * TOP_K)
    tile_expert = tmap[:n_slot_tiles, 0]
    gstart, padded, cnt = einfo[0, :N_EXPERTS], einfo[1, :N_EXPERTS], einfo[2, :N_EXPERTS]
    n_valid = einfo[3, :1]

    x_sorted = _dispatch(gstart, padded, cnt, n_valid, slots2d, n2_all, s_max)

    bgu = b_gate_up[l]
    b1g = bgu[:, 0::2].reshape(N_EXPERTS, 1, -1)
    b1u = bgu[:, 1::2].reshape(N_EXPERTS, 1, -1)
    b2 = b_down[l].reshape(N_EXPERTS, 1, D_MODEL)
    y_sorted = _expert_mlp(tile_expert, n_valid, x_sorted, w_gate_up.reshape(w_gate_up.shape[1:]),
                           w_down.reshape(w_down.shape[1:]), b1g, b1u, b2)

    y_p = _combine(slots2d, y_sorted, hp1_all, meta_all, g2p, normf_g, 0, n_p, tiles_per_seq)
    y_s = _combine(slots2d, y_sorted, hp1_all, meta_all, g2s, normf_g, n_p // TM, n_s, 1)

    y_prompt = y_p.reshape(bsz, seq, D_MODEL)
    y_sample = y_s.reshape(db, ts, D_MODEL)
    k_prompt = jnp.transpose(k_p, (0, 3, 1, 2))[None]
    v_prompt = v_p.reshape(1, bsz, seq, N_HEADS, DV)
    conv_prompt = u_p.reshape(bsz, seq, CONV_CH)[:, seq - (CONV_K - 1):][None]
    k_sample = k_s.reshape(1, db, ts, 2 * N_HEADS, DQK)
    v_sample = v_s.reshape(1, db, ts, N_HEADS, DV)
    conv_sample = uext_s[:, ts:][None]
    return (y_prompt, y_sample, k_prompt, v_prompt, conv_prompt, k_sample, v_sample, conv_sample)
```

```python
import functools
import math

import jax
import jax.numpy as jnp
from jax import lax
from jax.experimental import pallas as pl
from jax.experimental.pallas import tpu as pltpu

F32 = jnp.float32
BF16 = jnp.bfloat16
I32 = jnp.int32
HIGHEST = lax.Precision.HIGHEST

D_MODEL = 1024
N_HEADS = 4
DV = 128
DQK = 64
ROT_DIM = 16
ROPE_THETA = 500000.0
CONV_K = 31
CONV_CH = 512
QK_W = 512
V_W = 512
IN_WIDTH = 2 * QK_W + V_W + 2 * CONV_CH
N_EXPERTS = 32
TOP_K = 4
SWIGLU_LIMIT = 7.0
SWIGLU_ALPHA = 1.702
EPS = 1e-5
LAM_INIT = 0.8 - 0.6 * math.exp(-0.3 * 0)
PAGE_SIZE = 128
Q_SCALE = DQK ** -0.5 * math.log2(math.e)

LANES = 128
SUBLANES = 8
VMEM_LIMIT = 48 * 1024 * 1024
N_DMA_PRIORITIES = 2

TM = 256
TQ = 512
TK = 512
PAGES_PER_STEP = 32
HALO = 32
CONV_CHUNK = 64
NEG_INF = float("-inf")


def _cparams(sem):
    return pltpu.CompilerParams(dimension_semantics=sem, vmem_limit_bytes=VMEM_LIMIT)


def _adaln_kernel(c_ref, w_ref, b_ref, o_ref):
    c = c_ref[...]
    s = c * jax.nn.sigmoid(c)
    o_ref[...] = jnp.dot(s, w_ref[...], precision=HIGHEST, preferred_element_type=F32) + b_ref[...]


def _adaln(c_all, w, b):
    n, d = c_all.shape
    width = w.shape[1]
    bn = 1536
    return pl.pallas_call(
        _adaln_kernel,
        grid=(width // bn,),
        in_specs=[
            pl.BlockSpec((n, d), lambda j: (0, 0)),
            pl.BlockSpec((d, bn), lambda j: (0, j)),
            pl.BlockSpec((1, bn), lambda j: (0, j)),
        ],
        out_specs=pl.BlockSpec((n, bn), lambda j: (0, j)),
        out_shape=jax.ShapeDtypeStruct((n, width), F32),
        compiler_params=_cparams(("parallel",)),
        name="adaln",
    )(c_all, w, b.reshape(1, width))


def _rope_tables(pos):
    inv = ROPE_THETA ** (-jnp.arange(0, ROT_DIM, 2, dtype=F32) / ROT_DIM)
    ang = pos.astype(F32)[:, None] * inv
    cos, sin = jnp.cos(ang), jnp.sin(ang)
    half = ROT_DIM // 2
    ones = jnp.ones((pos.shape[0], DQK - ROT_DIM), F32)
    zeros_h = jnp.zeros((pos.shape[0], half), F32)
    zeros_r = jnp.zeros((pos.shape[0], DQK - ROT_DIM), F32)
    c64 = jnp.concatenate([cos, cos, ones], axis=1)
    a64 = jnp.concatenate([-sin, zeros_h, zeros_r], axis=1)
    b64 = jnp.concatenate([zeros_h, sin, zeros_r], axis=1)
    rep = LANES // DQK
    return jnp.tile(c64, (1, rep)), jnp.tile(a64, (1, rep)), jnp.tile(b64, (1, rep))


def _inproj_kernel(x_ref, g_ref, sc_ref, sh_ref, cos_ref, sa_ref, sb_ref, w_ref, *out_refs, head_major):
    x = x_ref[...]
    hn = x * lax.rsqrt(jnp.mean(x * x, axis=-1, keepdims=True) + EPS) * g_ref[...]
    hn = hn * (1.0 + sc_ref[0]) + sh_ref[0]
    proj = jnp.dot(hn.astype(BF16), w_ref[...], preferred_element_type=F32)
    cos, sa, sb = cos_ref[...], sa_ref[...], sb_ref[...]

    def rope(blk):
        return blk * cos + pltpu.roll(blk, LANES - ROT_DIM // 2, 1) * sa + pltpu.roll(blk, ROT_DIM // 2, 1) * sb

    a = proj[:, 2 * QK_W + V_W: 2 * QK_W + V_W + CONV_CH]
    gl = proj[:, 2 * QK_W + V_W + CONV_CH:]
    if head_major:
        qb_ref, k32_ref, v32_ref, kb_ref, vb_ref, u_ref = out_refs
    else:
        q32_ref, k32_ref, v32_ref, u_ref = out_refs
    u_ref[...] = a * jax.nn.sigmoid(gl)
    for h in range(N_HEADS):
        lo, hi = h * LANES, (h + 1) * LANES
        qh = rope(proj[:, lo:hi]) * Q_SCALE
        kh = rope(proj[:, QK_W + lo: QK_W + hi])
        vh = proj[:, 2 * QK_W + lo: 2 * QK_W + hi]
        v32_ref[pl.ds(h, x.shape[0], stride=N_HEADS), :] = vh
        if head_major:
            k32_ref[0, 2 * h:2 * h + 2] = kh.T.reshape(2, DQK, x.shape[0])
            qb_ref[h] = qh.astype(BF16)
            kb_ref[h] = kh.astype(BF16)
            vb_ref[h] = vh.astype(BF16)
        else:
            k32_ref[:, lo:hi] = kh
            q32_ref[:, lo:hi] = qh


def _inproj(x, norm_g, sc, sh, tabs, w_in_bf16, tiles_per_group, tiles_per_seq, head_major):
    n = x.shape[0]
    nt = n // TM
    r = sc.shape[1]
    row = lambda i: (i, 0)
    grp = lambda i: (i // tiles_per_group, 0, 0)
    tab = lambda i: (i % tiles_per_seq, 0)
    in_specs = [
        pl.BlockSpec((TM, D_MODEL), row),
        pl.BlockSpec((1, D_MODEL), lambda i: (0, 0)),
        pl.BlockSpec((1, r, D_MODEL), grp),
        pl.BlockSpec((1, r, D_MODEL), grp),
        pl.BlockSpec((TM, LANES), tab),
        pl.BlockSpec((TM, LANES), tab),
        pl.BlockSpec((TM, LANES), tab),
        pl.BlockSpec((D_MODEL, IN_WIDTH), lambda i: (0, 0)),
    ]
    wide = pl.BlockSpec((TM, QK_W), row)
    hm = pl.BlockSpec((N_HEADS, TM, LANES), lambda i: (0, i, 0))
    v_rows = pl.BlockSpec((TM * N_HEADS, DV), row)
    v_shape = jax.ShapeDtypeStruct((n * N_HEADS, DV), F32)
    wide_shape = jax.ShapeDtypeStruct((n, QK_W), F32)
    if head_major:
        n_seq = nt // tiles_per_seq
        k_t = pl.BlockSpec((1, 2 * N_HEADS, DQK, TM), lambda i: (i // tiles_per_seq, 0, 0, i % tiles_per_seq))
        out_specs = [hm, k_t, v_rows, hm, hm, wide]
        out_shape = [
            jax.ShapeDtypeStruct((N_HEADS, n, LANES), BF16),
            jax.ShapeDtypeStruct((n_seq, 2 * N_HEADS, DQK, tiles_per_seq * TM), F32),
            v_shape,
            jax.ShapeDtypeStruct((N_HEADS, n, LANES), BF16),
            jax.ShapeDtypeStruct((N_HEADS, n, LANES), BF16),
            wide_shape,
        ]
    else:
        out_specs = [wide, wide, v_rows, wide]
        out_shape = [wide_shape, wide_shape, v_shape, wide_shape]
    return pl.pallas_call(
        functools.partial(_inproj_kernel, head_major=head_major),
        grid=(nt,),
        in_specs=in_specs,
        out_specs=out_specs,
        out_shape=out_shape,
        compiler_params=_cparams(("parallel",)),
        name="inproj_hm" if head_major else "inproj",
    )(x, norm_g.reshape(1, D_MODEL), sc, sh, *tabs, w_in_bf16)


def _lambda_value(lq1, lk1, lq2, lk2):
    a = jnp.exp(jnp.sum(lq1[...] * lk1[...], axis=-1, keepdims=True))
    b = jnp.exp(jnp.sum(lq2[...] * lk2[...], axis=-1, keepdims=True))
    return a - b + LAM_INIT


def _diff_merge(o1, l1, o2, l2, lam, subln_g):
    o = o1 / l1 - lam * (o2 / l2)
    o = o * lax.rsqrt(jnp.mean(o * o, axis=-1, keepdims=True) + EPS) * subln_g
    return o * (1.0 - LAM_INIT)


def _attn_kernel(qt_ref, kt_ref, q_ref, k_ref, v_ref, lq1, lk1, lq2, lk2, sg_ref, o_ref, qs, m_s, l_s, acc):
    s_idx = pl.program_id(1)
    qi = qt_ref[s_idx]
    ki = kt_ref[s_idx]

    @pl.when(ki == 0)
    def _():
        for h in range(N_HEADS):
            q = q_ref[h]
            lane = lax.broadcasted_iota(I32, q.shape, 1)
            zero = jnp.zeros_like(q)
            qs[h, 0:TQ, :] = jnp.where(lane < DQK, q, zero)
            qs[h, TQ:2 * TQ, :] = jnp.where(lane >= DQK, q, zero)
        m_s[...] = jnp.full(m_s.shape, NEG_INF, F32)
        l_s[...] = jnp.zeros(l_s.shape, F32)
        acc[...] = jnp.zeros(acc.shape, F32)

    def update(h, masked):
        s = lax.dot_general(qs[h], k_ref[h], (((1,), (1,)), ((), ())), preferred_element_type=F32)
        if masked:
            row = lax.broadcasted_iota(I32, s.shape, 0) & (TQ - 1)
            col = lax.broadcasted_iota(I32, s.shape, 1)
            s = jnp.where(row >= col, s, NEG_INF)
        m_prev = m_s[h]
        m_next = jnp.maximum(m_prev, jnp.max(s, axis=1, keepdims=True))
        p = jnp.exp2(s - jnp.tile(m_next, (1, TK // LANES)))
        alpha = jnp.exp2(m_prev - m_next)
        v_ones = jnp.concatenate([v_ref[h], jnp.ones((TK, LANES), BF16)], axis=1)
        pv = jnp.dot(p.astype(BF16), v_ones, preferred_element_type=F32)
        l_s[h] = alpha * l_s[h] + pv[:, DV:]
        acc[h] = alpha * acc[h] + pv[:, :DV]
        m_s[h] = m_next

    @pl.when(ki < qi)
    def _():
        for h in range(N_HEADS):
            update(h, False)

    @pl.when(ki == qi)
    def _():
        lam = _lambda_value(lq1, lk1, lq2, lk2)
        for h in range(N_HEADS):
            update(h, True)
            o = _diff_merge(acc[h, 0:TQ, :], l_s[h, 0:TQ, :], acc[h, TQ:2 * TQ, :], l_s[h, TQ:2 * TQ, :],
                            lam, sg_ref[...])
            o_ref[:, h * DV:(h + 1) * DV] = o.astype(o_ref.dtype)


def _prompt_attention(q_hm, k_hm, v_hm, lams, subln_g, batch, seq):
    nq = seq // TQ
    pairs = [(qi, ki) for qi in range(nq) for ki in range(qi + 1)]
    qt = jnp.asarray([p[0] for p in pairs], I32)
    kt = jnp.asarray([p[1] for p in pairs], I32)
    n = batch * seq
    vec = lambda b, s, qt, kt: (0, 0)
    grid_spec = pltpu.PrefetchScalarGridSpec(
        num_scalar_prefetch=2,
        grid=(batch, len(pairs)),
        in_specs=[
            pl.BlockSpec((N_HEADS, TQ, LANES), lambda b, s, qt, kt: (0, b * nq + qt[s], 0)),
            pl.BlockSpec((N_HEADS, TK, LANES), lambda b, s, qt, kt: (0, b * nq + kt[s], 0)),
            pl.BlockSpec((N_HEADS, TK, LANES), lambda b, s, qt, kt: (0, b * nq + kt[s], 0)),
            pl.BlockSpec((1, DQK), vec), pl.BlockSpec((1, DQK), vec),
            pl.BlockSpec((1, DQK), vec), pl.BlockSpec((1, DQK), vec),
            pl.BlockSpec((1, DV), vec),
        ],
        out_specs=pl.BlockSpec((TQ, N_HEADS * DV), lambda b, s, qt, kt: (b * nq + qt[s], 0)),
        scratch_shapes=[
            pltpu.VMEM((N_HEADS, 2 * TQ, LANES), BF16),
            pltpu.VMEM((N_HEADS, 2 * TQ, LANES), F32),
            pltpu.VMEM((N_HEADS, 2 * TQ, LANES), F32),
            pltpu.VMEM((N_HEADS, 2 * TQ, LANES), F32),
        ],
    )
    return pl.pallas_call(
        _attn_kernel,
        grid_spec=grid_spec,
        out_shape=jax.ShapeDtypeStruct((n, N_HEADS * DV), BF16),
        compiler_params=_cparams(("parallel", "arbitrary")),
        name="prompt_attn",
    )(qt, kt, q_hm, k_hm, v_hm, *lams, subln_g.reshape(1, DV))


def _paged_attn_kernel(pt_ref, q_ref, kn_ref, vn_ref, lq1, lk1, lq2, lk2, sg_ref, *rest, n_steps, ts):
    kp = rest[:PAGES_PER_STEP]
    vp = rest[PAGES_PER_STEP:2 * PAGES_PER_STEP]
    o_ref, qe, m_s, l_s, acc = rest[2 * PAGES_PER_STEP:]
    n_maps = 2 * N_HEADS
    rows = n_maps * ts
    p_idx = pl.program_id(1)

    @pl.when(p_idx == 0)
    def _():
        qt = jnp.concatenate([q_ref[...]] * n_maps, axis=0)
        row = lax.broadcasted_iota(I32, qt.shape, 0)
        col = lax.broadcasted_iota(I32, qt.shape, 1)
        same_map = (row >> (ts.bit_length() - 1)) == (col >> (DQK.bit_length() - 1))
        qe[...] = jnp.where(same_map, qt, 0.0).astype(BF16)
        m_s[...] = jnp.full(m_s.shape, NEG_INF, F32)
        l_s[...] = jnp.zeros(l_s.shape, F32)
        acc[...] = jnp.zeros(acc.shape, F32)

    def update(kmat, vmat, causal, k_transposed):
        if k_transposed:
            s = jnp.dot(qe[...], kmat, preferred_element_type=F32)
        else:
            s = lax.dot_general(qe[...], kmat, (((1,), (1,)), ((), ())), preferred_element_type=F32)
        if causal:
            row = lax.broadcasted_iota(I32, s.shape, 0) & (ts - 1)
            col = lax.broadcasted_iota(I32, s.shape, 1)
            s = jnp.where(row >= col, s, NEG_INF)
        m_prev = m_s[...]
        m_next = jnp.maximum(m_prev, jnp.max(s, axis=1, keepdims=True))
        p = jnp.exp2(s - m_next[:, 0:1])
        alpha = jnp.exp2(m_prev - m_next)
        l_s[...] = alpha * l_s[...] + jnp.sum(p, axis=1, keepdims=True)
        acc[...] = jnp.tile(alpha, (1, V_W // LANES)) * acc[...] + jnp.dot(
            p.astype(BF16), vmat, preferred_element_type=F32)
        m_s[...] = m_next

    kmat = jnp.concatenate([r[...].reshape(QK_W, PAGE_SIZE) for r in kp], axis=1).astype(BF16)
    def heads_on_lanes(ref, n_pos):
        return jnp.concatenate([ref[pl.ds(h, n_pos, stride=N_HEADS), :] for h in range(N_HEADS)], axis=1)

    vmat = jnp.concatenate([heads_on_lanes(r, PAGE_SIZE) for r in vp], axis=0).astype(BF16)
    update(kmat, vmat, False, True)

    @pl.when(p_idx == n_steps - 1)
    def _():
        update(kn_ref[...].astype(BF16), heads_on_lanes(vn_ref, ts).astype(BF16), True, False)
        lam = _lambda_value(lq1, lk1, lq2, lk2)
        for h in range(N_HEADS):
            r1, r2 = 2 * h * ts, (2 * h + 1) * ts
            c0, c1 = h * DV, (h + 1) * DV
            o = _diff_merge(acc[r1:r1 + ts, c0:c1], l_s[r1:r1 + ts, :],
                            acc[r2:r2 + ts, c0:c1], l_s[r2:r2 + ts, :], lam, sg_ref[...])
            o_ref[:, c0:c1] = o


def _sample_attention(q_s, k_s, v_s, cache_k, cache_v, page_table, lams, subln_g):
    db, n_pages = page_table.shape
    ts = q_s.shape[0] // db
    n_pool = cache_k.shape[1]
    ck = jnp.transpose(cache_k, (0, 1, 3, 4, 2)).reshape(n_pool, 2 * N_HEADS, DQK, PAGE_SIZE)
    cv = cache_v.reshape(n_pool, PAGE_SIZE * N_HEADS, DV)
    n_steps = n_pages // PAGES_PER_STEP
    pt = page_table.reshape(-1).astype(I32)
    vec = lambda b, p, pt: (0, 0)
    new = lambda b, p, pt: (b, 0)

    def page_spec(j, block):
        def idx(b, p, pt):
            return (pt[b * n_pages + p * PAGES_PER_STEP + j],) + (0,) * (len(block) - 1)
        return pl.BlockSpec(block, idx)

    k_block = (None, 2 * N_HEADS, DQK, PAGE_SIZE)
    v_block = (None, PAGE_SIZE * N_HEADS, DV)

    rows = 2 * N_HEADS * ts
    grid_spec = pltpu.PrefetchScalarGridSpec(
        num_scalar_prefetch=1,
        grid=(db, n_steps),
        in_specs=[
            pl.BlockSpec((ts, QK_W), new), pl.BlockSpec((ts, QK_W), new), pl.BlockSpec((ts * N_HEADS, DV), new),
            pl.BlockSpec((1, DQK), vec), pl.BlockSpec((1, DQK), vec),
            pl.BlockSpec((1, DQK), vec), pl.BlockSpec((1, DQK), vec),
            pl.BlockSpec((1, DV), vec),
        ] + [page_spec(j, k_block) for j in range(PAGES_PER_STEP)]
          + [page_spec(j, v_block) for j in range(PAGES_PER_STEP)],
        out_specs=pl.BlockSpec((ts, V_W), new),
        scratch_shapes=[
            pltpu.VMEM((rows, QK_W), BF16),
            pltpu.VMEM((rows, LANES), F32),
            pltpu.VMEM((rows, LANES), F32),
            pltpu.VMEM((rows, V_W), F32),
        ],
    )
    return pl.pallas_call(
        functools.partial(_paged_attn_kernel, n_steps=n_steps, ts=ts),
        grid_spec=grid_spec,
        out_shape=jax.ShapeDtypeStruct((db * ts, V_W), F32),
        compiler_params=_cparams(("parallel", "arbitrary")),
        name="paged_attn",
    )(pt, q_s, k_s, v_s, *lams, subln_g.reshape(1, DV), *([ck] * PAGES_PER_STEP), *([cv] * PAGES_PER_STEP))


ROW_CHUNKS = D_MODEL // LANES


def _store_token_rows(ref, x):
    t = x.shape[0]
    for j in range(ROW_CHUNKS):
        ref[pl.ds(j, t, stride=ROW_CHUNKS), :] = x[:, j * LANES:(j + 1) * LANES]


def _load_token_rows(ref, t, lead=()):
    return jnp.concatenate(
        [ref[lead + (pl.ds(j, t, stride=ROW_CHUNKS), slice(None))] for j in range(ROW_CHUNKS)], axis=1)


def _split_bf16(x):
    hi = x.astype(BF16)
    return hi, (x - hi.astype(F32)).astype(BF16)


def _conv_ln_swish(y, lng, lnb):
    mu = jnp.mean(y, axis=-1, keepdims=True)
    var = jnp.mean(jnp.square(y - mu), axis=-1, keepdims=True)
    yn = (y - mu) * lax.rsqrt(var + EPS) * lng + lnb
    return yn * jax.nn.sigmoid(yn)


N_MIX_PROMPT_INPUTS = 16


def _mix_kernel(*refs, prompt, tiles_per_seq, n_real):
    i = pl.program_id(0)

    @pl.when(i < n_real)
    def _():
        _mix_body(*refs, prompt=prompt, tiles_per_seq=tiles_per_seq)

    if prompt:
        @pl.when(i == n_real)
        def _():
            for ref in refs[N_MIX_PROMPT_INPUTS:N_MIX_PROMPT_INPUTS + 3]:
                ref[...] = jnp.zeros(ref.shape, ref.dtype)


def _mix_body(*refs, prompt, tiles_per_seq):
    if prompt:
        (o_ref, ucur_ref, uhalo_ref, x_ref, g1_ref, sc2_ref, sh2_ref, wdw_ref, bdw_ref, lng_ref, lnb_ref,
         wout_ref, n2g_ref, wr_ref, br_ref, basein_ref,
         hp1_ref, n2_ref, meta_ref, cnt_ref, ext, shifted, base) = refs
    else:
        (o_ref, uext_ref, x_ref, g1_ref, sc2_ref, sh2_ref, wdw_ref, bdw_ref, lng_ref, lnb_ref,
         wout_ref, n2g_ref, wr_ref, br_ref, basein_ref, hp1_in, n2_in, meta_in,
         hp1_ref, n2_ref, meta_ref, cnt_ref, base) = refs
    i = pl.program_id(0)

    @pl.when(i == 0)
    def _():
        base[...] = basein_ref[0:1, :]

    if prompt:
        first = (i % tiles_per_seq) == 0
        halo = uhalo_ref[...]
        ext[0:HALO, :] = jnp.where(first, jnp.zeros_like(halo), halo)
        ext[HALO:HALO + TM, :] = ucur_ref[...]
        off = HALO - (CONV_K - 1)
        chunks = []
        for phase in range(SUBLANES):
            q_max = max([(off + j) // SUBLANES for j in range(CONV_K) if (off + j) % SUBLANES == phase])
            span = TM + SUBLANES * q_max
            shifted[phase, 0:span, :] = ext[pl.ds(phase, span), :]
        for c in range(TM // CONV_CHUNK):
            a = jnp.zeros((CONV_CHUNK, CONV_CH), F32) + bdw_ref[...]
            for j in range(CONV_K):
                phase, q = (off + j) % SUBLANES, (off + j) // SUBLANES
                start = c * CONV_CHUNK + SUBLANES * q
                a = a + wdw_ref[j:j + 1, :] * shifted[phase, start:start + CONV_CHUNK, :]
            chunks.append(a)
        y = jnp.concatenate(chunks, axis=0)
    else:
        nb, text, _ = uext_ref.shape
        ts = text - (CONV_K - 1)
        a = jnp.zeros((nb, ts, CONV_CH), F32) + bdw_ref[...]
        for j in range(CONV_K):
            a = a + wdw_ref[j:j + 1, :] * uext_ref[:, j:j + ts, :]
        y = a.reshape(nb * ts, CONV_CH)
    yc = _conv_ln_swish(y, lng_ref[...], lnb_ref[...])

    proj = (jnp.dot(o_ref[...].astype(BF16), wout_ref[0:N_HEADS * DV, :], preferred_element_type=F32)
            + jnp.dot(yc.astype(BF16), wout_ref[N_HEADS * DV:, :], preferred_element_type=F32))
    hp1 = x_ref[...] + g1_ref[0] * proj
    hp1_ref[...] = hp1
    n2 = hp1 * lax.rsqrt(jnp.mean(hp1 * hp1, axis=-1, keepdims=True) + EPS) * n2g_ref[...]
    n2 = n2 * (1.0 + sc2_ref[0]) + sh2_ref[0]
    _store_token_rows(n2_ref, n2)

    n_hi, n_lo = _split_bf16(n2)
    w_hi, w_lo = _split_bf16(wr_ref[...])
    logits = (jnp.dot(n_hi, w_hi, preferred_element_type=F32)
              + (jnp.dot(n_hi, w_lo, preferred_element_type=F32) + jnp.dot(n_lo, w_hi, preferred_element_type=F32))
              + br_ref[...])
    lane = lax.broadcasted_iota(I32, logits.shape, 1)
    lane_f = lane.astype(F32)
    lg = jnp.where(lane < N_EXPERTS, logits, NEG_INF)
    onehots, vals, idxs = [], [], []
    for _ in range(TOP_K):
        mx = jnp.max(lg, axis=-1, keepdims=True)
        idx = jnp.min(jnp.where(lg == mx, lane_f, float(LANES)), axis=-1, keepdims=True)
        oh = lane_f == idx
        lg = jnp.where(oh, NEG_INF, lg)
        onehots.append(oh)
        vals.append(mx)
        idxs.append(idx)
    exps = [jnp.exp(v - vals[0]) for v in vals]
    denom = exps[0] + exps[1] + exps[2] + exps[3]

    sel = jnp.zeros(logits.shape, F32)
    for oh in onehots:
        sel = sel + oh.astype(F32)
    r_i = lax.broadcasted_iota(I32, (TM, TM), 0)
    c_i = lax.broadcasted_iota(I32, (TM, TM), 1)
    ltri = (r_i > c_i).astype(BF16)
    before = jnp.dot(ltri, sel.astype(BF16), preferred_element_type=F32) + base[...]
    meta = jnp.zeros(logits.shape, F32)
    for k in range(TOP_K):
        rank = jnp.sum(jnp.where(onehots[k], before, 0.0), axis=-1, keepdims=True)
        meta = meta + jnp.where(lane == k, idxs[k], 0.0)
        meta = meta + jnp.where(lane == TOP_K + k, exps[k] / denom, 0.0)
        meta = meta + jnp.where(lane == 2 * TOP_K + k, rank, 0.0)
    meta_ref[...] = meta
    new_base = base[...] + jnp.sum(sel, axis=0, keepdims=True)
    base[...] = new_base
    cnt_ref[...] = jnp.broadcast_to(new_base, cnt_ref.shape)


def _mix_prompt(o_attn, u, x, g1, sc2, sh2, conv_w, w_out_bf16, norm2_g, w_router_pad, b_router_pad,
                base_in, n_total, tiles_per_seq):
    n = x.shape[0]
    nt = n // TM
    assert n_total == n + TM
    row = lambda i: (jnp.minimum(i, nt - 1), 0)
    out_row = lambda i: (i, 0)
    const2 = lambda i: (0, 0)
    grp = lambda i: (jnp.minimum(i, nt - 1) // tiles_per_seq, 0, 0)
    halo_row = lambda i: (jnp.maximum(jnp.minimum(i, nt - 1) * (TM // HALO) - 1, 0), 0)
    wdw, bdw, lng, lnb = conv_w
    in_specs = [
        pl.BlockSpec((TM, N_HEADS * DV), row),
        pl.BlockSpec((TM, CONV_CH), row),
        pl.BlockSpec((HALO, CONV_CH), halo_row),
        pl.BlockSpec((TM, D_MODEL), row),
        pl.BlockSpec((1, 1, D_MODEL), grp), pl.BlockSpec((1, 1, D_MODEL), grp), pl.BlockSpec((1, 1, D_MODEL), grp),
        pl.BlockSpec((CONV_K, CONV_CH), const2), pl.BlockSpec((1, CONV_CH), const2),
        pl.BlockSpec((1, CONV_CH), const2), pl.BlockSpec((1, CONV_CH), const2),
        pl.BlockSpec((D_MODEL, D_MODEL), const2),
        pl.BlockSpec((1, D_MODEL), const2),
        pl.BlockSpec((D_MODEL, LANES), const2), pl.BlockSpec((1, LANES), const2),
        pl.BlockSpec((SUBLANES, LANES), const2),
    ]
    assert len(in_specs) == N_MIX_PROMPT_INPUTS
    out_specs = [
        pl.BlockSpec((TM, D_MODEL), out_row),
        pl.BlockSpec((TM * ROW_CHUNKS, LANES), out_row),
        pl.BlockSpec((TM, LANES), out_row),
        pl.BlockSpec((SUBLANES, LANES), const2),
    ]
    out_shape = [
        jax.ShapeDtypeStruct((n_total, D_MODEL), F32),
        jax.ShapeDtypeStruct((n_total * ROW_CHUNKS, LANES), F32),
        jax.ShapeDtypeStruct((n_total, LANES), F32),
        jax.ShapeDtypeStruct((SUBLANES, LANES), F32),
    ]
    return pl.pallas_call(
        functools.partial(_mix_kernel, prompt=True, tiles_per_seq=tiles_per_seq, n_real=nt),
        grid=(nt + 1,),
        in_specs=in_specs,
        out_specs=out_specs,
        out_shape=out_shape,
        scratch_shapes=[
            pltpu.VMEM((HALO + TM, CONV_CH), F32),
            pltpu.VMEM((SUBLANES, HALO + TM, CONV_CH), F32),
            pltpu.VMEM((1, LANES), F32),
        ],
        compiler_params=_cparams(("arbitrary",)),
        name="mix_prompt",
    )(o_attn, u, u, x, g1, sc2, sh2, wdw, bdw.reshape(1, -1), lng.reshape(1, -1), lnb.reshape(1, -1),
      w_out_bf16, norm2_g.reshape(1, -1), w_router_pad, b_router_pad, base_in)


def _mix_sample(o_attn, uext, x, g1, sc2, sh2, conv_w, w_out_bf16, norm2_g, w_router_pad, b_router_pad,
                base_in, hp1_all, n2_all, meta_all, tile0):
    const2 = lambda i: (0, 0)
    const3 = lambda i: (0, 0, 0)
    out_row = lambda i: (tile0, 0)
    wdw, bdw, lng, lnb = conv_w
    nb, text, _ = uext.shape
    in_specs = [
        pl.BlockSpec((TM, N_HEADS * DV), const2),
        pl.BlockSpec((nb, text, CONV_CH), const3),
        pl.BlockSpec((TM, D_MODEL), const2),
        pl.BlockSpec((1, TM, D_MODEL), const3), pl.BlockSpec((1, TM, D_MODEL), const3),
        pl.BlockSpec((1, TM, D_MODEL), const3),
        pl.BlockSpec((CONV_K, CONV_CH), const2), pl.BlockSpec((1, CONV_CH), const2),
        pl.BlockSpec((1, CONV_CH), const2), pl.BlockSpec((1, CONV_CH), const2),
        pl.BlockSpec((D_MODEL, D_MODEL), const2),
        pl.BlockSpec((1, D_MODEL), const2),
        pl.BlockSpec((D_MODEL, LANES), const2), pl.BlockSpec((1, LANES), const2),
        pl.BlockSpec((SUBLANES, LANES), const2),
        pl.BlockSpec(memory_space=pl.ANY), pl.BlockSpec(memory_space=pl.ANY), pl.BlockSpec(memory_space=pl.ANY),
    ]
    out_specs = [
        pl.BlockSpec((TM, D_MODEL), out_row),
        pl.BlockSpec((TM * ROW_CHUNKS, LANES), out_row),
        pl.BlockSpec((TM, LANES), out_row),
        pl.BlockSpec((SUBLANES, LANES), const2),
    ]
    out_shape = [
        jax.ShapeDtypeStruct(hp1_all.shape, F32),
        jax.ShapeDtypeStruct(n2_all.shape, F32),
        jax.ShapeDtypeStruct(meta_all.shape, F32),
        jax.ShapeDtypeStruct((SUBLANES, LANES), F32),
    ]
    return pl.pallas_call(
        functools.partial(_mix_kernel, prompt=False, tiles_per_seq=1, n_real=1),
        grid=(1,),
        in_specs=in_specs,
        out_specs=out_specs,
        out_shape=out_shape,
        scratch_shapes=[pltpu.VMEM((1, LANES), F32)],
        input_output_aliases={15: 0, 16: 1, 17: 2},
        compiler_params=_cparams(("arbitrary",)),
        name="mix_sample",
    )(o_attn, uext, x, g1, sc2, sh2, wdw, bdw.reshape(1, -1), lng.reshape(1, -1), lnb.reshape(1, -1),
      w_out_bf16, norm2_g.reshape(1, -1), w_router_pad, b_router_pad, base_in, hp1_all, n2_all, meta_all)


MAX_SLOT_TILES = 16


def _lane_cumsum(x, lane):
    s = 1
    while s < LANES:
        x = x + jnp.where(lane >= s, pltpu.roll(x, s, 1), 0)
        s *= 2
    return x


def _slots_kernel(meta_ref, cnt_ref, slots_ref, tmap_ref, einfo_ref, *, n_tiles_pad):
    shift = TM.bit_length() - 1
    lane8 = lax.broadcasted_iota(I32, (SUBLANES, LANES), 1)
    cnt = cnt_ref[...].astype(I32)
    padded = ((cnt + (TM - 1)) >> shift) << shift
    csum = _lane_cumsum(padded, lane8)
    gstart = csum - padded

    meta = meta_ref[...]
    lane = lax.broadcasted_iota(I32, meta.shape, 1)
    lane_f = lane.astype(F32)
    gstart_f = gstart[0:1, :].astype(F32)
    out = jnp.zeros(meta.shape, F32)
    for k in range(TOP_K):
        sel = lane_f == meta[:, k:k + 1]
        gs = jnp.sum(jnp.where(sel, gstart_f, 0.0), axis=-1, keepdims=True)
        out = out + jnp.where(lane == k, gs + meta[:, 2 * TOP_K + k: 2 * TOP_K + k + 1], 0.0)
    slots_ref[...] = out.astype(I32)

    @pl.when(pl.program_id(0) == 0)
    def _():
        ctiles = csum[0:1, :] >> shift
        n_valid = jnp.max(ctiles, axis=-1, keepdims=True)
        t = lax.broadcasted_iota(I32, (n_tiles_pad, LANES), 0)
        t = jnp.minimum(t, n_valid - 1)
        lane_t = lax.broadcasted_iota(I32, (n_tiles_pad, LANES), 1)
        hit = jnp.where((lane_t < N_EXPERTS) & (ctiles <= t), 1, 0)
        te = jnp.sum(hit, axis=-1, keepdims=True)
        tmap_ref[...] = jnp.broadcast_to(jnp.minimum(te, N_EXPERTS - 1), tmap_ref.shape)
        row8 = lax.broadcasted_iota(I32, (SUBLANES, LANES), 0)
        info = jnp.where(row8 == 0, gstart, 0)
        info = info + jnp.where(row8 == 1, padded, 0)
        info = info + jnp.where(row8 == 2, cnt, 0)
        info = info + jnp.where(row8 == 3, jnp.broadcast_to(n_valid, (SUBLANES, LANES)), 0)
        einfo_ref[...] = info


def _routing_slots(meta_all, counts, n_tiles_pad):
    n = meta_all.shape[0]
    const2 = lambda i: (0, 0)
    n_tm = n // TM
    group = max(g for g in range(1, MAX_SLOT_TILES + 1) if n_tm % g == 0)
    rows = group * TM
    return pl.pallas_call(
        functools.partial(_slots_kernel, n_tiles_pad=n_tiles_pad),
        grid=(n // rows,),
        in_specs=[pl.BlockSpec((rows, LANES), lambda i: (i, 0)), pl.BlockSpec((SUBLANES, LANES), const2)],
        out_specs=[
            pl.BlockSpec((rows, LANES), lambda i: (i, 0)),
            pl.BlockSpec((n_tiles_pad, LANES), const2),
            pl.BlockSpec((SUBLANES, LANES), const2),
        ],
        out_shape=[
            jax.ShapeDtypeStruct((n, LANES), I32),
            jax.ShapeDtypeStruct((n_tiles_pad, LANES), I32),
            jax.ShapeDtypeStruct((SUBLANES, LANES), I32),
        ],
        compiler_params=_cparams(("arbitrary",)),
        name="routing_slots",
    )(meta_all, counts)


def _token_rows(ref, row):
    return ref.at[pl.ds(pl.multiple_of(row * ROW_CHUNKS, ROW_CHUNKS), ROW_CHUNKS)]


def _dispatch_kernel(gstart_ref, padded_ref, cnt_ref, nv_ref, slots_hbm, src_ref, xs_hbm,
                     idx0, idx1, zbuf, sem, zsem, isems):
    i = pl.program_id(0)
    tile_rows = TM * ROW_CHUNKS
    n_slot_tiles = xs_hbm.shape[0] // tile_rows

    def zero_tile_copy(tile):
        start = pl.multiple_of(tile * tile_rows, tile_rows)
        return pltpu.make_async_copy(zbuf, xs_hbm.at[pl.ds(start, tile_rows)], zsem)

    def pad_tile_copy(e):
        return zero_tile_copy((gstart_ref[e] + padded_ref[e]) // TM - 1)

    @pl.when(i == 0)
    def _():
        zbuf[...] = jnp.zeros(zbuf.shape, zbuf.dtype)
        for e in range(N_EXPERTS):
            @pl.when(cnt_ref[e] > 0)
            def _():
                pad_tile_copy(e).start()

        def start_unused(t, carry):
            zero_tile_copy(t).start()
            return carry

        def wait_unused(t, carry):
            zero_tile_copy(t).wait()
            return carry

        lax.fori_loop(nv_ref[0], n_slot_tiles, start_unused, 0)
        for e in range(N_EXPERTS):
            @pl.when(cnt_ref[e] > 0)
            def _():
                pad_tile_copy(e).wait()
        lax.fori_loop(nv_ref[0], n_slot_tiles, wait_unused, 0)

    n_steps = pl.num_programs(0)
    idxs = (idx0, idx1)

    def idx_copy(step, parity):
        return pltpu.make_async_copy(slots_hbm.at[step], idxs[parity], isems.at[parity])

    def for_each_row(parity, fn):
        def body(r, carry):
            for k in range(TOP_K):
                fn(pltpu.make_async_copy(_token_rows(src_ref, r),
                                         _token_rows(xs_hbm, idxs[parity][0, r * TOP_K + k]), sem), k)
            return carry
        lax.fori_loop(0, TM, body, 0, unroll=8)

    @pl.when(i == 0)
    def _():
        idx_copy(0, 0).start()

    def step(parity):
        @pl.when(i + 1 < n_steps)
        def _():
            idx_copy(i + 1, 1 - parity).start()

        idx_copy(i, parity).wait()
        for_each_row(parity, lambda c, k: c.start(priority=k % N_DMA_PRIORITIES))
        for_each_row(parity, lambda c, k: c.wait())

    for parity in range(2):
        @pl.when(i % 2 == parity)
        def _():
            step(parity)


def _dispatch(gstart, padded, cnt, n_valid, slots2d, n2_all, s_max):
    nt = slots2d.shape[0]
    grid_spec = pltpu.PrefetchScalarGridSpec(
        num_scalar_prefetch=4,
        grid=(nt,),
        in_specs=[
            pl.BlockSpec(memory_space=pl.ANY),
            pl.BlockSpec((TM * ROW_CHUNKS, LANES), lambda i, *_: (i, 0)),
        ],
        out_specs=pl.BlockSpec(memory_space=pl.ANY),
        scratch_shapes=[
            pltpu.SMEM((1, TM * TOP_K), I32),
            pltpu.SMEM((1, TM * TOP_K), I32),
            pltpu.VMEM((TM * ROW_CHUNKS, LANES), F32),
            pltpu.SemaphoreType.DMA,
            pltpu.SemaphoreType.DMA,
            pltpu.SemaphoreType.DMA((2,)),
        ],
    )
    return pl.pallas_call(
        _dispatch_kernel,
        grid_spec=grid_spec,
        out_shape=jax.ShapeDtypeStruct((s_max * ROW_CHUNKS, LANES), F32),
        compiler_params=pltpu.CompilerParams(dimension_semantics=("arbitrary",), has_side_effects=True),
        name="moe_dispatch",
    )(gstart, padded, cnt, n_valid, slots2d, n2_all)


DEINT_BLOCK = 2 * LANES
EXPERT_VMEM_LIMIT = 56 * 1024 * 1024


def _expert_kernel(te_ref, nv_ref, x_ref, wgu_hbm, wd_hbm, b1g_ref, b1u_ref, b2_ref, y_ref,
                   wgu_buf, wd_buf, w1g, w1u, w2, wsem, slot_ref):
    i = pl.program_id(0)
    n_valid = nv_ref[0]
    valid = i < n_valid
    expert = te_ref[i]
    new_expert = jnp.logical_or(i == 0, expert != te_ref[jnp.maximum(i - 1, 0)])
    last_tile = te_ref.shape[0] - 1

    def weight_copies(e, s):
        return (pltpu.make_async_copy(wgu_hbm.at[e], wgu_buf.at[s], wsem.at[0, s]),
                pltpu.make_async_copy(wd_hbm.at[e], wd_buf.at[s], wsem.at[1, s]))

    @pl.when(i == 0)
    def _():
        slot_ref[0] = 0
        for c in weight_copies(expert, 0):
            c.start()

    @pl.when(jnp.logical_and(valid, new_expert))
    def _():
        s = slot_ref[0]
        nxt = lax.while_loop(
            lambda j: jnp.logical_and(j < n_valid, te_ref[jnp.minimum(j, last_tile)] == expert),
            lambda j: j + 1, i + 1)

        @pl.when(nxt < n_valid)
        def _():
            for c in weight_copies(te_ref[jnp.minimum(nxt, last_tile)], 1 - s):
                c.start()

        for c in weight_copies(expert, s):
            c.wait()
        r = lax.broadcasted_iota(I32, (DEINT_BLOCK, DEINT_BLOCK), 0)
        c = lax.broadcasted_iota(I32, (DEINT_BLOCK, DEINT_BLOCK), 1)
        src_col = jnp.where(c < LANES, 2 * c, 2 * (c - LANES) + 1)
        perm = jnp.where(r == src_col, 1.0, 0.0).astype(BF16)
        for blk in range(wgu_buf.shape[2] // DEINT_BLOCK):
            cols = wgu_buf[s, :, blk * DEINT_BLOCK:(blk + 1) * DEINT_BLOCK].astype(BF16)
            split = jnp.dot(cols, perm, preferred_element_type=F32).astype(BF16)
            w1g[:, blk * LANES:(blk + 1) * LANES] = split[:, :LANES]
            w1u[:, blk * LANES:(blk + 1) * LANES] = split[:, LANES:]
        w2[...] = wd_buf[s].astype(BF16)
        slot_ref[0] = 1 - s

    @pl.when(valid)
    def _():
        x = _load_token_rows(x_ref, TM).astype(BF16)
        hg = jnp.dot(x, w1g[...], preferred_element_type=F32) + b1g_ref[0]
        hu = jnp.dot(x, w1u[...], preferred_element_type=F32) + b1u_ref[0]
        gate = jnp.minimum(hg, SWIGLU_LIMIT)
        up = jnp.clip(hu, -SWIGLU_LIMIT, SWIGLU_LIMIT)
        act = (up + 1.0) * gate * jax.nn.sigmoid(SWIGLU_ALPHA * gate)
        y = jnp.dot(act.astype(BF16), w2[...], preferred_element_type=F32) + b2_ref[0]
        _store_token_rows(y_ref, y)

    @pl.when(jnp.logical_not(valid))
    def _():
        y_ref[...] = jnp.zeros(y_ref.shape, y_ref.dtype)


def _expert_mlp(tile_expert, n_valid, xs, w_gate_up, w_down, b1g, b1u, b2):
    tile_rows = TM * ROW_CHUNKS
    nt = xs.shape[0] // tile_rows
    ff = w_down.shape[1]
    tile = lambda i, te, nv: (jnp.minimum(i, nv[0] - 1), 0)
    wsel = lambda i, te, nv: (te[i], 0, 0)
    grid_spec = pltpu.PrefetchScalarGridSpec(
        num_scalar_prefetch=2,
        grid=(nt,),
        in_specs=[
            pl.BlockSpec((tile_rows, LANES), tile),
            pl.BlockSpec(memory_space=pl.ANY),
            pl.BlockSpec(memory_space=pl.ANY),
            pl.BlockSpec((1, 1, ff), wsel),
            pl.BlockSpec((1, 1, ff), wsel),
            pl.BlockSpec((1, 1, D_MODEL), wsel),
        ],
        out_specs=pl.BlockSpec((tile_rows, LANES), lambda i, te, nv: (i, 0)),
        scratch_shapes=[
            pltpu.VMEM((2, D_MODEL, 2 * ff), F32),
            pltpu.VMEM((2, ff, D_MODEL), F32),
            pltpu.VMEM((D_MODEL, ff), BF16),
            pltpu.VMEM((D_MODEL, ff), BF16),
            pltpu.VMEM((ff, D_MODEL), BF16),
            pltpu.SemaphoreType.DMA((2, 2)),
            pltpu.SMEM((1,), I32),
        ],
    )
    return pl.pallas_call(
        _expert_kernel,
        grid_spec=grid_spec,
        out_shape=jax.ShapeDtypeStruct(xs.shape, F32),
        compiler_params=pltpu.CompilerParams(dimension_semantics=("arbitrary",),
                                             vmem_limit_bytes=EXPERT_VMEM_LIMIT),
        name="expert_mlp",
    )(tile_expert, n_valid, xs, w_gate_up, w_down, b1g, b1u, b2)


def _combine_kernel(slots_hbm, ys_hbm, hp1_ref, meta_ref, g2_ref, nf_ref, o_ref,
                    idx0, idx1, buf0, buf1, sems, isems, *, tile0):
    i = pl.program_id(0)
    n_steps = pl.num_programs(0)
    bufs = (buf0, buf1)
    idxs = (idx0, idx1)

    def idx_copy(step, parity):
        return pltpu.make_async_copy(slots_hbm.at[tile0 + step], idxs[parity], isems.at[parity])

    def for_each_row(parity, fn):
        def body(r, carry):
            for k in range(TOP_K):
                fn(pltpu.make_async_copy(_token_rows(ys_hbm, idxs[parity][0, r * TOP_K + k]),
                                         _token_rows(bufs[parity].at[k], r), sems.at[parity]), k)
            return carry
        lax.fori_loop(0, TM, body, 0, unroll=8)

    def start_rows(parity):
        for_each_row(parity, lambda c, k: c.start(priority=k % N_DMA_PRIORITIES))

    @pl.when(i == 0)
    def _():
        first = idx_copy(0, 0)
        first.start()
        first.wait()
        start_rows(0)

        @pl.when(n_steps > 1)
        def _():
            idx_copy(1, 1).start()

    def step(parity):
        @pl.when(i + 1 < n_steps)
        def _():
            idx_copy(i + 1, 1 - parity).wait()
            start_rows(1 - parity)

        for_each_row(parity, lambda c, k: c.wait())

        @pl.when(i + 2 < n_steps)
        def _():
            idx_copy(i + 2, parity).start()

        meta = meta_ref[...]
        f = jnp.zeros((TM, D_MODEL), F32)
        for k in range(TOP_K):
            f = f + meta[:, TOP_K + k: TOP_K + k + 1] * _load_token_rows(bufs[parity], TM, (k,))
        hp2 = hp1_ref[...] + g2_ref[0] * f
        o_ref[...] = hp2 * lax.rsqrt(jnp.mean(hp2 * hp2, axis=-1, keepdims=True) + EPS) * nf_ref[...]

    for parity in range(2):
        @pl.when(i % 2 == parity)
        def _():
            step(parity)


def _combine(slots2d, ys, hp1_all, meta_all, g2, normf_g, tile0, n_rows, tiles_per_group):
    nt = n_rows // TM
    r = g2.shape[1]
    row_in = lambda i: (tile0 + i, 0)
    return pl.pallas_call(
        functools.partial(_combine_kernel, tile0=tile0),
        grid=(nt,),
        in_specs=[
            pl.BlockSpec(memory_space=pl.ANY),
            pl.BlockSpec(memory_space=pl.ANY),
            pl.BlockSpec((TM, D_MODEL), row_in),
            pl.BlockSpec((TM, LANES), row_in),
            pl.BlockSpec((1, r, D_MODEL), lambda i: (i // tiles_per_group, 0, 0)),
            pl.BlockSpec((1, D_MODEL), lambda i: (0, 0)),
        ],
        out_specs=pl.BlockSpec((TM, D_MODEL), lambda i: (i, 0)),
        out_shape=jax.ShapeDtypeStruct((n_rows, D_MODEL), F32),
        scratch_shapes=[
            pltpu.SMEM((1, TM * TOP_K), I32),
            pltpu.SMEM((1, TM * TOP_K), I32),
            pltpu.VMEM((TOP_K, TM * ROW_CHUNKS, LANES), F32),
            pltpu.VMEM((TOP_K, TM * ROW_CHUNKS, LANES), F32),
            pltpu.SemaphoreType.DMA((2,)),
            pltpu.SemaphoreType.DMA((2,)),
        ],
        compiler_params=_cparams(("arbitrary",)),
        name="moe_combine",
    )(slots2d, ys, hp1_all, meta_all, g2, normf_g.reshape(1, D_MODEL))


def kernel(x_prompt, x_sample, cache_k, cache_v, state_conv, page_table, c_prompt, c_sample, norm1_g, norm2_g, w_ada, b_ada, w_in, lambda_q1, lambda_k1, lambda_q2, lambda_k2, subln_g, w_dw, b_dw, conv_ln_g, conv_ln_b, w_out, w_router, b_router, w_gate_up, b_gate_up, w_down, b_down, normf_g):
    depth = norm1_g.shape[0]
    assert depth == 1, "single-layer step"
    bsz, seq, d = x_prompt.shape
    db, ts, _ = x_sample.shape
    n_p, n_s = bsz * seq, db * ts
    assert d == D_MODEL and n_s == TM and seq % TQ == 0 and n_p % TM == 0
    n_all = n_p + n_s
    n_pages = page_table.shape[1]
    past = n_pages * cache_k.shape[2]
    tiles_per_seq = seq // TM
    l = 0

    n_cond = bsz + db
    c_all = jnp.concatenate([c_prompt, c_sample], axis=0)
    c_all = jnp.pad(c_all, ((0, -n_cond % SUBLANES), (0, 0)))
    mod = _adaln(c_all, w_ada[l], b_ada[l])[:n_cond]
    mod_p = mod[:bsz].reshape(bsz, 1, 6, D_MODEL)
    mod_s = jnp.repeat(mod[bsz:], ts, axis=0).reshape(1, n_s, 6, D_MODEL)
    sh1p, sc1p, g1p, sh2p, sc2p, g2p = [mod_p[:, :, j] for j in range(6)]
    sh1s, sc1s, g1s, sh2s, sc2s, g2s = [mod_s[:, :, j] for j in range(6)]

    lams = [v[l].reshape(1, DQK) for v in (lambda_q1, lambda_k1, lambda_q2, lambda_k2)]
    w_in_b = w_in[l].astype(BF16)
    w_out_b = w_out[l].astype(BF16)
    conv_w = (w_dw[l], b_dw[l], conv_ln_g[l], conv_ln_b[l])
    w_router_pad = jnp.pad(w_router[l], ((0, 0), (0, LANES - N_EXPERTS)))
    b_router_pad = jnp.pad(b_router[l], (0, LANES - N_EXPERTS)).reshape(1, LANES)

    xp = x_prompt.reshape(n_p, D_MODEL)
    xs_tok = x_sample.reshape(n_s, D_MODEL)
    tabs_p = _rope_tables(jnp.arange(seq))
    tabs_s = _rope_tables(jnp.tile(past + jnp.arange(ts), db))
    q_hm, k_p, v_p, k_hm, v_hm, u_p = _inproj(xp, norm1_g[l], sc1p, sh1p, tabs_p, w_in_b,
                                              tiles_per_seq, tiles_per_seq, True)
    q_s, k_s, v_s, u_s = _inproj(xs_tok, norm1_g[l], sc1s, sh1s, tabs_s, w_in_b, 1, 1, False)

    o_p = _prompt_attention(q_hm, k_hm, v_hm, lams, subln_g[l], bsz, seq)
    o_s = _sample_attention(q_s, k_s, v_s, cache_k, cache_v, page_table, lams, subln_g[l])

    uext_s = jnp.concatenate([state_conv[l], u_s.reshape(db, ts, CONV_CH)], axis=1)

    zeros_base = jnp.zeros((SUBLANES, LANES), F32)
    hp1_all, n2_all, meta_all, cnt_p = _mix_prompt(
        o_p, u_p, xp, g1p, sc2p, sh2p, conv_w, w_out_b, norm2_g[l], w_router_pad, b_router_pad,
        zeros_base, n_all, tiles_per_seq)
    hp1_all, n2_all, meta_all, counts = _mix_sample(
        o_s, uext_s, xs_tok, g1s, sc2s, sh2s, conv_w, w_out_b, norm2_g[l], w_router_pad, b_router_pad,
        cnt_p, hp1_all, n2_all, meta_all, n_p // TM)

    n_tok_tiles = n_all // TM
    s_max = n_all * TOP_K + N_EXPERTS * TM
    n_slot_tiles = s_max // TM
    n_tiles_pad = -(-n_slot_tiles // SUBLANES) * SUBLANES
    slots, tmap, einfo = _routing_slots(meta_all, counts, n_tiles_pad)
    slots2d = slots[:, :TOP_K].reshape(n_tok_tiles, 1, TM * TOP_K)
    tile_expert = tmap[:n_slot_tiles, 0]
    gstart, padded, cnt = einfo[0, :N_EXPERTS], einfo[1, :N_EXPERTS], einfo[2, :N_EXPERTS]
    n_valid = einfo[3, :1]

    x_sorted = _dispatch(gstart, padded, cnt, n_valid, slots2d, n2_all, s_max)

    bgu = b_gate_up[l]
    b1g = bgu[:, 0::2].reshape(N_EXPERTS, 1, -1)
    b1u = bgu[:, 1::2].reshape(N_EXPERTS, 1, -1)
    b2 = b_down[l].reshape(N_EXPERTS, 1, D_MODEL)
    y_sorted = _expert_mlp(tile_expert, n_valid, x_sorted, w_gate_up.reshape(w_gate_up.shape[1:]),
                           w_down.reshape(w_down.shape[1:]), b1g, b1u, b2)

    y_p = _combine(slots2d, y_sorted, hp1_all, meta_all, g2p, normf_g, 0, n_p, tiles_per_seq)
    y_s = _combine(slots2d, y_sorted, hp1_all, meta_all, g2s, normf_g, n_p // TM, n_s, 1)

    y_prompt = y_p.reshape(bsz, seq, D_MODEL)
    y_sample = y_s.reshape(db, ts, D_MODEL)
    k_prompt = jnp.transpose(k_p, (0, 3, 1, 2))[None]
    v_prompt = v_p.reshape(1, bsz, seq, N_HEADS, DV)
    conv_prompt = u_p.reshape(bsz, seq, CONV_CH)[:, seq - (CONV_K - 1):][None]
    k_sample = k_s.reshape(1, db, ts, 2 * N_HEADS, DQK)
    v_sample = v_s.reshape(1, db, ts, N_HEADS, DV)
    conv_sample = uext_s[:, ts:][None]
    return (y_prompt, y_sample, k_prompt, v_prompt, conv_prompt, k_sample, v_sample, conv_sample)
```

```python
import functools
import math

import jax
import jax.numpy as jnp
from jax import lax
from jax.experimental import pallas as pl
from jax.experimental.pallas import tpu as pltpu

F32 = jnp.float32
BF16 = jnp.bfloat16
I32 = jnp.int32
HIGHEST = lax.Precision.HIGHEST

D_MODEL = 1024
N_HEADS = 4
DV = 128
DQK = 64
ROT_DIM = 16
ROPE_THETA = 500000.0
CONV_K = 31
CONV_CH = 512
QK_W = 512
V_W = 512
IN_WIDTH = 2 * QK_W + V_W + 2 * CONV_CH
N_EXPERTS = 32
TOP_K = 4
SWIGLU_LIMIT = 7.0
SWIGLU_ALPHA = 1.702
EPS = 1e-5
LAM_INIT = 0.8 - 0.6 * math.exp(-0.3 * 0)
PAGE_SIZE = 128
Q_SCALE = DQK ** -0.5 * math.log2(math.e)

LANES = 128
SUBLANES = 8
VMEM_LIMIT = 48 * 1024 * 1024
N_DMA_PRIORITIES = 2

TM = 256
TQ = 512
TK = 512
PAGES_PER_STEP = 32
HALO = 32
CONV_CHUNK = 64
NEG_INF = float("-inf")


def _cparams(sem):
    return pltpu.CompilerParams(dimension_semantics=sem, vmem_limit_bytes=VMEM_LIMIT)


def _adaln_kernel(c_ref, w_ref, b_ref, o_ref):
    c = c_ref[...]
    s = c * jax.nn.sigmoid(c)
    o_ref[...] = jnp.dot(s, w_ref[...], precision=HIGHEST, preferred_element_type=F32) + b_ref[...]


def _adaln(c_all, w, b):
    n, d = c_all.shape
    width = w.shape[1]
    bn = 1536
    return pl.pallas_call(
        _adaln_kernel,
        grid=(width // bn,),
        in_specs=[
            pl.BlockSpec((n, d), lambda j: (0, 0)),
            pl.BlockSpec((d, bn), lambda j: (0, j)),
            pl.BlockSpec((1, bn), lambda j: (0, j)),
        ],
        out_specs=pl.BlockSpec((n, bn), lambda j: (0, j)),
        out_shape=jax.ShapeDtypeStruct((n, width), F32),
        compiler_params=_cparams(("parallel",)),
        name="adaln",
    )(c_all, w, b.reshape(1, width))


def _rope_tables(pos):
    inv = ROPE_THETA ** (-jnp.arange(0, ROT_DIM, 2, dtype=F32) / ROT_DIM)
    ang = pos.astype(F32)[:, None] * inv
    cos, sin = jnp.cos(ang), jnp.sin(ang)
    ones = jnp.ones((pos.shape[0], DQK - ROT_DIM), F32)
    zeros_r = jnp.zeros((pos.shape[0], DQK - ROT_DIM), F32)
    c64 = jnp.concatenate([cos, cos, ones], axis=1)
    s64 = jnp.concatenate([-sin, sin, zeros_r], axis=1)
    rep = LANES // DQK
    return jnp.tile(c64, (1, rep)), jnp.tile(s64, (1, rep))


def _inproj_kernel(x_ref, g_ref, sc_ref, sh_ref, cos_ref, sin_ref, w_ref, *out_refs, head_major):
    x = x_ref[...]
    hn = x * lax.rsqrt(jnp.mean(x * x, axis=-1, keepdims=True) + EPS) * g_ref[...]
    hn = hn * (1.0 + sc_ref[0]) + sh_ref[0]
    proj = jnp.dot(hn.astype(BF16), w_ref[...], preferred_element_type=F32)
    cos, sin = cos_ref[...], sin_ref[...]
    lane = lax.broadcasted_iota(I32, cos.shape, 1)
    first_half = (lane & (DQK - 1)) < ROT_DIM // 2

    def rope(blk):
        partner = jnp.where(first_half, pltpu.roll(blk, LANES - ROT_DIM // 2, 1), pltpu.roll(blk, ROT_DIM // 2, 1))
        return blk * cos + partner * sin

    a = proj[:, 2 * QK_W + V_W: 2 * QK_W + V_W + CONV_CH]
    gl = proj[:, 2 * QK_W + V_W + CONV_CH:]
    if head_major:
        qb_ref, k32_ref, v32_ref, kb_ref, vb_ref, u_ref = out_refs
    else:
        q32_ref, k32_ref, v32_ref, u_ref = out_refs
    u_ref[...] = a * jax.nn.sigmoid(gl)
    for h in range(N_HEADS):
        lo, hi = h * LANES, (h + 1) * LANES
        qh = rope(proj[:, lo:hi]) * Q_SCALE
        kh = rope(proj[:, QK_W + lo: QK_W + hi])
        vh = proj[:, 2 * QK_W + lo: 2 * QK_W + hi]
        v32_ref[pl.ds(h, x.shape[0], stride=N_HEADS), :] = vh
        if head_major:
            k32_ref[0, 2 * h:2 * h + 2] = kh.T.reshape(2, DQK, x.shape[0])
            qb_ref[h] = qh.astype(BF16)
            kb_ref[h] = kh.astype(BF16)
            vb_ref[h] = vh.astype(BF16)
        else:
            k32_ref[:, lo:hi] = kh
            q32_ref[:, lo:hi] = qh


def _inproj(x, norm_g, sc, sh, tabs, w_in_bf16, tiles_per_group, tiles_per_seq, head_major):
    n = x.shape[0]
    nt = n // TM
    r = sc.shape[1]
    row = lambda i: (i, 0)
    grp = lambda i: (i // tiles_per_group, 0, 0)
    tab = lambda i: (i % tiles_per_seq, 0)
    in_specs = [
        pl.BlockSpec((TM, D_MODEL), row),
        pl.BlockSpec((1, D_MODEL), lambda i: (0, 0)),
        pl.BlockSpec((1, r, D_MODEL), grp),
        pl.BlockSpec((1, r, D_MODEL), grp),
        pl.BlockSpec((TM, LANES), tab),
        pl.BlockSpec((TM, LANES), tab),
        pl.BlockSpec((D_MODEL, IN_WIDTH), lambda i: (0, 0)),
    ]
    wide = pl.BlockSpec((TM, QK_W), row)
    hm = pl.BlockSpec((N_HEADS, TM, LANES), lambda i: (0, i, 0))
    v_rows = pl.BlockSpec((TM * N_HEADS, DV), row)
    v_shape = jax.ShapeDtypeStruct((n * N_HEADS, DV), F32)
    wide_shape = jax.ShapeDtypeStruct((n, QK_W), F32)
    if head_major:
        n_seq = nt // tiles_per_seq
        k_t = pl.BlockSpec((1, 2 * N_HEADS, DQK, TM), lambda i: (i // tiles_per_seq, 0, 0, i % tiles_per_seq))
        out_specs = [hm, k_t, v_rows, hm, hm, wide]
        out_shape = [
            jax.ShapeDtypeStruct((N_HEADS, n, LANES), BF16),
            jax.ShapeDtypeStruct((n_seq, 2 * N_HEADS, DQK, tiles_per_seq * TM), F32),
            v_shape,
            jax.ShapeDtypeStruct((N_HEADS, n, LANES), BF16),
            jax.ShapeDtypeStruct((N_HEADS, n, LANES), BF16),
            wide_shape,
        ]
    else:
        out_specs = [wide, wide, v_rows, wide]
        out_shape = [wide_shape, wide_shape, v_shape, wide_shape]
    return pl.pallas_call(
        functools.partial(_inproj_kernel, head_major=head_major),
        grid=(nt,),
        in_specs=in_specs,
        out_specs=out_specs,
        out_shape=out_shape,
        compiler_params=_cparams(("parallel",)),
        name="inproj_hm" if head_major else "inproj",
    )(x, norm_g.reshape(1, D_MODEL), sc, sh, *tabs, w_in_bf16)


def _lambda_value(lq1, lk1, lq2, lk2):
    a = jnp.exp(jnp.sum(lq1[...] * lk1[...], axis=-1, keepdims=True))
    b = jnp.exp(jnp.sum(lq2[...] * lk2[...], axis=-1, keepdims=True))
    return a - b + LAM_INIT


def _diff_merge(o1, l1, o2, l2, lam, subln_g):
    o = o1 / l1 - lam * (o2 / l2)
    o = o * lax.rsqrt(jnp.mean(o * o, axis=-1, keepdims=True) + EPS) * subln_g
    return o * (1.0 - LAM_INIT)


def _attn_kernel(qt_ref, kt_ref, q_ref, k_ref, v_ref, lq1, lk1, lq2, lk2, sg_ref, o_ref, qs, m_s, l_s, acc):
    s_idx = pl.program_id(1)
    qi = qt_ref[s_idx]
    ki = kt_ref[s_idx]

    @pl.when(ki == 0)
    def _():
        for h in range(N_HEADS):
            q = q_ref[h]
            lane = lax.broadcasted_iota(I32, q.shape, 1)
            zero = jnp.zeros_like(q)
            qs[h, 0:TQ, :] = jnp.where(lane < DQK, q, zero)
            qs[h, TQ:2 * TQ, :] = jnp.where(lane >= DQK, q, zero)
        m_s[...] = jnp.full(m_s.shape, NEG_INF, F32)
        l_s[...] = jnp.zeros(l_s.shape, F32)
        acc[...] = jnp.zeros(acc.shape, F32)

    def update(h, masked):
        s = lax.dot_general(qs[h], k_ref[h], (((1,), (1,)), ((), ())), preferred_element_type=F32)
        if masked:
            row = lax.broadcasted_iota(I32, s.shape, 0) & (TQ - 1)
            col = lax.broadcasted_iota(I32, s.shape, 1)
            s = jnp.where(row >= col, s, NEG_INF)
        m_prev = m_s[h]
        m_next = jnp.maximum(m_prev, jnp.max(s, axis=1, keepdims=True))
        p = jnp.exp2(s - jnp.tile(m_next, (1, TK // LANES)))
        alpha = jnp.exp2(m_prev - m_next)
        v_ones = jnp.concatenate([v_ref[h], jnp.ones((TK, LANES), BF16)], axis=1)
        pv = jnp.dot(p.astype(BF16), v_ones, preferred_element_type=F32)
        l_s[h] = alpha * l_s[h] + pv[:, DV:]
        acc[h] = alpha * acc[h] + pv[:, :DV]
        m_s[h] = m_next

    @pl.when(ki < qi)
    def _():
        for h in range(N_HEADS):
            update(h, False)

    @pl.when(ki == qi)
    def _():
        lam = _lambda_value(lq1, lk1, lq2, lk2)
        for h in range(N_HEADS):
            update(h, True)
            o = _diff_merge(acc[h, 0:TQ, :], l_s[h, 0:TQ, :], acc[h, TQ:2 * TQ, :], l_s[h, TQ:2 * TQ, :],
                            lam, sg_ref[...])
            o_ref[:, h * DV:(h + 1) * DV] = o.astype(o_ref.dtype)


def _prompt_attention(q_hm, k_hm, v_hm, lams, subln_g, batch, seq):
    nq = seq // TQ
    pairs = [(qi, ki) for qi in range(nq) for ki in range(qi + 1)]
    qt = jnp.asarray([p[0] for p in pairs], I32)
    kt = jnp.asarray([p[1] for p in pairs], I32)
    n = batch * seq
    vec = lambda b, s, qt, kt: (0, 0)
    grid_spec = pltpu.PrefetchScalarGridSpec(
        num_scalar_prefetch=2,
        grid=(batch, len(pairs)),
        in_specs=[
            pl.BlockSpec((N_HEADS, TQ, LANES), lambda b, s, qt, kt: (0, b * nq + qt[s], 0)),
            pl.BlockSpec((N_HEADS, TK, LANES), lambda b, s, qt, kt: (0, b * nq + kt[s], 0)),
            pl.BlockSpec((N_HEADS, TK, LANES), lambda b, s, qt, kt: (0, b * nq + kt[s], 0)),
            pl.BlockSpec((1, DQK), vec), pl.BlockSpec((1, DQK), vec),
            pl.BlockSpec((1, DQK), vec), pl.BlockSpec((1, DQK), vec),
            pl.BlockSpec((1, DV), vec),
        ],
        out_specs=pl.BlockSpec((TQ, N_HEADS * DV), lambda b, s, qt, kt: (b * nq + qt[s], 0)),
        scratch_shapes=[
            pltpu.VMEM((N_HEADS, 2 * TQ, LANES), BF16),
            pltpu.VMEM((N_HEADS, 2 * TQ, LANES), F32),
            pltpu.VMEM((N_HEADS, 2 * TQ, LANES), F32),
            pltpu.VMEM((N_HEADS, 2 * TQ, LANES), F32),
        ],
    )
    return pl.pallas_call(
        _attn_kernel,
        grid_spec=grid_spec,
        out_shape=jax.ShapeDtypeStruct((n, N_HEADS * DV), BF16),
        compiler_params=_cparams(("parallel", "arbitrary")),
        name="prompt_attn",
    )(qt, kt, q_hm, k_hm, v_hm, *lams, subln_g.reshape(1, DV))


def _paged_attn_kernel(pt_ref, q_ref, kn_ref, vn_ref, lq1, lk1, lq2, lk2, sg_ref, *rest, n_steps, ts):
    kp = rest[:PAGES_PER_STEP]
    vp = rest[PAGES_PER_STEP:2 * PAGES_PER_STEP]
    o_ref, qe, m_s, l_s, acc = rest[2 * PAGES_PER_STEP:]
    n_maps = 2 * N_HEADS
    rows = n_maps * ts
    p_idx = pl.program_id(1)

    @pl.when(p_idx == 0)
    def _():
        qt = jnp.concatenate([q_ref[...]] * n_maps, axis=0)
        row = lax.broadcasted_iota(I32, qt.shape, 0)
        col = lax.broadcasted_iota(I32, qt.shape, 1)
        same_map = (row >> (ts.bit_length() - 1)) == (col >> (DQK.bit_length() - 1))
        qe[...] = jnp.where(same_map, qt, 0.0).astype(BF16)
        m_s[...] = jnp.full(m_s.shape, NEG_INF, F32)
        l_s[...] = jnp.zeros(l_s.shape, F32)
        acc[...] = jnp.zeros(acc.shape, F32)

    def update(kmat, vmat, causal, k_transposed):
        if k_transposed:
            s = jnp.dot(qe[...], kmat, preferred_element_type=F32)
        else:
            s = lax.dot_general(qe[...], kmat, (((1,), (1,)), ((), ())), preferred_element_type=F32)
        if causal:
            row = lax.broadcasted_iota(I32, s.shape, 0) & (ts - 1)
            col = lax.broadcasted_iota(I32, s.shape, 1)
            s = jnp.where(row >= col, s, NEG_INF)
        m_prev = m_s[...]
        m_next = jnp.maximum(m_prev, jnp.max(s, axis=1, keepdims=True))
        p = jnp.exp2(s - m_next[:, 0:1])
        alpha = jnp.exp2(m_prev - m_next)
        l_s[...] = alpha * l_s[...] + jnp.sum(p, axis=1, keepdims=True)
        acc[...] = jnp.tile(alpha, (1, V_W // LANES)) * acc[...] + jnp.dot(
            p.astype(BF16), vmat, preferred_element_type=F32)
        m_s[...] = m_next

    kmat = jnp.concatenate([r[...].reshape(QK_W, PAGE_SIZE) for r in kp], axis=1).astype(BF16)
    def heads_on_lanes(ref, n_pos):
        return jnp.concatenate([ref[pl.ds(h, n_pos, stride=N_HEADS), :] for h in range(N_HEADS)], axis=1)

    vmat = jnp.concatenate([heads_on_lanes(r, PAGE_SIZE) for r in vp], axis=0).astype(BF16)
    update(kmat, vmat, False, True)

    @pl.when(p_idx == n_steps - 1)
    def _():
        update(kn_ref[...].astype(BF16), heads_on_lanes(vn_ref, ts).astype(BF16), True, False)
        lam = _lambda_value(lq1, lk1, lq2, lk2)
        for h in range(N_HEADS):
            r1, r2 = 2 * h * ts, (2 * h + 1) * ts
            c0, c1 = h * DV, (h + 1) * DV
            o = _diff_merge(acc[r1:r1 + ts, c0:c1], l_s[r1:r1 + ts, :],
                            acc[r2:r2 + ts, c0:c1], l_s[r2:r2 + ts, :], lam, sg_ref[...])
            o_ref[:, c0:c1] = o


def _sample_attention(q_s, k_s, v_s, cache_k, cache_v, page_table, lams, subln_g):
    db, n_pages = page_table.shape
    ts = q_s.shape[0] // db
    n_pool = cache_k.shape[1]
    ck = jnp.transpose(cache_k, (0, 1, 3, 4, 2)).reshape(n_pool, 2 * N_HEADS, DQK, PAGE_SIZE)
    cv = cache_v.reshape(n_pool, PAGE_SIZE * N_HEADS, DV)
    n_steps = n_pages // PAGES_PER_STEP
    pt = page_table.reshape(-1).astype(I32)
    vec = lambda b, p, pt: (0, 0)
    new = lambda b, p, pt: (b, 0)

    def page_spec(j, block):
        def idx(b, p, pt):
            return (pt[b * n_pages + p * PAGES_PER_STEP + j],) + (0,) * (len(block) - 1)
        return pl.BlockSpec(block, idx)

    k_block = (None, 2 * N_HEADS, DQK, PAGE_SIZE)
    v_block = (None, PAGE_SIZE * N_HEADS, DV)

    rows = 2 * N_HEADS * ts
    grid_spec = pltpu.PrefetchScalarGridSpec(
        num_scalar_prefetch=1,
        grid=(db, n_steps),
        in_specs=[
            pl.BlockSpec((ts, QK_W), new), pl.BlockSpec((ts, QK_W), new), pl.BlockSpec((ts * N_HEADS, DV), new),
            pl.BlockSpec((1, DQK), vec), pl.BlockSpec((1, DQK), vec),
            pl.BlockSpec((1, DQK), vec), pl.BlockSpec((1, DQK), vec),
            pl.BlockSpec((1, DV), vec),
        ] + [page_spec(j, k_block) for j in range(PAGES_PER_STEP)]
          + [page_spec(j, v_block) for j in range(PAGES_PER_STEP)],
        out_specs=pl.BlockSpec((ts, V_W), new),
        scratch_shapes=[
            pltpu.VMEM((rows, QK_W), BF16),
            pltpu.VMEM((rows, LANES), F32),
            pltpu.VMEM((rows, LANES), F32),
            pltpu.VMEM((rows, V_W), F32),
        ],
    )
    return pl.pallas_call(
        functools.partial(_paged_attn_kernel, n_steps=n_steps, ts=ts),
        grid_spec=grid_spec,
        out_shape=jax.ShapeDtypeStruct((db * ts, V_W), F32),
        compiler_params=_cparams(("parallel", "arbitrary")),
        name="paged_attn",
    )(pt, q_s, k_s, v_s, *lams, subln_g.reshape(1, DV), *([ck] * PAGES_PER_STEP), *([cv] * PAGES_PER_STEP))


ROW_CHUNKS = D_MODEL // LANES


def _store_token_rows(ref, x):
    t = x.shape[0]
    for j in range(ROW_CHUNKS):
        ref[pl.ds(j, t, stride=ROW_CHUNKS), :] = x[:, j * LANES:(j + 1) * LANES]


def _load_token_rows(ref, t, lead=()):
    return jnp.concatenate(
        [ref[lead + (pl.ds(j, t, stride=ROW_CHUNKS), slice(None))] for j in range(ROW_CHUNKS)], axis=1)


def _split_bf16(x):
    hi = x.astype(BF16)
    return hi, (x - hi.astype(F32)).astype(BF16)


def _conv_ln_swish(y, lng, lnb):
    mu = jnp.mean(y, axis=-1, keepdims=True)
    var = jnp.mean(jnp.square(y - mu), axis=-1, keepdims=True)
    yn = (y - mu) * lax.rsqrt(var + EPS) * lng + lnb
    return yn * jax.nn.sigmoid(yn)


N_MIX_PROMPT_INPUTS = 16


def _mix_kernel(*refs, prompt, tiles_per_seq, n_real):
    i = pl.program_id(0)

    @pl.when(i < n_real)
    def _():
        _mix_body(*refs, prompt=prompt, tiles_per_seq=tiles_per_seq)

    if prompt:
        @pl.when(i == n_real)
        def _():
            for ref in refs[N_MIX_PROMPT_INPUTS:N_MIX_PROMPT_INPUTS + 3]:
                ref[...] = jnp.zeros(ref.shape, ref.dtype)


def _mix_body(*refs, prompt, tiles_per_seq):
    if prompt:
        (o_ref, ucur_ref, uhalo_ref, x_ref, g1_ref, sc2_ref, sh2_ref, wdw_ref, bdw_ref, lng_ref, lnb_ref,
         wout_ref, n2g_ref, wr_ref, br_ref, basein_ref,
         hp1_ref, n2_ref, meta_ref, cnt_ref, ext, shifted, base) = refs
    else:
        (o_ref, uext_ref, x_ref, g1_ref, sc2_ref, sh2_ref, wdw_ref, bdw_ref, lng_ref, lnb_ref,
         wout_ref, n2g_ref, wr_ref, br_ref, basein_ref, hp1_in, n2_in, meta_in,
         hp1_ref, n2_ref, meta_ref, cnt_ref, base) = refs
    i = pl.program_id(0)

    @pl.when(i == 0)
    def _():
        base[...] = basein_ref[0:1, :]

    if prompt:
        first = (i % tiles_per_seq) == 0
        halo = uhalo_ref[...]
        ext[0:HALO, :] = jnp.where(first, jnp.zeros_like(halo), halo)
        ext[HALO:HALO + TM, :] = ucur_ref[...]
        off = HALO - (CONV_K - 1)
        chunks = []
        for phase in range(SUBLANES):
            q_max = max([(off + j) // SUBLANES for j in range(CONV_K) if (off + j) % SUBLANES == phase])
            span = TM + SUBLANES * q_max
            shifted[phase, 0:span, :] = ext[pl.ds(phase, span), :]
        for c in range(TM // CONV_CHUNK):
            a = jnp.zeros((CONV_CHUNK, CONV_CH), F32) + bdw_ref[...]
            for j in range(CONV_K):
                phase, q = (off + j) % SUBLANES, (off + j) // SUBLANES
                start = c * CONV_CHUNK + SUBLANES * q
                a = a + wdw_ref[j:j + 1, :] * shifted[phase, start:start + CONV_CHUNK, :]
            chunks.append(a)
        y = jnp.concatenate(chunks, axis=0)
    else:
        nb, text, _ = uext_ref.shape
        ts = text - (CONV_K - 1)
        a = jnp.zeros((nb, ts, CONV_CH), F32) + bdw_ref[...]
        for j in range(CONV_K):
            a = a + wdw_ref[j:j + 1, :] * uext_ref[:, j:j + ts, :]
        y = a.reshape(nb * ts, CONV_CH)
    yc = _conv_ln_swish(y, lng_ref[...], lnb_ref[...])

    proj = (jnp.dot(o_ref[...].astype(BF16), wout_ref[0:N_HEADS * DV, :], preferred_element_type=F32)
            + jnp.dot(yc.astype(BF16), wout_ref[N_HEADS * DV:, :], preferred_element_type=F32))
    hp1 = x_ref[...] + g1_ref[0] * proj
    hp1_ref[...] = hp1
    n2 = hp1 * lax.rsqrt(jnp.mean(hp1 * hp1, axis=-1, keepdims=True) + EPS) * n2g_ref[...]
    n2 = n2 * (1.0 + sc2_ref[0]) + sh2_ref[0]
    _store_token_rows(n2_ref, n2)

    n_hi, n_lo = _split_bf16(n2)
    w_hi, w_lo = _split_bf16(wr_ref[...])
    logits = (jnp.dot(n_hi, w_hi, preferred_element_type=F32)
              + (jnp.dot(n_hi, w_lo, preferred_element_type=F32) + jnp.dot(n_lo, w_hi, preferred_element_type=F32))
              + br_ref[...])
    lane = lax.broadcasted_iota(I32, logits.shape, 1)
    lane_f = lane.astype(F32)
    lg = jnp.where(lane < N_EXPERTS, logits, NEG_INF)
    onehots, vals, idxs = [], [], []
    for _ in range(TOP_K):
        mx = jnp.max(lg, axis=-1, keepdims=True)
        idx = jnp.min(jnp.where(lg == mx, lane_f, float(LANES)), axis=-1, keepdims=True)
        oh = lane_f == idx
        lg = jnp.where(oh, NEG_INF, lg)
        onehots.append(oh)
        vals.append(mx)
        idxs.append(idx)
    exps = [jnp.exp(v - vals[0]) for v in vals]
    denom = exps[0] + exps[1] + exps[2] + exps[3]

    sel = jnp.zeros(logits.shape, F32)
    for oh in onehots:
        sel = sel + oh.astype(F32)
    r_i = lax.broadcasted_iota(I32, (TM, TM), 0)
    c_i = lax.broadcasted_iota(I32, (TM, TM), 1)
    ltri = (r_i > c_i).astype(BF16)
    before = jnp.dot(ltri, sel.astype(BF16), preferred_element_type=F32) + base[...]
    meta = jnp.zeros(logits.shape, F32)
    for k in range(TOP_K):
        rank = jnp.sum(jnp.where(onehots[k], before, 0.0), axis=-1, keepdims=True)
        meta = meta + jnp.where(lane == k, idxs[k], 0.0)
        meta = meta + jnp.where(lane == TOP_K + k, exps[k] / denom, 0.0)
        meta = meta + jnp.where(lane == 2 * TOP_K + k, rank, 0.0)
    meta_ref[...] = meta
    new_base = base[...] + jnp.sum(sel, axis=0, keepdims=True)
    base[...] = new_base
    cnt_ref[...] = jnp.broadcast_to(new_base, cnt_ref.shape)


def _mix_prompt(o_attn, u, x, g1, sc2, sh2, conv_w, w_out_bf16, norm2_g, w_router_pad, b_router_pad,
                base_in, n_total, tiles_per_seq):
    n = x.shape[0]
    nt = n // TM
    assert n_total == n + TM
    row = lambda i: (jnp.minimum(i, nt - 1), 0)
    out_row = lambda i: (i, 0)
    const2 = lambda i: (0, 0)
    grp = lambda i: (jnp.minimum(i, nt - 1) // tiles_per_seq, 0, 0)
    halo_row = lambda i: (jnp.maximum(jnp.minimum(i, nt - 1) * (TM // HALO) - 1, 0), 0)
    wdw, bdw, lng, lnb = conv_w
    in_specs = [
        pl.BlockSpec((TM, N_HEADS * DV), row),
        pl.BlockSpec((TM, CONV_CH), row),
        pl.BlockSpec((HALO, CONV_CH), halo_row),
        pl.BlockSpec((TM, D_MODEL), row),
        pl.BlockSpec((1, 1, D_MODEL), grp), pl.BlockSpec((1, 1, D_MODEL), grp), pl.BlockSpec((1, 1, D_MODEL), grp),
        pl.BlockSpec((CONV_K, CONV_CH), const2), pl.BlockSpec((1, CONV_CH), const2),
        pl.BlockSpec((1, CONV_CH), const2), pl.BlockSpec((1, CONV_CH), const2),
        pl.BlockSpec((D_MODEL, D_MODEL), const2),
        pl.BlockSpec((1, D_MODEL), const2),
        pl.BlockSpec((D_MODEL, LANES), const2), pl.BlockSpec((1, LANES), const2),
        pl.BlockSpec((SUBLANES, LANES), const2),
    ]
    assert len(in_specs) == N_MIX_PROMPT_INPUTS
    out_specs = [
        pl.BlockSpec((TM, D_MODEL), out_row),
        pl.BlockSpec((TM * ROW_CHUNKS, LANES), out_row),
        pl.BlockSpec((TM, LANES), out_row),
        pl.BlockSpec((SUBLANES, LANES), const2),
    ]
    out_shape = [
        jax.ShapeDtypeStruct((n_total, D_MODEL), F32),
        jax.ShapeDtypeStruct((n_total * ROW_CHUNKS, LANES), F32),
        jax.ShapeDtypeStruct((n_total, LANES), F32),
        jax.ShapeDtypeStruct((SUBLANES, LANES), F32),
    ]
    return pl.pallas_call(
        functools.partial(_mix_kernel, prompt=True, tiles_per_seq=tiles_per_seq, n_real=nt),
        grid=(nt + 1,),
        in_specs=in_specs,
        out_specs=out_specs,
        out_shape=out_shape,
        scratch_shapes=[
            pltpu.VMEM((HALO + TM, CONV_CH), F32),
            pltpu.VMEM((SUBLANES, HALO + TM, CONV_CH), F32),
            pltpu.VMEM((1, LANES), F32),
        ],
        compiler_params=_cparams(("arbitrary",)),
        name="mix_prompt",
    )(o_attn, u, u, x, g1, sc2, sh2, wdw, bdw.reshape(1, -1), lng.reshape(1, -1), lnb.reshape(1, -1),
      w_out_bf16, norm2_g.reshape(1, -1), w_router_pad, b_router_pad, base_in)


def _mix_sample(o_attn, uext, x, g1, sc2, sh2, conv_w, w_out_bf16, norm2_g, w_router_pad, b_router_pad,
                base_in, hp1_all, n2_all, meta_all, tile0):
    const2 = lambda i: (0, 0)
    const3 = lambda i: (0, 0, 0)
    out_row = lambda i: (tile0, 0)
    wdw, bdw, lng, lnb = conv_w
    nb, text, _ = uext.shape
    in_specs = [
        pl.BlockSpec((TM, N_HEADS * DV), const2),
        pl.BlockSpec((nb, text, CONV_CH), const3),
        pl.BlockSpec((TM, D_MODEL), const2),
        pl.BlockSpec((1, TM, D_MODEL), const3), pl.BlockSpec((1, TM, D_MODEL), const3),
        pl.BlockSpec((1, TM, D_MODEL), const3),
        pl.BlockSpec((CONV_K, CONV_CH), const2), pl.BlockSpec((1, CONV_CH), const2),
        pl.BlockSpec((1, CONV_CH), const2), pl.BlockSpec((1, CONV_CH), const2),
        pl.BlockSpec((D_MODEL, D_MODEL), const2),
        pl.BlockSpec((1, D_MODEL), const2),
        pl.BlockSpec((D_MODEL, LANES), const2), pl.BlockSpec((1, LANES), const2),
        pl.BlockSpec((SUBLANES, LANES), const2),
        pl.BlockSpec(memory_space=pl.ANY), pl.BlockSpec(memory_space=pl.ANY), pl.BlockSpec(memory_space=pl.ANY),
    ]
    out_specs = [
        pl.BlockSpec((TM, D_MODEL), out_row),
        pl.BlockSpec((TM * ROW_CHUNKS, LANES), out_row),
        pl.BlockSpec((TM, LANES), out_row),
        pl.BlockSpec((SUBLANES, LANES), const2),
    ]
    out_shape = [
        jax.ShapeDtypeStruct(hp1_all.shape, F32),
        jax.ShapeDtypeStruct(n2_all.shape, F32),
        jax.ShapeDtypeStruct(meta_all.shape, F32),
        jax.ShapeDtypeStruct((SUBLANES, LANES), F32),
    ]
    return pl.pallas_call(
        functools.partial(_mix_kernel, prompt=False, tiles_per_seq=1, n_real=1),
        grid=(1,),
        in_specs=in_specs,
        out_specs=out_specs,
        out_shape=out_shape,
        scratch_shapes=[pltpu.VMEM((1, LANES), F32)],
        input_output_aliases={15: 0, 16: 1, 17: 2},
        compiler_params=_cparams(("arbitrary",)),
        name="mix_sample",
    )(o_attn, uext, x, g1, sc2, sh2, wdw, bdw.reshape(1, -1), lng.reshape(1, -1), lnb.reshape(1, -1),
      w_out_bf16, norm2_g.reshape(1, -1), w_router_pad, b_router_pad, base_in, hp1_all, n2_all, meta_all)


MAX_SLOT_TILES = 16


def _lane_cumsum(x, lane):
    s = 1
    while s < LANES:
        x = x + jnp.where(lane >= s, pltpu.roll(x, s, 1), 0)
        s *= 2
    return x


def _slots_kernel(meta_ref, cnt_ref, slots_ref, tmap_ref, einfo_ref, *, n_tiles_pad):
    shift = TM.bit_length() - 1
    lane8 = lax.broadcasted_iota(I32, (SUBLANES, LANES), 1)
    cnt = cnt_ref[...].astype(I32)
    padded = ((cnt + (TM - 1)) >> shift) << shift
    csum = _lane_cumsum(padded, lane8)
    gstart = csum - padded

    meta = meta_ref[...]
    lane = lax.broadcasted_iota(I32, meta.shape, 1)
    lane_f = lane.astype(F32)
    gstart_f = gstart[0:1, :].astype(F32)
    out = jnp.zeros(meta.shape, F32)
    for k in range(TOP_K):
        sel = lane_f == meta[:, k:k + 1]
        gs = jnp.sum(jnp.where(sel, gstart_f, 0.0), axis=-1, keepdims=True)
        out = out + jnp.where(lane == k, gs + meta[:, 2 * TOP_K + k: 2 * TOP_K + k + 1], 0.0)
    slots_ref[...] = out.astype(I32)

    @pl.when(pl.program_id(0) == 0)
    def _():
        ctiles = csum[0:1, :] >> shift
        n_valid = jnp.max(ctiles, axis=-1, keepdims=True)
        t = lax.broadcasted_iota(I32, (n_tiles_pad, LANES), 0)
        t = jnp.minimum(t, n_valid - 1)
        lane_t = lax.broadcasted_iota(I32, (n_tiles_pad, LANES), 1)
        hit = jnp.where((lane_t < N_EXPERTS) & (ctiles <= t), 1, 0)
        te = jnp.sum(hit, axis=-1, keepdims=True)
        tmap_ref[...] = jnp.broadcast_to(jnp.minimum(te, N_EXPERTS - 1), tmap_ref.shape)
        row8 = lax.broadcasted_iota(I32, (SUBLANES, LANES), 0)
        info = jnp.where(row8 == 0, gstart, 0)
        info = info + jnp.where(row8 == 1, padded, 0)
        info = info + jnp.where(row8 == 2, cnt, 0)
        info = info + jnp.where(row8 == 3, jnp.broadcast_to(n_valid, (SUBLANES, LANES)), 0)
        einfo_ref[...] = info


def _routing_slots(meta_all, counts, n_tiles_pad):
    n = meta_all.shape[0]
    const2 = lambda i: (0, 0)
    n_tm = n // TM
    group = max(g for g in range(1, MAX_SLOT_TILES + 1) if n_tm % g == 0)
    rows = group * TM
    return pl.pallas_call(
        functools.partial(_slots_kernel, n_tiles_pad=n_tiles_pad),
        grid=(n // rows,),
        in_specs=[pl.BlockSpec((rows, LANES), lambda i: (i, 0)), pl.BlockSpec((SUBLANES, LANES), const2)],
        out_specs=[
            pl.BlockSpec((rows, LANES), lambda i: (i, 0)),
            pl.BlockSpec((n_tiles_pad, LANES), const2),
            pl.BlockSpec((SUBLANES, LANES), const2),
        ],
        out_shape=[
            jax.ShapeDtypeStruct((n, LANES), I32),
            jax.ShapeDtypeStruct((n_tiles_pad, LANES), I32),
            jax.ShapeDtypeStruct((SUBLANES, LANES), I32),
        ],
        compiler_params=_cparams(("arbitrary",)),
        name="routing_slots",
    )(meta_all, counts)


def _token_rows(ref, row):
    return ref.at[pl.ds(pl.multiple_of(row * ROW_CHUNKS, ROW_CHUNKS), ROW_CHUNKS)]


def _dispatch_kernel(gstart_ref, padded_ref, cnt_ref, nv_ref, slots_hbm, src_ref, xs_hbm,
                     idx0, idx1, zbuf, sem, zsem, isems):
    i = pl.program_id(0)
    tile_rows = TM * ROW_CHUNKS
    n_slot_tiles = xs_hbm.shape[0] // tile_rows

    def zero_tile_copy(tile):
        start = pl.multiple_of(tile * tile_rows, tile_rows)
        return pltpu.make_async_copy(zbuf, xs_hbm.at[pl.ds(start, tile_rows)], zsem)

    def pad_tile_copy(e):
        return zero_tile_copy((gstart_ref[e] + padded_ref[e]) // TM - 1)

    @pl.when(i == 0)
    def _():
        zbuf[...] = jnp.zeros(zbuf.shape, zbuf.dtype)
        for e in range(N_EXPERTS):
            @pl.when(cnt_ref[e] > 0)
            def _():
                pad_tile_copy(e).start()

        def start_unused(t, carry):
            zero_tile_copy(t).start()
            return carry

        def wait_unused(t, carry):
            zero_tile_copy(t).wait()
            return carry

        lax.fori_loop(nv_ref[0], n_slot_tiles, start_unused, 0)
        for e in range(N_EXPERTS):
            @pl.when(cnt_ref[e] > 0)
            def _():
                pad_tile_copy(e).wait()
        lax.fori_loop(nv_ref[0], n_slot_tiles, wait_unused, 0)

    n_steps = pl.num_programs(0)
    idxs = (idx0, idx1)

    def idx_copy(step, parity):
        return pltpu.make_async_copy(slots_hbm.at[step], idxs[parity], isems.at[parity])

    def for_each_row(parity, fn):
        def body(r, carry):
            for k in range(TOP_K):
                fn(pltpu.make_async_copy(_token_rows(src_ref, r),
                                         _token_rows(xs_hbm, idxs[parity][0, r * TOP_K + k]), sem), k)
            return carry
        lax.fori_loop(0, TM, body, 0, unroll=8)

    @pl.when(i == 0)
    def _():
        idx_copy(0, 0).start()

    def step(parity):
        @pl.when(i + 1 < n_steps)
        def _():
            idx_copy(i + 1, 1 - parity).start()

        idx_copy(i, parity).wait()
        for_each_row(parity, lambda c, k: c.start(priority=k % N_DMA_PRIORITIES))
        for_each_row(parity, lambda c, k: c.wait())

    for parity in range(2):
        @pl.when(i % 2 == parity)
        def _():
            step(parity)


def _dispatch(gstart, padded, cnt, n_valid, slots2d, n2_all, s_max):
    nt = slots2d.shape[0]
    grid_spec = pltpu.PrefetchScalarGridSpec(
        num_scalar_prefetch=4,
        grid=(nt,),
        in_specs=[
            pl.BlockSpec(memory_space=pl.ANY),
            pl.BlockSpec((TM * ROW_CHUNKS, LANES), lambda i, *_: (i, 0)),
        ],
        out_specs=pl.BlockSpec(memory_space=pl.ANY),
        scratch_shapes=[
            pltpu.SMEM((1, TM * TOP_K), I32),
            pltpu.SMEM((1, TM * TOP_K), I32),
            pltpu.VMEM((TM * ROW_CHUNKS, LANES), F32),
            pltpu.SemaphoreType.DMA,
            pltpu.SemaphoreType.DMA,
            pltpu.SemaphoreType.DMA((2,)),
        ],
    )
    return pl.pallas_call(
        _dispatch_kernel,
        grid_spec=grid_spec,
        out_shape=jax.ShapeDtypeStruct((s_max * ROW_CHUNKS, LANES), F32),
        compiler_params=pltpu.CompilerParams(dimension_semantics=("arbitrary",), has_side_effects=True),
        name="moe_dispatch",
    )(gstart, padded, cnt, n_valid, slots2d, n2_all)


DEINT_BLOCK = 2 * LANES
EXPERT_VMEM_LIMIT = 56 * 1024 * 1024


def _expert_kernel(te_ref, nv_ref, x_ref, wgu_hbm, wd_hbm, b1g_ref, b1u_ref, b2_ref, y_ref,
                   wgu_buf, wd_buf, w1g, w1u, w2, wsem, slot_ref):
    i = pl.program_id(0)
    n_valid = nv_ref[0]
    valid = i < n_valid
    expert = te_ref[i]
    new_expert = jnp.logical_or(i == 0, expert != te_ref[jnp.maximum(i - 1, 0)])
    last_tile = te_ref.shape[0] - 1

    def weight_copies(e, s):
        return (pltpu.make_async_copy(wgu_hbm.at[e], wgu_buf.at[s], wsem.at[0, s]),
                pltpu.make_async_copy(wd_hbm.at[e], wd_buf.at[s], wsem.at[1, s]))

    @pl.when(i == 0)
    def _():
        slot_ref[0] = 0
        for c in weight_copies(expert, 0):
            c.start()

    @pl.when(jnp.logical_and(valid, new_expert))
    def _():
        s = slot_ref[0]
        nxt = lax.while_loop(
            lambda j: jnp.logical_and(j < n_valid, te_ref[jnp.minimum(j, last_tile)] == expert),
            lambda j: j + 1, i + 1)

        @pl.when(nxt < n_valid)
        def _():
            for c in weight_copies(te_ref[jnp.minimum(nxt, last_tile)], 1 - s):
                c.start()

        for c in weight_copies(expert, s):
            c.wait()
        r = lax.broadcasted_iota(I32, (DEINT_BLOCK, DEINT_BLOCK), 0)
        c = lax.broadcasted_iota(I32, (DEINT_BLOCK, DEINT_BLOCK), 1)
        src_col = jnp.where(c < LANES, 2 * c, 2 * (c - LANES) + 1)
        perm = jnp.where(r == src_col, 1.0, 0.0).astype(BF16)
        for blk in range(wgu_buf.shape[2] // DEINT_BLOCK):
            cols = wgu_buf[s, :, blk * DEINT_BLOCK:(blk + 1) * DEINT_BLOCK].astype(BF16)
            split = jnp.dot(cols, perm, preferred_element_type=F32).astype(BF16)
            w1g[:, blk * LANES:(blk + 1) * LANES] = split[:, :LANES]
            w1u[:, blk * LANES:(blk + 1) * LANES] = split[:, LANES:]
        w2[...] = wd_buf[s].astype(BF16)
        slot_ref[0] = 1 - s

    @pl.when(valid)
    def _():
        x = _load_token_rows(x_ref, TM).astype(BF16)
        hg = jnp.dot(x, w1g[...], preferred_element_type=F32) + b1g_ref[0]
        hu = jnp.dot(x, w1u[...], preferred_element_type=F32) + b1u_ref[0]
        gate = jnp.minimum(hg, SWIGLU_LIMIT)
        up = jnp.clip(hu, -SWIGLU_LIMIT, SWIGLU_LIMIT)
        act = (up + 1.0) * gate * jax.nn.sigmoid(SWIGLU_ALPHA * gate)
        y = jnp.dot(act.astype(BF16), w2[...], preferred_element_type=F32) + b2_ref[0]
        _store_token_rows(y_ref, y)

    @pl.when(jnp.logical_not(valid))
    def _():
        y_ref[...] = jnp.zeros(y_ref.shape, y_ref.dtype)


def _expert_mlp(tile_expert, n_valid, xs, w_gate_up, w_down, b1g, b1u, b2):
    tile_rows = TM * ROW_CHUNKS
    nt = xs.shape[0] // tile_rows
    ff = w_down.shape[1]
    tile = lambda i, te, nv: (jnp.minimum(i, nv[0] - 1), 0)
    wsel = lambda i, te, nv: (te[i], 0, 0)
    grid_spec = pltpu.PrefetchScalarGridSpec(
        num_scalar_prefetch=2,
        grid=(nt,),
        in_specs=[
            pl.BlockSpec((tile_rows, LANES), tile),
            pl.BlockSpec(memory_space=pl.ANY),
            pl.BlockSpec(memory_space=pl.ANY),
            pl.BlockSpec((1, 1, ff), wsel),
            pl.BlockSpec((1, 1, ff), wsel),
            pl.BlockSpec((1, 1, D_MODEL), wsel),
        ],
        out_specs=pl.BlockSpec((tile_rows, LANES), lambda i, te, nv: (i, 0)),
        scratch_shapes=[
            pltpu.VMEM((2, D_MODEL, 2 * ff), F32),
            pltpu.VMEM((2, ff, D_MODEL), F32),
            pltpu.VMEM((D_MODEL, ff), BF16),
            pltpu.VMEM((D_MODEL, ff), BF16),
            pltpu.VMEM((ff, D_MODEL), BF16),
            pltpu.SemaphoreType.DMA((2, 2)),
            pltpu.SMEM((1,), I32),
        ],
    )
    return pl.pallas_call(
        _expert_kernel,
        grid_spec=grid_spec,
        out_shape=jax.ShapeDtypeStruct(xs.shape, F32),
        compiler_params=pltpu.CompilerParams(dimension_semantics=("arbitrary",),
                                             vmem_limit_bytes=EXPERT_VMEM_LIMIT),
        name="expert_mlp",
    )(tile_expert, n_valid, xs, w_gate_up, w_down, b1g, b1u, b2)


def _combine_kernel(slots_hbm, ys_hbm, hp1_ref, meta_ref, g2_ref, nf_ref, o_ref,
                    idx0, idx1, buf0, buf1, sems, isems, *, tile0):
    i = pl.program_id(0)
    n_steps = pl.num_programs(0)
    bufs = (buf0, buf1)
    idxs = (idx0, idx1)

    def idx_copy(step, parity):
        return pltpu.make_async_copy(slots_hbm.at[tile0 + step], idxs[parity], isems.at[parity])

    def for_each_row(parity, fn):
        def body(r, carry):
            for k in range(TOP_K):
                fn(pltpu.make_async_copy(_token_rows(ys_hbm, idxs[parity][0, r * TOP_K + k]),
                                         _token_rows(bufs[parity].at[k], r), sems.at[parity]), k)
            return carry
        lax.fori_loop(0, TM, body, 0, unroll=8)

    def start_rows(parity):
        for_each_row(parity, lambda c, k: c.start(priority=k % N_DMA_PRIORITIES))

    @pl.when(i == 0)
    def _():
        first = idx_copy(0, 0)
        first.start()
        first.wait()
        start_rows(0)

        @pl.when(n_steps > 1)
        def _():
            idx_copy(1, 1).start()

    def step(parity):
        @pl.when(i + 1 < n_steps)
        def _():
            idx_copy(i + 1, 1 - parity).wait()
            start_rows(1 - parity)

        for_each_row(parity, lambda c, k: c.wait())

        @pl.when(i + 2 < n_steps)
        def _():
            idx_copy(i + 2, parity).start()

        meta = meta_ref[...]
        f = jnp.zeros((TM, D_MODEL), F32)
        for k in range(TOP_K):
            f = f + meta[:, TOP_K + k: TOP_K + k + 1] * _load_token_rows(bufs[parity], TM, (k,))
        hp2 = hp1_ref[...] + g2_ref[0] * f
        o_ref[...] = hp2 * lax.rsqrt(jnp.mean(hp2 * hp2, axis=-1, keepdims=True) + EPS) * nf_ref[...]

    for parity in range(2):
        @pl.when(i % 2 == parity)
        def _():
            step(parity)


def _combine(slots2d, ys, hp1_all, meta_all, g2, normf_g, tile0, n_rows, tiles_per_group):
    nt = n_rows // TM
    r = g2.shape[1]
    row_in = lambda i: (tile0 + i, 0)
    return pl.pallas_call(
        functools.partial(_combine_kernel, tile0=tile0),
        grid=(nt,),
        in_specs=[
            pl.BlockSpec(memory_space=pl.ANY),
            pl.BlockSpec(memory_space=pl.ANY),
            pl.BlockSpec((TM, D_MODEL), row_in),
            pl.BlockSpec((TM, LANES), row_in),
            pl.BlockSpec((1, r, D_MODEL), lambda i: (i // tiles_per_group, 0, 0)),
            pl.BlockSpec((1, D_MODEL), lambda i: (0, 0)),
        ],
        out_specs=pl.BlockSpec((TM, D_MODEL), lambda i: (i, 0)),
        out_shape=jax.ShapeDtypeStruct((n_rows, D_MODEL), F32),
        scratch_shapes=[
            pltpu.SMEM((1, TM * TOP_K), I32),
            pltpu.SMEM((1, TM * TOP_K), I32),
            pltpu.VMEM((TOP_K, TM * ROW_CHUNKS, LANES), F32),
            pltpu.VMEM((TOP_K, TM * ROW_CHUNKS, LANES), F32),
            pltpu.SemaphoreType.DMA((2,)),
            pltpu.SemaphoreType.DMA((2,)),
        ],
        compiler_params=_cparams(("arbitrary",)),
        name="moe_combine",
    )(slots2d, ys, hp1_all, meta_all, g2, normf_g.reshape(1, D_MODEL))


def kernel(x_prompt, x_sample, cache_k, cache_v, state_conv, page_table, c_prompt, c_sample, norm1_g, norm2_g, w_ada, b_ada, w_in, lambda_q1, lambda_k1, lambda_q2, lambda_k2, subln_g, w_dw, b_dw, conv_ln_g, conv_ln_b, w_out, w_router, b_router, w_gate_up, b_gate_up, w_down, b_down, normf_g):
    depth = norm1_g.shape[0]
    assert depth == 1, "single-layer step"
    bsz, seq, d = x_prompt.shape
    db, ts, _ = x_sample.shape
    n_p, n_s = bsz * seq, db * ts
    assert d == D_MODEL and n_s == TM and seq % TQ == 0 and n_p % TM == 0
    n_all = n_p + n_s
    n_pages = page_table.shape[1]
    past = n_pages * cache_k.shape[2]
    tiles_per_seq = seq // TM
    l = 0

    n_cond = bsz + db
    c_all = jnp.concatenate([c_prompt, c_sample], axis=0)
    c_all = jnp.pad(c_all, ((0, -n_cond % SUBLANES), (0, 0)))
    mod = _adaln(c_all, w_ada[l], b_ada[l])[:n_cond]
    mod_p = mod[:bsz].reshape(bsz, 1, 6, D_MODEL)
    mod_s = mod[bsz:].reshape(db, 6, D_MODEL)
    sh1p, sc1p, g1p, sh2p, sc2p, g2p = [mod_p[:, :, j] for j in range(6)]
    sh1s, sc1s, g1s, sh2s, sc2s, g2s = [
        jnp.repeat(mod_s[:, j], ts, axis=0).reshape(1, n_s, D_MODEL) for j in range(6)]

    lams = [v[l].reshape(1, DQK) for v in (lambda_q1, lambda_k1, lambda_q2, lambda_k2)]
    w_in_b = w_in[l].astype(BF16)
    w_out_b = w_out[l].astype(BF16)
    conv_w = (w_dw[l], b_dw[l], conv_ln_g[l], conv_ln_b[l])
    w_router_pad = jnp.pad(w_router[l], ((0, 0), (0, LANES - N_EXPERTS)))
    b_router_pad = jnp.pad(b_router[l], (0, LANES - N_EXPERTS)).reshape(1, LANES)

    xp = x_prompt.reshape(n_p, D_MODEL)
    xs_tok = x_sample.reshape(n_s, D_MODEL)
    tabs_p = _rope_tables(jnp.arange(seq))
    tabs_s = _rope_tables(jnp.tile(past + jnp.arange(ts), db))
    q_hm, k_p, v_p, k_hm, v_hm, u_p = _inproj(xp, norm1_g[l], sc1p, sh1p, tabs_p, w_in_b,
                                              tiles_per_seq, tiles_per_seq, True)
    q_s, k_s, v_s, u_s = _inproj(xs_tok, norm1_g[l], sc1s, sh1s, tabs_s, w_in_b, 1, 1, False)

    o_p = _prompt_attention(q_hm, k_hm, v_hm, lams, subln_g[l], bsz, seq)
    o_s = _sample_attention(q_s, k_s, v_s, cache_k, cache_v, page_table, lams, subln_g[l])

    uext_s = jnp.concatenate([state_conv[l], u_s.reshape(db, ts, CONV_CH)], axis=1)

    zeros_base = jnp.zeros((SUBLANES, LANES), F32)
    hp1_all, n2_all, meta_all, cnt_p = _mix_prompt(
        o_p, u_p, xp, g1p, sc2p, sh2p, conv_w, w_out_b, norm2_g[l], w_router_pad, b_router_pad,
        zeros_base, n_all, tiles_per_seq)
    hp1_all, n2_all, meta_all, counts = _mix_sample(
        o_s, uext_s, xs_tok, g1s, sc2s, sh2s, conv_w, w_out_b, norm2_g[l], w_router_pad, b_router_pad,
        cnt_p, hp1_all, n2_all, meta_all, n_p // TM)

    n_tok_tiles = n_all // TM
    s_max = n_all * TOP_K + N_EXPERTS * TM
    n_slot_tiles = s_max // TM
    n_tiles_pad = -(-n_slot_tiles // SUBLANES) * SUBLANES
    slots, tmap, einfo = _routing_slots(meta_all, counts, n_tiles_pad)
    slots2d = slots[:, :TOP_K].reshape(n_tok_tiles, 1, TM * TOP_K)
    tile_expert = tmap[:n_slot_tiles, 0]
    gstart, padded, cnt = einfo[0, :N_EXPERTS], einfo[1, :N_EXPERTS], einfo[2, :N_EXPERTS]
    n_valid = einfo[3, :1]

    x_sorted = _dispatch(gstart, padded, cnt, n_valid, slots2d, n2_all, s_max)

    bgu = b_gate_up[l]
    b1g = bgu[:, 0::2].reshape(N_EXPERTS, 1, -1)
    b1u = bgu[:, 1::2].reshape(N_EXPERTS, 1, -1)
    b2 = b_down[l].reshape(N_EXPERTS, 1, D_MODEL)
    y_sorted = _expert_mlp(tile_expert, n_valid, x_sorted, w_gate_up.reshape(w_gate_up.shape[1:]),
                           w_down.reshape(w_down.shape[1:]), b1g, b1u, b2)

    y_p = _combine(slots2d, y_sorted, hp1_all, meta_all, g2p, normf_g, 0, n_p, tiles_per_seq)
    y_s = _combine(slots2d, y_sorted, hp1_all, meta_all, g2s, normf_g, n_p // TM, n_s, 1)

    y_prompt = y_p.reshape(bsz, seq, D_MODEL)
    y_sample = y_s.reshape(db, ts, D_MODEL)
    k_prompt = jnp.transpose(k_p, (0, 3, 1, 2))[None]
    v_prompt = v_p.reshape(1, bsz, seq, N_HEADS, DV)
    conv_prompt = u_p.reshape(bsz, seq, CONV_CH)[:, seq - (CONV_K - 1):][None]
    k_sample = k_s.reshape(1, db, ts, 2 * N_HEADS, DQK)
    v_sample = v_s.reshape(1, db, ts, N_HEADS, DV)
    conv_sample = uext_s[:, ts:][None]
    return (y_prompt, y_sample, k_prompt, v_prompt, conv_prompt, k_sample, v_sample, conv_sample)
```

```python
import functools
import math

import jax
import jax.numpy as jnp
from jax import lax
from jax.experimental import pallas as pl
from jax.experimental.pallas import tpu as pltpu

F32 = jnp.float32
BF16 = jnp.bfloat16
I32 = jnp.int32
HIGHEST = lax.Precision.HIGHEST

D_MODEL = 1024
N_HEADS = 4
DV = 128
DQK = 64
ROT_DIM = 16
ROPE_THETA = 500000.0
CONV_K = 31
CONV_CH = 512
QK_W = 512
V_W = 512
IN_WIDTH = 2 * QK_W + V_W + 2 * CONV_CH
N_EXPERTS = 32
TOP_K = 4
SWIGLU_LIMIT = 7.0
SWIGLU_ALPHA = 1.702
EPS = 1e-5
LAM_INIT = 0.8 - 0.6 * math.exp(-0.3 * 0)
PAGE_SIZE = 128
Q_SCALE = DQK ** -0.5 * math.log2(math.e)

LANES = 128
SUBLANES = 8
VMEM_LIMIT = 48 * 1024 * 1024
N_DMA_PRIORITIES = 2

TM = 256
TQ = 512
TK = 512
PAGES_PER_STEP = 32
HALO = 32
CONV_CHUNK = 64
NEG_INF = float("-inf")


def _cparams(sem):
    return pltpu.CompilerParams(dimension_semantics=sem, vmem_limit_bytes=VMEM_LIMIT)


def _adaln_kernel(c_ref, w_ref, b_ref, o_ref):
    c = c_ref[...]
    s = c * jax.nn.sigmoid(c)
    o_ref[...] = jnp.dot(s, w_ref[...], precision=HIGHEST, preferred_element_type=F32) + b_ref[...]


def _adaln(c_all, w, b):
    n, d = c_all.shape
    width = w.shape[1]
    bn = 1536
    return pl.pallas_call(
        _adaln_kernel,
        grid=(width // bn,),
        in_specs=[
            pl.BlockSpec((n, d), lambda j: (0, 0)),
            pl.BlockSpec((d, bn), lambda j: (0, j)),
            pl.BlockSpec((1, bn), lambda j: (0, j)),
        ],
        out_specs=pl.BlockSpec((n, bn), lambda j: (0, j)),
        out_shape=jax.ShapeDtypeStruct((n, width), F32),
        compiler_params=_cparams(("parallel",)),
        name="adaln",
    )(c_all, w, b.reshape(1, width))


def _rope_tables(pos):
    inv = ROPE_THETA ** (-jnp.arange(0, ROT_DIM, 2, dtype=F32) / ROT_DIM)
    ang = pos.astype(F32)[:, None] * inv
    cos, sin = jnp.cos(ang), jnp.sin(ang)
    ones = jnp.ones((pos.shape[0], DQK - ROT_DIM), F32)
    zeros_r = jnp.zeros((pos.shape[0], DQK - ROT_DIM), F32)
    c64 = jnp.concatenate([cos, cos, ones], axis=1)
    s64 = jnp.concatenate([-sin, sin, zeros_r], axis=1)
    rep = LANES // DQK
    return jnp.tile(c64, (1, rep)), jnp.tile(s64, (1, rep))


def _inproj_kernel(x_ref, g_ref, sc_ref, sh_ref, cos_ref, sin_ref, w_ref, *out_refs, head_major):
    x = x_ref[...]
    hn = x * lax.rsqrt(jnp.mean(x * x, axis=-1, keepdims=True) + EPS) * g_ref[...]
    hn = hn * (1.0 + sc_ref[0]) + sh_ref[0]
    proj = jnp.dot(hn.astype(BF16), w_ref[...], preferred_element_type=F32)
    cos, sin = cos_ref[...], sin_ref[...]
    lane = lax.broadcasted_iota(I32, cos.shape, 1)
    first_half = (lane & (DQK - 1)) < ROT_DIM // 2

    def rope(blk):
        partner = jnp.where(first_half, pltpu.roll(blk, LANES - ROT_DIM // 2, 1), pltpu.roll(blk, ROT_DIM // 2, 1))
        return blk * cos + partner * sin

    a = proj[:, 2 * QK_W + V_W: 2 * QK_W + V_W + CONV_CH]
    gl = proj[:, 2 * QK_W + V_W + CONV_CH:]
    if head_major:
        qb_ref, k32_ref, v32_ref, kb_ref, vb_ref, u_ref = out_refs
    else:
        q32_ref, k32_ref, v32_ref, u_ref = out_refs
    u_ref[...] = a * jax.nn.sigmoid(gl)
    for h in range(N_HEADS):
        lo, hi = h * LANES, (h + 1) * LANES
        qh = rope(proj[:, lo:hi]) * Q_SCALE
        kh = rope(proj[:, QK_W + lo: QK_W + hi])
        vh = proj[:, 2 * QK_W + lo: 2 * QK_W + hi]
        v32_ref[pl.ds(h, x.shape[0], stride=N_HEADS), :] = vh
        if head_major:
            k32_ref[0, 2 * h:2 * h + 2] = kh.T.reshape(2, DQK, x.shape[0])
            qb_ref[h] = qh.astype(BF16)
            kb_ref[h] = kh.astype(BF16)
            vb_ref[h] = vh.astype(BF16)
        else:
            k32_ref[:, lo:hi] = kh
            q32_ref[:, lo:hi] = qh


def _inproj(x, norm_g, sc, sh, tabs, w_in_bf16, tiles_per_group, tiles_per_seq, head_major):
    n = x.shape[0]
    nt = n // TM
    r = sc.shape[1]
    row = lambda i: (i, 0)
    grp = lambda i: (i // tiles_per_group, 0, 0)
    tab = lambda i: (i % tiles_per_seq, 0)
    in_specs = [
        pl.BlockSpec((TM, D_MODEL), row),
        pl.BlockSpec((1, D_MODEL), lambda i: (0, 0)),
        pl.BlockSpec((1, r, D_MODEL), grp),
        pl.BlockSpec((1, r, D_MODEL), grp),
        pl.BlockSpec((TM, LANES), tab),
        pl.BlockSpec((TM, LANES), tab),
        pl.BlockSpec((D_MODEL, IN_WIDTH), lambda i: (0, 0)),
    ]
    wide = pl.BlockSpec((TM, QK_W), row)
    hm = pl.BlockSpec((N_HEADS, TM, LANES), lambda i: (0, i, 0))
    v_rows = pl.BlockSpec((TM * N_HEADS, DV), row)
    v_shape = jax.ShapeDtypeStruct((n * N_HEADS, DV), F32)
    wide_shape = jax.ShapeDtypeStruct((n, QK_W), F32)
    if head_major:
        n_seq = nt // tiles_per_seq
        k_t = pl.BlockSpec((1, 2 * N_HEADS, DQK, TM), lambda i: (i // tiles_per_seq, 0, 0, i % tiles_per_seq))
        out_specs = [hm, k_t, v_rows, hm, hm, wide]
        out_shape = [
            jax.ShapeDtypeStruct((N_HEADS, n, LANES), BF16),
            jax.ShapeDtypeStruct((n_seq, 2 * N_HEADS, DQK, tiles_per_seq * TM), F32),
            v_shape,
            jax.ShapeDtypeStruct((N_HEADS, n, LANES), BF16),
            jax.ShapeDtypeStruct((N_HEADS, n, LANES), BF16),
            wide_shape,
        ]
    else:
        out_specs = [wide, wide, v_rows, wide]
        out_shape = [wide_shape, wide_shape, v_shape, wide_shape]
    return pl.pallas_call(
        functools.partial(_inproj_kernel, head_major=head_major),
        grid=(nt,),
        in_specs=in_specs,
        out_specs=out_specs,
        out_shape=out_shape,
        compiler_params=_cparams(("parallel",)),
        name="inproj_hm" if head_major else "inproj",
    )(x, norm_g.reshape(1, D_MODEL), sc, sh, *tabs, w_in_bf16)


def _lambda_value(lq1, lk1, lq2, lk2):
    a = jnp.exp(jnp.sum(lq1[...] * lk1[...], axis=-1, keepdims=True))
    b = jnp.exp(jnp.sum(lq2[...] * lk2[...], axis=-1, keepdims=True))
    return a - b + LAM_INIT


def _diff_merge(o1, l1, o2, l2, lam, subln_g):
    o = o1 / l1 - lam * (o2 / l2)
    o = o * lax.rsqrt(jnp.mean(o * o, axis=-1, keepdims=True) + EPS) * subln_g
    return o * (1.0 - LAM_INIT)


def _attn_kernel(qt_ref, kt_ref, q_ref, k_ref, v_ref, lq1, lk1, lq2, lk2, sg_ref, o_ref, qs, m_s, l_s, acc):
    s_idx = pl.program_id(1)
    qi = qt_ref[s_idx]
    ki = kt_ref[s_idx]

    @pl.when(ki == 0)
    def _():
        for h in range(N_HEADS):
            q = q_ref[h]
            lane = lax.broadcasted_iota(I32, q.shape, 1)
            zero = jnp.zeros_like(q)
            qs[h, 0:TQ, :] = jnp.where(lane < DQK, q, zero)
            qs[h, TQ:2 * TQ, :] = jnp.where(lane >= DQK, q, zero)
        m_s[...] = jnp.full(m_s.shape, NEG_INF, F32)
        l_s[...] = jnp.zeros(l_s.shape, F32)
        acc[...] = jnp.zeros(acc.shape, F32)

    def update(h, masked):
        s = lax.dot_general(qs[h], k_ref[h], (((1,), (1,)), ((), ())), preferred_element_type=F32)
        if masked:
            row = lax.broadcasted_iota(I32, s.shape, 0) & (TQ - 1)
            col = lax.broadcasted_iota(I32, s.shape, 1)
            s = jnp.where(row >= col, s, NEG_INF)
        m_prev = m_s[h]
        m_next = jnp.maximum(m_prev, jnp.max(s, axis=1, keepdims=True))
        p = jnp.exp2(s - jnp.tile(m_next, (1, TK // LANES)))
        alpha = jnp.exp2(m_prev - m_next)
        v_ones = jnp.concatenate([v_ref[h], jnp.ones((TK, LANES), BF16)], axis=1)
        pv = jnp.dot(p.astype(BF16), v_ones, preferred_element_type=F32)
        l_s[h] = alpha * l_s[h] + pv[:, DV:]
        acc[h] = alpha * acc[h] + pv[:, :DV]
        m_s[h] = m_next

    @pl.when(ki < qi)
    def _():
        for h in range(N_HEADS):
            update(h, False)

    @pl.when(ki == qi)
    def _():
        lam = _lambda_value(lq1, lk1, lq2, lk2)
        for h in range(N_HEADS):
            update(h, True)
            o = _diff_merge(acc[h, 0:TQ, :], l_s[h, 0:TQ, :], acc[h, TQ:2 * TQ, :], l_s[h, TQ:2 * TQ, :],
                            lam, sg_ref[...])
            o_ref[:, h * DV:(h + 1) * DV] = o.astype(o_ref.dtype)


def _prompt_attention(q_hm, k_hm, v_hm, lams, subln_g, batch, seq):
    nq = seq // TQ
    pairs = [(qi, ki) for qi in range(nq) for ki in range(qi + 1)]
    qt = jnp.asarray([p[0] for p in pairs], I32)
    kt = jnp.asarray([p[1] for p in pairs], I32)
    n = batch * seq
    vec = lambda b, s, qt, kt: (0, 0)
    grid_spec = pltpu.PrefetchScalarGridSpec(
        num_scalar_prefetch=2,
        grid=(batch, len(pairs)),
        in_specs=[
            pl.BlockSpec((N_HEADS, TQ, LANES), lambda b, s, qt, kt: (0, b * nq + qt[s], 0)),
            pl.BlockSpec((N_HEADS, TK, LANES), lambda b, s, qt, kt: (0, b * nq + kt[s], 0)),
            pl.BlockSpec((N_HEADS, TK, LANES), lambda b, s, qt, kt: (0, b * nq + kt[s], 0)),
            pl.BlockSpec((1, DQK), vec), pl.BlockSpec((1, DQK), vec),
            pl.BlockSpec((1, DQK), vec), pl.BlockSpec((1, DQK), vec),
            pl.BlockSpec((1, DV), vec),
        ],
        out_specs=pl.BlockSpec((TQ, N_HEADS * DV), lambda b, s, qt, kt: (b * nq + qt[s], 0)),
        scratch_shapes=[
            pltpu.VMEM((N_HEADS, 2 * TQ, LANES), BF16),
            pltpu.VMEM((N_HEADS, 2 * TQ, LANES), F32),
            pltpu.VMEM((N_HEADS, 2 * TQ, LANES), F32),
            pltpu.VMEM((N_HEADS, 2 * TQ, LANES), F32),
        ],
    )
    return pl.pallas_call(
        _attn_kernel,
        grid_spec=grid_spec,
        out_shape=jax.ShapeDtypeStruct((n, N_HEADS * DV), BF16),
        compiler_params=_cparams(("parallel", "arbitrary")),
        name="prompt_attn",
    )(qt, kt, q_hm, k_hm, v_hm, *lams, subln_g.reshape(1, DV))


def _paged_attn_kernel(pt_ref, q_ref, kn_ref, vn_ref, lq1, lk1, lq2, lk2, sg_ref, *rest, n_steps, ts):
    kp = rest[:PAGES_PER_STEP]
    vp = rest[PAGES_PER_STEP:2 * PAGES_PER_STEP]
    o_ref, qe, m_s, l_s, acc = rest[2 * PAGES_PER_STEP:]
    n_maps = 2 * N_HEADS
    rows = n_maps * ts
    p_idx = pl.program_id(1)

    @pl.when(p_idx == 0)
    def _():
        qt = jnp.concatenate([q_ref[...]] * n_maps, axis=0)
        row = lax.broadcasted_iota(I32, qt.shape, 0)
        col = lax.broadcasted_iota(I32, qt.shape, 1)
        same_map = (row >> (ts.bit_length() - 1)) == (col >> (DQK.bit_length() - 1))
        qe[...] = jnp.where(same_map, qt, 0.0).astype(BF16)
        m_s[...] = jnp.full(m_s.shape, NEG_INF, F32)
        l_s[...] = jnp.zeros(l_s.shape, F32)
        acc[...] = jnp.zeros(acc.shape, F32)

    def update(kmat, vmat, causal, k_transposed):
        if k_transposed:
            s = jnp.dot(qe[...], kmat, preferred_element_type=F32)
        else:
            s = lax.dot_general(qe[...], kmat, (((1,), (1,)), ((), ())), preferred_element_type=F32)
        if causal:
            row = lax.broadcasted_iota(I32, s.shape, 0) & (ts - 1)
            col = lax.broadcasted_iota(I32, s.shape, 1)
            s = jnp.where(row >= col, s, NEG_INF)
        m_prev = m_s[...]
        m_next = jnp.maximum(m_prev, jnp.max(s, axis=1, keepdims=True))
        p = jnp.exp2(s - m_next[:, 0:1])
        alpha = jnp.exp2(m_prev - m_next)
        l_s[...] = alpha * l_s[...] + jnp.sum(p, axis=1, keepdims=True)
        acc[...] = jnp.tile(alpha, (1, V_W // LANES)) * acc[...] + jnp.dot(
            p.astype(BF16), vmat, preferred_element_type=F32)
        m_s[...] = m_next

    kmat = jnp.concatenate([r[...].reshape(QK_W, PAGE_SIZE) for r in kp], axis=1).astype(BF16)
    def heads_on_lanes(ref, n_pos):
        return jnp.concatenate([ref[pl.ds(h, n_pos, stride=N_HEADS), :] for h in range(N_HEADS)], axis=1)

    vmat = jnp.concatenate([heads_on_lanes(r, PAGE_SIZE) for r in vp], axis=0).astype(BF16)
    update(kmat, vmat, False, True)

    @pl.when(p_idx == n_steps - 1)
    def _():
        update(kn_ref[...].astype(BF16), heads_on_lanes(vn_ref, ts).astype(BF16), True, False)
        lam = _lambda_value(lq1, lk1, lq2, lk2)
        for h in range(N_HEADS):
            r1, r2 = 2 * h * ts, (2 * h + 1) * ts
            c0, c1 = h * DV, (h + 1) * DV
            o = _diff_merge(acc[r1:r1 + ts, c0:c1], l_s[r1:r1 + ts, :],
                            acc[r2:r2 + ts, c0:c1], l_s[r2:r2 + ts, :], lam, sg_ref[...])
            o_ref[:, c0:c1] = o


def _sample_attention(q_s, k_s, v_s, cache_k, cache_v, page_table, lams, subln_g):
    db, n_pages = page_table.shape
    ts = q_s.shape[0] // db
    n_pool = cache_k.shape[1]
    ck = jnp.transpose(cache_k, (0, 1, 3, 4, 2)).reshape(n_pool, 2 * N_HEADS, DQK, PAGE_SIZE)
    cv = cache_v.reshape(n_pool, PAGE_SIZE * N_HEADS, DV)
    n_steps = n_pages // PAGES_PER_STEP
    pt = page_table.reshape(-1).astype(I32)
    vec = lambda b, p, pt: (0, 0)
    new = lambda b, p, pt: (b, 0)

    def page_spec(j, block):
        def idx(b, p, pt):
            return (pt[b * n_pages + p * PAGES_PER_STEP + j],) + (0,) * (len(block) - 1)
        return pl.BlockSpec(block, idx)

    k_block = (None, 2 * N_HEADS, DQK, PAGE_SIZE)
    v_block = (None, PAGE_SIZE * N_HEADS, DV)

    rows = 2 * N_HEADS * ts
    grid_spec = pltpu.PrefetchScalarGridSpec(
        num_scalar_prefetch=1,
        grid=(db, n_steps),
        in_specs=[
            pl.BlockSpec((ts, QK_W), new), pl.BlockSpec((ts, QK_W), new), pl.BlockSpec((ts * N_HEADS, DV), new),
            pl.BlockSpec((1, DQK), vec), pl.BlockSpec((1, DQK), vec),
            pl.BlockSpec((1, DQK), vec), pl.BlockSpec((1, DQK), vec),
            pl.BlockSpec((1, DV), vec),
        ] + [page_spec(j, k_block) for j in range(PAGES_PER_STEP)]
          + [page_spec(j, v_block) for j in range(PAGES_PER_STEP)],
        out_specs=pl.BlockSpec((ts, V_W), new),
        scratch_shapes=[
            pltpu.VMEM((rows, QK_W), BF16),
            pltpu.VMEM((rows, LANES), F32),
            pltpu.VMEM((rows, LANES), F32),
            pltpu.VMEM((rows, V_W), F32),
        ],
    )
    return pl.pallas_call(
        functools.partial(_paged_attn_kernel, n_steps=n_steps, ts=ts),
        grid_spec=grid_spec,
        out_shape=jax.ShapeDtypeStruct((db * ts, V_W), F32),
        compiler_params=_cparams(("parallel", "arbitrary")),
        name="paged_attn",
    )(pt, q_s, k_s, v_s, *lams, subln_g.reshape(1, DV), *([ck] * PAGES_PER_STEP), *([cv] * PAGES_PER_STEP))


ROW_CHUNKS = D_MODEL // LANES


def _store_token_rows(ref, x):
    t = x.shape[0]
    for j in range(ROW_CHUNKS):
        ref[pl.ds(j, t, stride=ROW_CHUNKS), :] = x[:, j * LANES:(j + 1) * LANES]


def _load_token_rows(ref, t, lead=()):
    return jnp.concatenate(
        [ref[lead + (pl.ds(j, t, stride=ROW_CHUNKS), slice(None))] for j in range(ROW_CHUNKS)], axis=1)


def _split_bf16(x):
    hi = x.astype(BF16)
    return hi, (x - hi.astype(F32)).astype(BF16)


def _conv_ln_swish(y, lng, lnb):
    mu = jnp.mean(y, axis=-1, keepdims=True)
    var = jnp.mean(jnp.square(y - mu), axis=-1, keepdims=True)
    yn = (y - mu) * lax.rsqrt(var + EPS) * lng + lnb
    return yn * jax.nn.sigmoid(yn)


N_MIX_PROMPT_INPUTS = 16


def _mix_kernel(*refs, prompt, tiles_per_seq, n_real):
    i = pl.program_id(0)

    @pl.when(i < n_real)
    def _():
        _mix_body(*refs, prompt=prompt, tiles_per_seq=tiles_per_seq)

    if prompt:
        @pl.when(i == n_real)
        def _():
            for ref in refs[N_MIX_PROMPT_INPUTS:N_MIX_PROMPT_INPUTS + 3]:
                ref[...] = jnp.zeros(ref.shape, ref.dtype)


def _mix_body(*refs, prompt, tiles_per_seq):
    if prompt:
        (o_ref, ucur_ref, uhalo_ref, x_ref, g1_ref, sc2_ref, sh2_ref, wdw_ref, bdw_ref, lng_ref, lnb_ref,
         wout_ref, n2g_ref, wr_ref, br_ref, basein_ref,
         hp1_ref, n2_ref, meta_ref, cnt_ref, ext, shifted, base) = refs
    else:
        (o_ref, uext_ref, x_ref, g1_ref, sc2_ref, sh2_ref, wdw_ref, bdw_ref, lng_ref, lnb_ref,
         wout_ref, n2g_ref, wr_ref, br_ref, basein_ref, hp1_in, n2_in, meta_in,
         hp1_ref, n2_ref, meta_ref, cnt_ref, base) = refs
    i = pl.program_id(0)

    @pl.when(i == 0)
    def _():
        base[...] = basein_ref[0:1, :]

    if prompt:
        first = (i % tiles_per_seq) == 0
        halo = uhalo_ref[...]
        ext[0:HALO, :] = jnp.where(first, jnp.zeros_like(halo), halo)
        ext[HALO:HALO + TM, :] = ucur_ref[...]
        off = HALO - (CONV_K - 1)
        chunks = []
        for phase in range(SUBLANES):
            q_max = max([(off + j) // SUBLANES for j in range(CONV_K) if (off + j) % SUBLANES == phase])
            span = TM + SUBLANES * q_max
            shifted[phase, 0:span, :] = ext[pl.ds(phase, span), :]
        for c in range(TM // CONV_CHUNK):
            a = jnp.zeros((CONV_CHUNK, CONV_CH), F32) + bdw_ref[...]
            for j in range(CONV_K):
                phase, q = (off + j) % SUBLANES, (off + j) // SUBLANES
                start = c * CONV_CHUNK + SUBLANES * q
                a = a + wdw_ref[j:j + 1, :] * shifted[phase, start:start + CONV_CHUNK, :]
            chunks.append(a)
        y = jnp.concatenate(chunks, axis=0)
    else:
        nb, text, _ = uext_ref.shape
        ts = text - (CONV_K - 1)
        a = jnp.zeros((nb, ts, CONV_CH), F32) + bdw_ref[...]
        for j in range(CONV_K):
            a = a + wdw_ref[j:j + 1, :] * uext_ref[:, j:j + ts, :]
        y = a.reshape(nb * ts, CONV_CH)
    yc = _conv_ln_swish(y, lng_ref[...], lnb_ref[...])

    proj = (jnp.dot(o_ref[...].astype(BF16), wout_ref[0:N_HEADS * DV, :], preferred_element_type=F32)
            + jnp.dot(yc.astype(BF16), wout_ref[N_HEADS * DV:, :], preferred_element_type=F32))
    hp1 = x_ref[...] + g1_ref[0] * proj
    hp1_ref[...] = hp1
    n2 = hp1 * lax.rsqrt(jnp.mean(hp1 * hp1, axis=-1, keepdims=True) + EPS) * n2g_ref[...]
    n2 = n2 * (1.0 + sc2_ref[0]) + sh2_ref[0]
    _store_token_rows(n2_ref, n2)

    n_hi, n_lo = _split_bf16(n2)
    w_hi, w_lo = _split_bf16(wr_ref[...])
    logits = (jnp.dot(n_hi, w_hi, preferred_element_type=F32)
              + (jnp.dot(n_hi, w_lo, preferred_element_type=F32) + jnp.dot(n_lo, w_hi, preferred_element_type=F32))
              + br_ref[...])
    lane = lax.broadcasted_iota(I32, logits.shape, 1)
    lane_f = lane.astype(F32)
    lg = jnp.where(lane < N_EXPERTS, logits, NEG_INF)
    onehots, vals, idxs = [], [], []
    for _ in range(TOP_K):
        mx = jnp.max(lg, axis=-1, keepdims=True)
        idx = jnp.min(jnp.where(lg == mx, lane_f, float(LANES)), axis=-1, keepdims=True)
        oh = lane_f == idx
        lg = jnp.where(oh, NEG_INF, lg)
        onehots.append(oh)
        vals.append(mx)
        idxs.append(idx)
    exps = [jnp.exp(v - vals[0]) for v in vals]
    denom = exps[0] + exps[1] + exps[2] + exps[3]

    sel = jnp.zeros(logits.shape, F32)
    for oh in onehots:
        sel = sel + oh.astype(F32)
    r_i = lax.broadcasted_iota(I32, (TM, TM), 0)
    c_i = lax.broadcasted_iota(I32, (TM, TM), 1)
    ltri = (r_i > c_i).astype(BF16)
    before = jnp.dot(ltri, sel.astype(BF16), preferred_element_type=F32) + base[...]
    meta = jnp.zeros(logits.shape, F32)
    for k in range(TOP_K):
        rank = jnp.sum(jnp.where(onehots[k], before, 0.0), axis=-1, keepdims=True)
        meta = meta + jnp.where(lane == k, idxs[k], 0.0)
        meta = meta + jnp.where(lane == TOP_K + k, exps[k] / denom, 0.0)
        meta = meta + jnp.where(lane == 2 * TOP_K + k, rank, 0.0)
    meta_ref[...] = meta
    new_base = base[...] + jnp.sum(sel, axis=0, keepdims=True)
    base[...] = new_base
    cnt_ref[...] = jnp.broadcast_to(new_base, cnt_ref.shape)


def _mix_prompt(o_attn, u, x, g1, sc2, sh2, conv_w, w_out_bf16, norm2_g, w_router_pad, b_router_pad,
                base_in, n_total, tiles_per_seq):
    n = x.shape[0]
    nt = n // TM
    assert n_total == n + TM
    row = lambda i: (jnp.minimum(i, nt - 1), 0)
    out_row = lambda i: (i, 0)
    const2 = lambda i: (0, 0)
    grp = lambda i: (jnp.minimum(i, nt - 1) // tiles_per_seq, 0, 0)
    halo_row = lambda i: (jnp.maximum(jnp.minimum(i, nt - 1) * (TM // HALO) - 1, 0), 0)
    wdw, bdw, lng, lnb = conv_w
    in_specs = [
        pl.BlockSpec((TM, N_HEADS * DV), row),
        pl.BlockSpec((TM, CONV_CH), row),
        pl.BlockSpec((HALO, CONV_CH), halo_row),
        pl.BlockSpec((TM, D_MODEL), row),
        pl.BlockSpec((1, 1, D_MODEL), grp), pl.BlockSpec((1, 1, D_MODEL), grp), pl.BlockSpec((1, 1, D_MODEL), grp),
        pl.BlockSpec((CONV_K, CONV_CH), const2), pl.BlockSpec((1, CONV_CH), const2),
        pl.BlockSpec((1, CONV_CH), const2), pl.BlockSpec((1, CONV_CH), const2),
        pl.BlockSpec((D_MODEL, D_MODEL), const2),
        pl.BlockSpec((1, D_MODEL), const2),
        pl.BlockSpec((D_MODEL, LANES), const2), pl.BlockSpec((1, LANES), const2),
        pl.BlockSpec((SUBLANES, LANES), const2),
    ]
    assert len(in_specs) == N_MIX_PROMPT_INPUTS
    out_specs = [
        pl.BlockSpec((TM, D_MODEL), out_row),
        pl.BlockSpec((TM * ROW_CHUNKS, LANES), out_row),
        pl.BlockSpec((TM, LANES), out_row),
        pl.BlockSpec((SUBLANES, LANES), const2),
    ]
    out_shape = [
        jax.ShapeDtypeStruct((n_total, D_MODEL), F32),
        jax.ShapeDtypeStruct((n_total * ROW_CHUNKS, LANES), F32),
        jax.ShapeDtypeStruct((n_total, LANES), F32),
        jax.ShapeDtypeStruct((SUBLANES, LANES), F32),
    ]
    return pl.pallas_call(
        functools.partial(_mix_kernel, prompt=True, tiles_per_seq=tiles_per_seq, n_real=nt),
        grid=(nt + 1,),
        in_specs=in_specs,
        out_specs=out_specs,
        out_shape=out_shape,
        scratch_shapes=[
            pltpu.VMEM((HALO + TM, CONV_CH), F32),
            pltpu.VMEM((SUBLANES, HALO + TM, CONV_CH), F32),
            pltpu.VMEM((1, LANES), F32),
        ],
        compiler_params=_cparams(("arbitrary",)),
        name="mix_prompt",
    )(o_attn, u, u, x, g1, sc2, sh2, wdw, bdw.reshape(1, -1), lng.reshape(1, -1), lnb.reshape(1, -1),
      w_out_bf16, norm2_g.reshape(1, -1), w_router_pad, b_router_pad, base_in)


def _mix_sample(o_attn, uext, x, g1, sc2, sh2, conv_w, w_out_bf16, norm2_g, w_router_pad, b_router_pad,
                base_in, hp1_all, n2_all, meta_all, tile0):
    const2 = lambda i: (0, 0)
    const3 = lambda i: (0, 0, 0)
    out_row = lambda i: (tile0, 0)
    wdw, bdw, lng, lnb = conv_w
    nb, text, _ = uext.shape
    in_specs = [
        pl.BlockSpec((TM, N_HEADS * DV), const2),
        pl.BlockSpec((nb, text, CONV_CH), const3),
        pl.BlockSpec((TM, D_MODEL), const2),
        pl.BlockSpec((1, TM, D_MODEL), const3), pl.BlockSpec((1, TM, D_MODEL), const3),
        pl.BlockSpec((1, TM, D_MODEL), const3),
        pl.BlockSpec((CONV_K, CONV_CH), const2), pl.BlockSpec((1, CONV_CH), const2),
        pl.BlockSpec((1, CONV_CH), const2), pl.BlockSpec((1, CONV_CH), const2),
        pl.BlockSpec((D_MODEL, D_MODEL), const2),
        pl.BlockSpec((1, D_MODEL), const2),
        pl.BlockSpec((D_MODEL, LANES), const2), pl.BlockSpec((1, LANES), const2),
        pl.BlockSpec((SUBLANES, LANES), const2),
        pl.BlockSpec(memory_space=pl.ANY), pl.BlockSpec(memory_space=pl.ANY), pl.BlockSpec(memory_space=pl.ANY),
    ]
    out_specs = [
        pl.BlockSpec((TM, D_MODEL), out_row),
        pl.BlockSpec((TM * ROW_CHUNKS, LANES), out_row),
        pl.BlockSpec((TM, LANES), out_row),
        pl.BlockSpec((SUBLANES, LANES), const2),
    ]
    out_shape = [
        jax.ShapeDtypeStruct(hp1_all.shape, F32),
        jax.ShapeDtypeStruct(n2_all.shape, F32),
        jax.ShapeDtypeStruct(meta_all.shape, F32),
        jax.ShapeDtypeStruct((SUBLANES, LANES), F32),
    ]
    return pl.pallas_call(
        functools.partial(_mix_kernel, prompt=False, tiles_per_seq=1, n_real=1),
        grid=(1,),
        in_specs=in_specs,
        out_specs=out_specs,
        out_shape=out_shape,
        scratch_shapes=[pltpu.VMEM((1, LANES), F32)],
        input_output_aliases={15: 0, 16: 1, 17: 2},
        compiler_params=_cparams(("arbitrary",)),
        name="mix_sample",
    )(o_attn, uext, x, g1, sc2, sh2, wdw, bdw.reshape(1, -1), lng.reshape(1, -1), lnb.reshape(1, -1),
      w_out_bf16, norm2_g.reshape(1, -1), w_router_pad, b_router_pad, base_in, hp1_all, n2_all, meta_all)


MAX_SLOT_TILES = 16


def _lane_cumsum(x, lane):
    s = 1
    while s < LANES:
        x = x + jnp.where(lane >= s, pltpu.roll(x, s, 1), 0)
        s *= 2
    return x


def _slots_kernel(meta_ref, cnt_ref, slots_ref, tmap_ref, einfo_ref, *, n_tiles_pad):
    shift = TM.bit_length() - 1
    lane8 = lax.broadcasted_iota(I32, (SUBLANES, LANES), 1)
    cnt = cnt_ref[...].astype(I32)
    padded = ((cnt + (TM - 1)) >> shift) << shift
    csum = _lane_cumsum(padded, lane8)
    gstart = csum - padded

    meta = meta_ref[...]
    lane = lax.broadcasted_iota(I32, meta.shape, 1)
    lane_f = lane.astype(F32)
    gstart_f = gstart[0:1, :].astype(F32)
    out = jnp.zeros(meta.shape, F32)
    for k in range(TOP_K):
        sel = lane_f == meta[:, k:k + 1]
        gs = jnp.sum(jnp.where(sel, gstart_f, 0.0), axis=-1, keepdims=True)
        out = out + jnp.where(lane == k, gs + meta[:, 2 * TOP_K + k: 2 * TOP_K + k + 1], 0.0)
    slots_ref[...] = out.astype(I32)

    @pl.when(pl.program_id(0) == 0)
    def _():
        ctiles = csum[0:1, :] >> shift
        n_valid = jnp.max(ctiles, axis=-1, keepdims=True)
        t = lax.broadcasted_iota(I32, (n_tiles_pad, LANES), 0)
        t = jnp.minimum(t, n_valid - 1)
        lane_t = lax.broadcasted_iota(I32, (n_tiles_pad, LANES), 1)
        hit = jnp.where((lane_t < N_EXPERTS) & (ctiles <= t), 1, 0)
        te = jnp.sum(hit, axis=-1, keepdims=True)
        tmap_ref[...] = jnp.broadcast_to(jnp.minimum(te, N_EXPERTS - 1), tmap_ref.shape)
        row8 = lax.broadcasted_iota(I32, (SUBLANES, LANES), 0)
        info = jnp.where(row8 == 0, gstart, 0)
        info = info + jnp.where(row8 == 1, padded, 0)
        info = info + jnp.where(row8 == 2, cnt, 0)
        info = info + jnp.where(row8 == 3, jnp.broadcast_to(n_valid, (SUBLANES, LANES)), 0)
        einfo_ref[...] = info


def _routing_slots(meta_all, counts, n_tiles_pad):
    n = meta_all.shape[0]
    const2 = lambda i: (0, 0)
    n_tm = n // TM
    group = max(g for g in range(1, MAX_SLOT_TILES + 1) if n_tm % g == 0)
    rows = group * TM
    return pl.pallas_call(
        functools.partial(_slots_kernel, n_tiles_pad=n_tiles_pad),
        grid=(n // rows,),
        in_specs=[pl.BlockSpec((rows, LANES), lambda i: (i, 0)), pl.BlockSpec((SUBLANES, LANES), const2)],
        out_specs=[
            pl.BlockSpec((rows, LANES), lambda i: (i, 0)),
            pl.BlockSpec((n_tiles_pad, LANES), const2),
            pl.BlockSpec((SUBLANES, LANES), const2),
        ],
        out_shape=[
            jax.ShapeDtypeStruct((n, LANES), I32),
            jax.ShapeDtypeStruct((n_tiles_pad, LANES), I32),
            jax.ShapeDtypeStruct((SUBLANES, LANES), I32),
        ],
        compiler_params=_cparams(("arbitrary",)),
        name="routing_slots",
    )(meta_all, counts)


def _token_rows(ref, row):
    return ref.at[pl.ds(pl.multiple_of(row * ROW_CHUNKS, ROW_CHUNKS), ROW_CHUNKS)]


def _dispatch_kernel(gstart_ref, padded_ref, cnt_ref, nv_ref, slots_hbm, src_hbm, xs_hbm,
                     idx0, idx1, buf0, buf1, zbuf, sems, zsem, isems, lsems):
    i = pl.program_id(0)
    tile_rows = TM * ROW_CHUNKS
    n_slot_tiles = xs_hbm.shape[0] // tile_rows

    def zero_tile_copy(tile):
        start = pl.multiple_of(tile * tile_rows, tile_rows)
        return pltpu.make_async_copy(zbuf, xs_hbm.at[pl.ds(start, tile_rows)], zsem)

    def pad_tile_copy(e):
        return zero_tile_copy((gstart_ref[e] + padded_ref[e]) // TM - 1)

    @pl.when(i == 0)
    def _():
        zbuf[...] = jnp.zeros(zbuf.shape, zbuf.dtype)
        for e in range(N_EXPERTS):
            @pl.when(cnt_ref[e] > 0)
            def _():
                pad_tile_copy(e).start()

        def start_unused(t, carry):
            zero_tile_copy(t).start()
            return carry

        def wait_unused(t, carry):
            zero_tile_copy(t).wait()
            return carry

        lax.fori_loop(nv_ref[0], n_slot_tiles, start_unused, 0)
        for e in range(N_EXPERTS):
            @pl.when(cnt_ref[e] > 0)
            def _():
                pad_tile_copy(e).wait()
        lax.fori_loop(nv_ref[0], n_slot_tiles, wait_unused, 0)

    n_steps = pl.num_programs(0)
    idxs = (idx0, idx1)

    def idx_copy(step, parity):
        return pltpu.make_async_copy(slots_hbm.at[step], idxs[parity], isems.at[parity])

    bufs = (buf0, buf1)
    src_tile_rows = TM * ROW_CHUNKS

    def load_copy(step, parity):
        start = pl.multiple_of(step * src_tile_rows, src_tile_rows)
        return pltpu.make_async_copy(src_hbm.at[pl.ds(start, src_tile_rows)], bufs[parity], lsems.at[parity])

    def for_each_row(parity, fn):
        def body(r, carry):
            for k in range(TOP_K):
                fn(pltpu.make_async_copy(_token_rows(bufs[parity], r),
                                         _token_rows(xs_hbm, idxs[parity][0, r * TOP_K + k]), sems.at[parity]), k)
            return carry
        lax.fori_loop(0, TM, body, 0, unroll=8)

    @pl.when(i == 0)
    def _():
        load_copy(0, 0).start()
        idx_copy(0, 0).start()

    def step(parity):
        load_copy(i, parity).wait()
        idx_copy(i, parity).wait()
        for_each_row(parity, lambda c, k: c.start(priority=k % N_DMA_PRIORITIES))

        @pl.when(i > 0)
        def _():
            for_each_row(1 - parity, lambda c, k: c.wait())

        @pl.when(i + 1 < n_steps)
        def _():
            load_copy(i + 1, 1 - parity).start()
            idx_copy(i + 1, 1 - parity).start()

        @pl.when(i == n_steps - 1)
        def _():
            for_each_row(parity, lambda c, k: c.wait())

    for parity in range(2):
        @pl.when(i % 2 == parity)
        def _():
            step(parity)


def _dispatch(gstart, padded, cnt, n_valid, slots2d, n2_all, s_max):
    nt = slots2d.shape[0]
    grid_spec = pltpu.PrefetchScalarGridSpec(
        num_scalar_prefetch=4,
        grid=(nt,),
        in_specs=[
            pl.BlockSpec(memory_space=pl.ANY),
            pl.BlockSpec(memory_space=pl.ANY),
        ],
        out_specs=pl.BlockSpec(memory_space=pl.ANY),
        scratch_shapes=[
            pltpu.SMEM((1, TM * TOP_K), I32),
            pltpu.SMEM((1, TM * TOP_K), I32),
            pltpu.VMEM((TM * ROW_CHUNKS, LANES), F32),
            pltpu.VMEM((TM * ROW_CHUNKS, LANES), F32),
            pltpu.VMEM((TM * ROW_CHUNKS, LANES), F32),
            pltpu.SemaphoreType.DMA((2,)),
            pltpu.SemaphoreType.DMA,
            pltpu.SemaphoreType.DMA((2,)),
            pltpu.SemaphoreType.DMA((2,)),
        ],
    )
    return pl.pallas_call(
        _dispatch_kernel,
        grid_spec=grid_spec,
        out_shape=jax.ShapeDtypeStruct((s_max * ROW_CHUNKS, LANES), F32),
        compiler_params=pltpu.CompilerParams(dimension_semantics=("arbitrary",), has_side_effects=True),
        name="moe_dispatch",
    )(gstart, padded, cnt, n_valid, slots2d, n2_all)


DEINT_BLOCK = 2 * LANES
EXPERT_VMEM_LIMIT = 56 * 1024 * 1024


def _expert_kernel(te_ref, nv_ref, x_ref, wgu_hbm, wd_hbm, b1g_ref, b1u_ref, b2_ref, y_ref,
                   wgu_buf, wd_buf, w1g, w1u, w2, wsem, slot_ref):
    i = pl.program_id(0)
    n_valid = nv_ref[0]
    valid = i < n_valid
    expert = te_ref[i]
    new_expert = jnp.logical_or(i == 0, expert != te_ref[jnp.maximum(i - 1, 0)])
    last_tile = te_ref.shape[0] - 1

    def weight_copies(e, s):
        return (pltpu.make_async_copy(wgu_hbm.at[e], wgu_buf.at[s], wsem.at[0, s]),
                pltpu.make_async_copy(wd_hbm.at[e], wd_buf.at[s], wsem.at[1, s]))

    @pl.when(i == 0)
    def _():
        slot_ref[0] = 0
        for c in weight_copies(expert, 0):
            c.start()

    @pl.when(jnp.logical_and(valid, new_expert))
    def _():
        s = slot_ref[0]
        nxt = lax.while_loop(
            lambda j: jnp.logical_and(j < n_valid, te_ref[jnp.minimum(j, last_tile)] == expert),
            lambda j: j + 1, i + 1)

        @pl.when(nxt < n_valid)
        def _():
            for c in weight_copies(te_ref[jnp.minimum(nxt, last_tile)], 1 - s):
                c.start()

        for c in weight_copies(expert, s):
            c.wait()
        r = lax.broadcasted_iota(I32, (DEINT_BLOCK, DEINT_BLOCK), 0)
        c = lax.broadcasted_iota(I32, (DEINT_BLOCK, DEINT_BLOCK), 1)
        src_col = jnp.where(c < LANES, 2 * c, 2 * (c - LANES) + 1)
        perm = jnp.where(r == src_col, 1.0, 0.0).astype(BF16)
        for blk in range(wgu_buf.shape[2] // DEINT_BLOCK):
            cols = wgu_buf[s, :, blk * DEINT_BLOCK:(blk + 1) * DEINT_BLOCK].astype(BF16)
            split = jnp.dot(cols, perm, preferred_element_type=F32).astype(BF16)
            w1g[:, blk * LANES:(blk + 1) * LANES] = split[:, :LANES]
            w1u[:, blk * LANES:(blk + 1) * LANES] = split[:, LANES:]
        w2[...] = wd_buf[s].astype(BF16)
        slot_ref[0] = 1 - s

    @pl.when(valid)
    def _():
        x = _load_token_rows(x_ref, TM).astype(BF16)
        hg = jnp.dot(x, w1g[...], preferred_element_type=F32) + b1g_ref[0]
        hu = jnp.dot(x, w1u[...], preferred_element_type=F32) + b1u_ref[0]
        gate = jnp.minimum(hg, SWIGLU_LIMIT)
        up = jnp.clip(hu, -SWIGLU_LIMIT, SWIGLU_LIMIT)
        act = (up + 1.0) * gate * jax.nn.sigmoid(SWIGLU_ALPHA * gate)
        y = jnp.dot(act.astype(BF16), w2[...], preferred_element_type=F32) + b2_ref[0]
        _store_token_rows(y_ref, y)

    @pl.when(jnp.logical_not(valid))
    def _():
        y_ref[...] = jnp.zeros(y_ref.shape, y_ref.dtype)


def _expert_mlp(tile_expert, n_valid, xs, w_gate_up, w_down, b1g, b1u, b2):
    tile_rows = TM * ROW_CHUNKS
    nt = xs.shape[0] // tile_rows
    ff = w_down.shape[1]
    tile = lambda i, te, nv: (jnp.minimum(i, nv[0] - 1), 0)
    wsel = lambda i, te, nv: (te[i], 0, 0)
    grid_spec = pltpu.PrefetchScalarGridSpec(
        num_scalar_prefetch=2,
        grid=(nt,),
        in_specs=[
            pl.BlockSpec((tile_rows, LANES), tile),
            pl.BlockSpec(memory_space=pl.ANY),
            pl.BlockSpec(memory_space=pl.ANY),
            pl.BlockSpec((1, 1, ff), wsel),
            pl.BlockSpec((1, 1, ff), wsel),
            pl.BlockSpec((1, 1, D_MODEL), wsel),
        ],
        out_specs=pl.BlockSpec((tile_rows, LANES), lambda i, te, nv: (i, 0)),
        scratch_shapes=[
            pltpu.VMEM((2, D_MODEL, 2 * ff), F32),
            pltpu.VMEM((2, ff, D_MODEL), F32),
            pltpu.VMEM((D_MODEL, ff), BF16),
            pltpu.VMEM((D_MODEL, ff), BF16),
            pltpu.VMEM((ff, D_MODEL), BF16),
            pltpu.SemaphoreType.DMA((2, 2)),
            pltpu.SMEM((1,), I32),
        ],
    )
    return pl.pallas_call(
        _expert_kernel,
        grid_spec=grid_spec,
        out_shape=jax.ShapeDtypeStruct(xs.shape, F32),
        compiler_params=pltpu.CompilerParams(dimension_semantics=("arbitrary",),
                                             vmem_limit_bytes=EXPERT_VMEM_LIMIT),
        name="expert_mlp",
    )(tile_expert, n_valid, xs, w_gate_up, w_down, b1g, b1u, b2)


def _combine_kernel(slots_hbm, ys_hbm, hp1_ref, meta_ref, g2_ref, nf_ref, o_ref,
                    idx0, idx1, buf0, buf1, sems, isems, *, tile0):
    i = pl.program_id(0)
    n_steps = pl.num_programs(0)
    bufs = (buf0, buf1)
    idxs = (idx0, idx1)

    def idx_copy(step, parity):
        return pltpu.make_async_copy(slots_hbm.at[tile0 + step], idxs[parity], isems.at[parity])

    def for_each_row(parity, fn):
        def body(r, carry):
            for k in range(TOP_K):
                fn(pltpu.make_async_copy(_token_rows(ys_hbm, idxs[parity][0, r * TOP_K + k]),
                                         _token_rows(bufs[parity].at[k], r), sems.at[parity]), k)
            return carry
        lax.fori_loop(0, TM, body, 0, unroll=8)

    def start_rows(parity):
        for_each_row(parity, lambda c, k: c.start(priority=k % N_DMA_PRIORITIES))

    @pl.when(i == 0)
    def _():
        first = idx_copy(0, 0)
        first.start()
        first.wait()
        start_rows(0)

        @pl.when(n_steps > 1)
        def _():
            idx_copy(1, 1).start()

    def step(parity):
        @pl.when(i + 1 < n_steps)
        def _():
            idx_copy(i + 1, 1 - parity).wait()
            start_rows(1 - parity)

        for_each_row(parity, lambda c, k: c.wait())

        @pl.when(i + 2 < n_steps)
        def _():
            idx_copy(i + 2, parity).start()

        meta = meta_ref[...]
        f = jnp.zeros((TM, D_MODEL), F32)
        for k in range(TOP_K):
            f = f + meta[:, TOP_K + k: TOP_K + k + 1] * _load_token_rows(bufs[parity], TM, (k,))
        hp2 = hp1_ref[...] + g2_ref[0] * f
        o_ref[...] = hp2 * lax.rsqrt(jnp.mean(hp2 * hp2, axis=-1, keepdims=True) + EPS) * nf_ref[...]

    for parity in range(2):
        @pl.when(i % 2 == parity)
        def _():
            step(parity)


def _combine(slots2d, ys, hp1_all, meta_all, g2, normf_g, tile0, n_rows, tiles_per_group):
    nt = n_rows // TM
    r = g2.shape[1]
    row_in = lambda i: (tile0 + i, 0)
    return pl.pallas_call(
        functools.partial(_combine_kernel, tile0=tile0),
        grid=(nt,),
        in_specs=[
            pl.BlockSpec(memory_space=pl.ANY),
            pl.BlockSpec(memory_space=pl.ANY),
            pl.BlockSpec((TM, D_MODEL), row_in),
            pl.BlockSpec((TM, LANES), row_in),
            pl.BlockSpec((1, r, D_MODEL), lambda i: (i // tiles_per_group, 0, 0)),
            pl.BlockSpec((1, D_MODEL), lambda i: (0, 0)),
        ],
        out_specs=pl.BlockSpec((TM, D_MODEL), lambda i: (i, 0)),
        out_shape=jax.ShapeDtypeStruct((n_rows, D_MODEL), F32),
        scratch_shapes=[
            pltpu.SMEM((1, TM * TOP_K), I32),
            pltpu.SMEM((1, TM * TOP_K), I32),
            pltpu.VMEM((TOP_K, TM * ROW_CHUNKS, LANES), F32),
            pltpu.VMEM((TOP_K, TM * ROW_CHUNKS, LANES), F32),
            pltpu.SemaphoreType.DMA((2,)),
            pltpu.SemaphoreType.DMA((2,)),
        ],
        compiler_params=_cparams(("arbitrary",)),
        name="moe_combine",
    )(slots2d, ys, hp1_all, meta_all, g2, normf_g.reshape(1, D_MODEL))


def kernel(x_prompt, x_sample, cache_k, cache_v, state_conv, page_table, c_prompt, c_sample, norm1_g, norm2_g, w_ada, b_ada, w_in, lambda_q1, lambda_k1, lambda_q2, lambda_k2, subln_g, w_dw, b_dw, conv_ln_g, conv_ln_b, w_out, w_router, b_router, w_gate_up, b_gate_up, w_down, b_down, normf_g):
    depth = norm1_g.shape[0]
    assert depth == 1, "single-layer step"
    bsz, seq, d = x_prompt.shape
    db, ts, _ = x_sample.shape
    n_p, n_s = bsz * seq, db * ts
    assert d == D_MODEL and n_s == TM and seq % TQ == 0 and n_p % TM == 0
    n_all = n_p + n_s
    n_pages = page_table.shape[1]
    past = n_pages * cache_k.shape[2]
    tiles_per_seq = seq // TM
    l = 0

    n_cond = bsz + db
    c_all = jnp.concatenate([c_prompt, c_sample], axis=0)
    c_all = jnp.pad(c_all, ((0, -n_cond % SUBLANES), (0, 0)))
    mod = _adaln(c_all, w_ada[l], b_ada[l])[:n_cond]
    mod_p = mod[:bsz].reshape(bsz, 1, 6, D_MODEL)
    mod_s = mod[bsz:].reshape(db, 6, D_MODEL)
    sh1p, sc1p, g1p, sh2p, sc2p, g2p = [mod_p[:, :, j] for j in range(6)]
    sh1s, sc1s, g1s, sh2s, sc2s, g2s = [
        jnp.repeat(mod_s[:, j], ts, axis=0).reshape(1, n_s, D_MODEL) for j in range(6)]

    lams = [v[l].reshape(1, DQK) for v in (lambda_q1, lambda_k1, lambda_q2, lambda_k2)]
    w_in_b = w_in[l].astype(BF16)
    w_out_b = w_out[l].astype(BF16)
    conv_w = (w_dw[l], b_dw[l], conv_ln_g[l], conv_ln_b[l])
    w_router_pad = jnp.pad(w_router[l], ((0, 0), (0, LANES - N_EXPERTS)))
    b_router_pad = jnp.pad(b_router[l], (0, LANES - N_EXPERTS)).reshape(1, LANES)

    xp = x_prompt.reshape(n_p, D_MODEL)
    xs_tok = x_sample.reshape(n_s, D_MODEL)
    tabs_p = _rope_tables(jnp.arange(seq))
    tabs_s = _rope_tables(jnp.tile(past + jnp.arange(ts), db))
    q_hm, k_p, v_p, k_hm, v_hm, u_p = _inproj(xp, norm1_g[l], sc1p, sh1p, tabs_p, w_in_b,
                                              tiles_per_seq, tiles_per_seq, True)
    q_s, k_s, v_s, u_s = _inproj(xs_tok, norm1_g[l], sc1s, sh1s, tabs_s, w_in_b, 1, 1, False)

    o_p = _prompt_attention(q_hm, k_hm, v_hm, lams, subln_g[l], bsz, seq)
    o_s = _sample_attention(q_s, k_s, v_s, cache_k, cache_v, page_table, lams, subln_g[l])

    uext_s = jnp.concatenate([state_conv[l], u_s.reshape(db, ts, CONV_CH)], axis=1)

    zeros_base = jnp.zeros((SUBLANES, LANES), F32)
    hp1_all, n2_all, meta_all, cnt_p = _mix_prompt(
        o_p, u_p, xp, g1p, sc2p, sh2p, conv_w, w_out_b, norm2_g[l], w_router_pad, b_router_pad,
        zeros_base, n_all, tiles_per_seq)
    hp1_all, n2_all, meta_all, counts = _mix_sample(
        o_s, uext_s, xs_tok, g1s, sc2s, sh2s, conv_w, w_out_b, norm2_g[l], w_router_pad, b_router_pad,
        cnt_p, hp1_all, n2_all, meta_all, n_p // TM)

    n_tok_tiles = n_all // TM
    s_max = n_all * TOP_K + N_EXPERTS * TM
    n_slot_tiles = s_max // TM
    n_tiles_pad = -(-n_slot_tiles // SUBLANES) * SUBLANES
    slots, tmap, einfo = _routing_slots(meta_all, counts, n_tiles_pad)
    slots2d = slots[:, :TOP_K].reshape(n_tok_tiles, 1, TM * TOP_K)
    tile_expert = tmap[:n_slot_tiles, 0]
    gstart, padded, cnt = einfo[0, :N_EXPERTS], einfo[1, :N_EXPERTS], einfo[2, :N_EXPERTS]
    n_valid = einfo[3, :1]

    x_sorted = _dispatch(gstart, padded, cnt, n_valid, slots2d, n2_all, s_max)

    bgu = b_gate_up[l]
    b1g = bgu[:, 0::2].reshape(N_EXPERTS, 1, -1)
    b1u = bgu[:, 1::2].reshape(N_EXPERTS, 1, -1)
    b2 = b_down[l].reshape(N_EXPERTS, 1, D_MODEL)
    y_sorted = _expert_mlp(tile_expert, n_valid, x_sorted, w_gate_up.reshape(w_gate_up.shape[1:]),
                           w_down.reshape(w_down.shape[1:]), b1g, b1u, b2)

    y_p = _combine(slots2d, y_sorted, hp1_all, meta_all, g2p, normf_g, 0, n_p, tiles_per_seq)
    y_s = _combine(slots2d, y_sorted, hp1_all, meta_all, g2s, normf_g, n_p // TM, n_s, 1)

    y_prompt = y_p.reshape(bsz, seq, D_MODEL)
    y_sample = y_s.reshape(db, ts, D_MODEL)
    k_prompt = jnp.transpose(k_p, (0, 3, 1, 2))[None]
    v_prompt = v_p.reshape(1, bsz, seq, N_HEADS, DV)
    conv_prompt = u_p.reshape(bsz, seq, CONV_CH)[:, seq - (CONV_K - 1):][None]
    k_sample = k_s.reshape(1, db, ts, 2 * N_HEADS, DQK)
    v_sample = v_s.reshape(1, db, ts, N_HEADS, DV)
    conv_sample = uext_s[:, ts:][None]
    return (y_prompt, y_sample, k_prompt, v_prompt, conv_prompt, k_sample, v_sample, conv_sample)
```
